```python
import math
import jax, jax.numpy as jnp
from jax import lax
import numpy as np

D_MODEL = 2048
BATCH = 8
SEQ = 8192
DEPTH = 4

LRU_WIDTH = D_MODEL // 2
LRU_BLOCKS = 16
LRU_BLOCK = LRU_WIDTH // LRU_BLOCKS
LRU_CONV = 4
LRU_C = 8.0
HG_WIDTH = D_MODEL // 2
HG_EXPAND = 128
HG_HEADS = HG_WIDTH // HG_EXPAND
HG_DK = HG_EXPAND
HG_DV = HG_WIDTH // HG_HEADS
HG_CHUNK = 64
EVEN_IN = 2 * LRU_WIDTH + 4 * HG_WIDTH
MIX_WIDTH = LRU_WIDTH + HG_WIDTH
SSD_INNER = 2 * D_MODEL
SSD_HEADDIM = 64
SSD_HEADS = SSD_INNER // SSD_HEADDIM
SSD_GROUPS = 8
SSD_HPG = SSD_HEADS // SSD_GROUPS
SSD_STATE = 128
SSD_CONV = 4
SSD_CHUNK = 64
SSD_CONV_DIM = SSD_INNER + 2 * SSD_GROUPS * SSD_STATE
ODD_IN = SSD_INNER + SSD_CONV_DIM + SSD_HEADS
D_FF = 5632
FFN_CONV = 3
EPS = 1e-6
N_EVEN = (DEPTH + 1) // 2
N_ODD = DEPTH // 2

kernel_name = 'hybrid_rglru_hgrn2_mamba2_convffn'


def rmsnorm(x, w):
    xf = x.astype(jnp.float32)
    y = xf * lax.rsqrt(jnp.mean(xf * xf, axis=-1, keepdims=True) + EPS)
    return (y * w.astype(jnp.float32)).astype(x.dtype)


def causal_dwconv(x, w, b):
    k, c = w.shape
    y = lax.conv_general_dilated(x, w.astype(x.dtype)[:, None, :], window_strides=(1,),
                                 padding=[(k - 1, 0)], dimension_numbers=('NWC', 'WIO', 'NWC'),
                                 feature_group_count=c)
    return y + b.astype(x.dtype)


def to_chunks(t, chunk):
    return t.reshape(t.shape[0], t.shape[1] // chunk, chunk, *t.shape[2:])


def rg_lru(x, w_r, b_r, w_i, b_i, lam):
    bsz, s, w = x.shape
    xf = x.astype(jnp.float32)
    xb = xf.reshape(bsz, s, LRU_BLOCKS, LRU_BLOCK)
    r = jax.nn.sigmoid(jnp.einsum('bshi,hij->bshj', xb, w_r.astype(jnp.float32)).reshape(bsz, s, w) + b_r)
    gi = jax.nn.sigmoid(jnp.einsum('bshi,hij->bshj', xb, w_i.astype(jnp.float32)).reshape(bsz, s, w) + b_i)
    log_a = -LRU_C * r * jax.nn.softplus(-lam.astype(jnp.float32))
    a = jnp.exp(log_a)
    u = jnp.sqrt(-jnp.expm1(2.0 * log_a)) * (gi * xf)

    def combine(lhs, rhs):
        a1, b1 = lhs
        a2, b2 = rhs
        return a1 * a2, a2 * b1 + b2

    _, h = lax.associative_scan(combine, (a, u), axis=1)
    return h


def hgrn2(q, f_raw, v, lb):
    bsz, s, _ = q.shape
    lb = lb.astype(jnp.float32)
    f = lb + (1.0 - lb) * jax.nn.sigmoid(f_raw.astype(jnp.float32))
    k = 1.0 - f
    g = jnp.log(f)
    qs = jax.nn.silu(q.astype(jnp.float32)) * (HG_DK ** -0.5)
    hd = (HG_HEADS, HG_DK)
    qc = to_chunks(qs.reshape(bsz, s, *hd), HG_CHUNK)
    kc = to_chunks(k.reshape(bsz, s, *hd), HG_CHUNK)
    gc = to_chunks(g.reshape(bsz, s, *hd), HG_CHUNK)
    vc = to_chunks(v.astype(jnp.float32).reshape(bsz, s, HG_HEADS, HG_DV), HG_CHUNK)
    cum = jnp.cumsum(gc, axis=2)
    tot = cum[:, :, -1]
    mid = cum[:, :, HG_CHUNK // 2 - 1][:, :, None]
    causal = jnp.tril(jnp.ones((HG_CHUNK, HG_CHUNK), dtype=bool))
    scores = jnp.einsum('bnthd,bnshd->bnhts', qc * jnp.exp(cum - mid), kc * jnp.exp(mid - cum))
    scores = jnp.where(causal, scores, 0.0)
    o_intra = jnp.einsum('bnhts,bnshv->bnthv', scores, vc)

    def step(state, inp):
        q_c, k_c, v_c, cum_c, tot_c = inp
        o = jnp.einsum('bthd,bhdv->bthv', q_c * jnp.exp(cum_c), state)
        k_dec = k_c * jnp.exp(tot_c[:, None] - cum_c)
        new = jnp.exp(tot_c)[..., None] * state + jnp.einsum('bshd,bshv->bhdv', k_dec, v_c)
        return new, o

    state0 = jnp.zeros((bsz, HG_HEADS, HG_DK, HG_DV), jnp.float32)
    xs = tuple(jnp.moveaxis(t, 1, 0) for t in (qc, kc, vc, cum, tot))
    _, o_inter = lax.scan(step, state0, xs)
    o = o_intra + jnp.moveaxis(o_inter, 0, 1)
    return o.reshape(bsz, s, HG_HEADS, HG_DV)


def even_mixer(h, w_in, lru_conv_w, lru_conv_b, lru_w_r, lru_b_r, lru_w_i, lru_b_i, lru_lambda,
               lb, hg_norm_w, w_out):
    bsz, s, _ = h.shape
    proj = h @ w_in
    cuts = [LRU_WIDTH, 2 * LRU_WIDTH, 2 * LRU_WIDTH + HG_WIDTH,
            2 * LRU_WIDTH + 2 * HG_WIDTH, 2 * LRU_WIDTH + 3 * HG_WIDTH]
    xa, ga, qb, fb, ib, gb = jnp.split(proj, cuts, axis=-1)
    ha = rg_lru(causal_dwconv(xa, lru_conv_w, lru_conv_b), lru_w_r, lru_b_r, lru_w_i, lru_b_i, lru_lambda)
    ya = ha * jax.nn.gelu(ga.astype(jnp.float32))
    ob = hgrn2(qb, fb, ib, lb)
    ob = ob * lax.rsqrt(jnp.mean(ob * ob, axis=-1, keepdims=True) + EPS) * hg_norm_w.astype(jnp.float32)
    yb = ob.reshape(bsz, s, HG_WIDTH) * jax.nn.silu(gb.astype(jnp.float32))
    y = jnp.concatenate([ya, yb], axis=-1).astype(h.dtype)
    return y @ w_out


def ssd_scan(x, dt, a_neg, bm, cm):
    bsz, s, _ = x.shape
    xc = to_chunks(x.reshape(bsz, s, SSD_GROUPS, SSD_HPG, SSD_HEADDIM), SSD_CHUNK)
    dtc = to_chunks(dt.reshape(bsz, s, SSD_GROUPS, SSD_HPG), SSD_CHUNK)
    bc = to_chunks(bm, SSD_CHUNK)
    cc = to_chunks(cm, SSD_CHUNK)
    xdt = xc * dtc[..., None]
    cum = jnp.cumsum(dtc * a_neg.reshape(SSD_GROUPS, SSD_HPG), axis=2)
    tot = cum[:, :, -1]
    causal = jnp.tril(jnp.ones((SSD_CHUNK, SSD_CHUNK), dtype=bool))[:, :, None, None]
    seg = cum[:, :, :, None] - cum[:, :, None, :]
    decay = jnp.exp(jnp.where(causal, seg, -jnp.inf))
    cb = jnp.einsum('bntgk,bnsgk->bntsg', cc, bc)
    y_diag = jnp.einsum('bntsgh,bnsghp->bntghp', cb[..., None] * decay, xdt)

    def step(state, inp):
        c_c, b_c, xdt_c, cum_c, tot_c = inp
        y_off = jnp.einsum('btgk,bghpk->btghp', c_c, state) * jnp.exp(cum_c)[..., None]
        dec = jnp.exp(tot_c[:, None] - cum_c)
        new = jnp.exp(tot_c)[..., None, None] * state + jnp.einsum('bsgk,bsghp->bghpk', b_c, xdt_c * dec[..., None])
        return new, y_off

    state0 = jnp.zeros((bsz, SSD_GROUPS, SSD_HPG, SSD_HEADDIM, SSD_STATE), jnp.float32)
    xs = tuple(jnp.moveaxis(t, 1, 0) for t in (cc, bc, xdt, cum, tot))
    _, y_off = lax.scan(step, state0, xs)
    y = y_diag + jnp.moveaxis(y_off, 0, 1)
    return y, xc


def ssd_mixer(h, w_in, conv_w, conv_b, dt_bias, a_log, d_skip, norm_w, w_out):
    bsz, s, _ = h.shape
    proj = h @ w_in
    z, xbc, dt_raw = jnp.split(proj, [SSD_INNER, SSD_INNER + SSD_CONV_DIM], axis=-1)
    xbc = jax.nn.silu(causal_dwconv(xbc, conv_w, conv_b)).astype(jnp.float32)
    xs, bm, cm = jnp.split(xbc, [SSD_INNER, SSD_INNER + SSD_GROUPS * SSD_STATE], axis=-1)
    bm = bm.reshape(bsz, s, SSD_GROUPS, SSD_STATE)
    cm = cm.reshape(bsz, s, SSD_GROUPS, SSD_STATE)
    dt = jax.nn.softplus(dt_raw.astype(jnp.float32) + dt_bias.astype(jnp.float32))
    a_neg = -jnp.exp(a_log.astype(jnp.float32))
    y, xc = ssd_scan(xs, dt, a_neg, bm, cm)
    y = y + xc * d_skip.astype(jnp.float32).reshape(SSD_GROUPS, SSD_HPG)[..., None]
    y = y.reshape(bsz, s, SSD_INNER) * jax.nn.silu(z.astype(jnp.float32))
    y = y.reshape(bsz, s, SSD_GROUPS, SSD_INNER // SSD_GROUPS)
    y = y * lax.rsqrt(jnp.mean(y * y, axis=-1, keepdims=True) + EPS)
    y = y.reshape(bsz, s, SSD_INNER) * norm_w.astype(jnp.float32)
    return y.astype(h.dtype) @ w_out


def conv_ffn(h, w_up, conv_w, conv_b, w_down):
    u = causal_dwconv(h @ w_up, conv_w, conv_b)
    gate, val = jnp.split(u, 2, axis=-1)
    return (jax.nn.silu(gate) * val) @ w_down


def _fwd_setup_inputs(seed: int = 0) -> dict:
    key = jax.random.key(seed)
    ks = iter(jax.random.split(key, 48))

    def nrm(shape, scale):
        return scale * jax.random.normal(next(ks), shape, jnp.float32)

    def gain(shape):
        return 1.0 + 0.02 * jax.random.normal(next(ks), shape, jnp.float32)

    lam_u = jax.random.uniform(next(ks), (N_EVEN, LRU_WIDTH), jnp.float32, minval=0.9, maxval=0.999)
    dt0 = jnp.exp(jax.random.uniform(next(ks), (N_ODD, SSD_HEADS), jnp.float32,
                                     minval=math.log(1e-3), maxval=math.log(1e-1)))
    a0 = jax.random.uniform(next(ks), (N_ODD, SSD_HEADS), jnp.float32, minval=1.0, maxval=16.0)
    return {
        'x': nrm((BATCH, SEQ, D_MODEL), 1.0),
        'norm_mix_w': gain((DEPTH, D_MODEL)),
        'norm_ffn_w': gain((DEPTH, D_MODEL)),
        'norm_final_w': gain((D_MODEL,)),
        'ev_w_in': nrm((N_EVEN, D_MODEL, EVEN_IN), D_MODEL ** -0.5),
        'lru_conv_w': nrm((N_EVEN, LRU_CONV, LRU_WIDTH), LRU_CONV ** -0.5),
        'lru_conv_b': nrm((N_EVEN, LRU_WIDTH), 0.02),
        'lru_w_r': nrm((N_EVEN, LRU_BLOCKS, LRU_BLOCK, LRU_BLOCK), LRU_BLOCK ** -0.5),
        'lru_b_r': nrm((N_EVEN, LRU_WIDTH), 0.02),
        'lru_w_i': nrm((N_EVEN, LRU_BLOCKS, LRU_BLOCK, LRU_BLOCK), LRU_BLOCK ** -0.5),
        'lru_b_i': nrm((N_EVEN, LRU_WIDTH), 0.02),
        'lru_lambda': jnp.log(lam_u) - jnp.log1p(-lam_u),
        'hg_lower_bounds': nrm((N_EVEN, HG_WIDTH), 0.02),
        'hg_norm_w': gain((N_EVEN, HG_DV)),
        'ev_w_out': nrm((N_EVEN, MIX_WIDTH, D_MODEL), MIX_WIDTH ** -0.5),
        'ssd_w_in': nrm((N_ODD, D_MODEL, ODD_IN), D_MODEL ** -0.5),
        'ssd_conv_w': nrm((N_ODD, SSD_CONV, SSD_CONV_DIM), SSD_CONV ** -0.5),
        'ssd_conv_b': nrm((N_ODD, SSD_CONV_DIM), 0.02),
        'ssd_dt_bias': dt0 + jnp.log(-jnp.expm1(-dt0)),
        'ssd_a_log': jnp.log(a0),
        'ssd_d': gain((N_ODD, SSD_HEADS)),
        'ssd_norm_w': gain((N_ODD, SSD_INNER)),
        'ssd_w_out': nrm((N_ODD, SSD_INNER, D_MODEL), SSD_INNER ** -0.5),
        'ffn_w_up': nrm((DEPTH, D_MODEL, 2 * D_FF), D_MODEL ** -0.5),
        'ffn_conv_w': nrm((DEPTH, FFN_CONV, 2 * D_FF), FFN_CONV ** -0.5),
        'ffn_conv_b': nrm((DEPTH, 2 * D_FF), 0.02),
        'ffn_w_down': nrm((DEPTH, D_FF, D_MODEL), D_FF ** -0.5),
    }


def _fwd_reference(x, norm_mix_w, norm_ffn_w, norm_final_w, ev_w_in, lru_conv_w, lru_conv_b, lru_w_r,
              lru_b_r, lru_w_i, lru_b_i, lru_lambda, hg_lower_bounds, hg_norm_w, ev_w_out,
              ssd_w_in, ssd_conv_w, ssd_conv_b, ssd_dt_bias, ssd_a_log, ssd_d, ssd_norm_w, ssd_w_out,
              ffn_w_up, ffn_conv_w, ffn_conv_b, ffn_w_down):
    lbp = jax.nn.softmax(hg_lower_bounds.astype(jnp.float32), axis=0)
    lbs = jnp.cumsum(lbp, axis=0) - lbp[0]
    for l in range(DEPTH):
        h = rmsnorm(x, norm_mix_w[l])
        if l % 2 == 0:
            e = l // 2
            mix = even_mixer(h, ev_w_in[e], lru_conv_w[e], lru_conv_b[e], lru_w_r[e], lru_b_r[e],
                             lru_w_i[e], lru_b_i[e], lru_lambda[e], lbs[e], hg_norm_w[e], ev_w_out[e])
        else:
            o = l // 2
            mix = ssd_mixer(h, ssd_w_in[o], ssd_conv_w[o], ssd_conv_b[o], ssd_dt_bias[o], ssd_a_log[o],
                            ssd_d[o], ssd_norm_w[o], ssd_w_out[o])
        x = x + mix
        x = x + conv_ffn(rmsnorm(x, norm_ffn_w[l]), ffn_w_up[l], ffn_conv_w[l], ffn_conv_b[l], ffn_w_down[l])
    return rmsnorm(x, norm_final_w)


import jax as _jax
import jax.numpy as _jnp

TWIN_FORMAT = 'train_step'
FWD_PARAMS = ['x', 'norm_mix_w', 'norm_ffn_w', 'norm_final_w', 'ev_w_in', 'lru_conv_w', 'lru_conv_b', 'lru_w_r', 'lru_b_r', 'lru_w_i', 'lru_b_i', 'lru_lambda', 'hg_lower_bounds', 'hg_norm_w', 'ev_w_out', 'ssd_w_in', 'ssd_conv_w', 'ssd_conv_b', 'ssd_dt_bias', 'ssd_a_log', 'ssd_d', 'ssd_norm_w', 'ssd_w_out', 'ffn_w_up', 'ffn_conv_w', 'ffn_conv_b', 'ffn_w_down']
TWIN_WEIGHTS = ['norm_mix_w', 'norm_ffn_w', 'norm_final_w', 'ev_w_in', 'lru_conv_w', 'lru_conv_b', 'lru_w_r', 'lru_b_r', 'lru_w_i', 'lru_b_i', 'lru_lambda', 'hg_lower_bounds', 'hg_norm_w', 'ev_w_out', 'ssd_w_in', 'ssd_conv_w', 'ssd_conv_b', 'ssd_dt_bias', 'ssd_a_log', 'ssd_d', 'ssd_norm_w', 'ssd_w_out', 'ffn_w_up', 'ffn_conv_w', 'ffn_conv_b', 'ffn_w_down']
TWIN_DIFF_INPUT = 'x'
TWIN_INPUTS = ['x', 'norm_mix_w', 'norm_ffn_w', 'norm_final_w', 'ev_w_in', 'lru_conv_w', 'lru_conv_b', 'lru_w_r', 'lru_b_r', 'lru_w_i', 'lru_b_i', 'lru_lambda', 'hg_lower_bounds', 'hg_norm_w', 'ev_w_out', 'ssd_w_in', 'ssd_conv_w', 'ssd_conv_b', 'ssd_dt_bias', 'ssd_a_log', 'ssd_d', 'ssd_norm_w', 'ssd_w_out', 'ffn_w_up', 'ffn_conv_w', 'ffn_conv_b', 'ffn_w_down', 'loss_target', 'm_norm_mix_w', 'm_norm_ffn_w', 'm_norm_final_w', 'm_ev_w_in', 'm_lru_conv_w', 'm_lru_conv_b', 'm_lru_w_r', 'm_lru_b_r', 'm_lru_w_i', 'm_lru_b_i', 'm_lru_lambda', 'm_hg_lower_bounds', 'm_hg_norm_w', 'm_ev_w_out', 'm_ssd_w_in', 'm_ssd_conv_w', 'm_ssd_conv_b', 'm_ssd_dt_bias', 'm_ssd_a_log', 'm_ssd_d', 'm_ssd_norm_w', 'm_ssd_w_out', 'm_ffn_w_up', 'm_ffn_conv_w', 'm_ffn_conv_b', 'm_ffn_w_down', 'v_norm_mix_w', 'v_norm_ffn_w', 'v_norm_final_w', 'v_ev_w_in', 'v_lru_conv_w', 'v_lru_conv_b', 'v_lru_w_r', 'v_lru_b_r', 'v_lru_w_i', 'v_lru_b_i', 'v_lru_lambda', 'v_hg_lower_bounds', 'v_hg_norm_w', 'v_ev_w_out', 'v_ssd_w_in', 'v_ssd_conv_w', 'v_ssd_conv_b', 'v_ssd_dt_bias', 'v_ssd_a_log', 'v_ssd_d', 'v_ssd_norm_w', 'v_ssd_w_out', 'v_ffn_w_up', 'v_ffn_conv_w', 'v_ffn_conv_b', 'v_ffn_w_down']
TWIN_OUTPUTS = ['loss', 'grad_x', 'grad_norm_mix_w', 'grad_norm_ffn_w', 'grad_norm_final_w', 'grad_ev_w_in', 'grad_lru_conv_w', 'grad_lru_conv_b', 'grad_lru_w_r', 'grad_lru_b_r', 'grad_lru_w_i', 'grad_lru_b_i', 'grad_lru_lambda', 'grad_hg_lower_bounds', 'grad_hg_norm_w', 'grad_ev_w_out', 'grad_ssd_w_in', 'grad_ssd_conv_w', 'grad_ssd_conv_b', 'grad_ssd_dt_bias', 'grad_ssd_a_log', 'grad_ssd_d', 'grad_ssd_norm_w', 'grad_ssd_w_out', 'grad_ffn_w_up', 'grad_ffn_conv_w', 'grad_ffn_conv_b', 'grad_ffn_w_down', 'delta_norm_mix_w', 'delta_norm_ffn_w', 'delta_norm_final_w', 'delta_ev_w_in', 'delta_lru_conv_w', 'delta_lru_conv_b', 'delta_lru_w_r', 'delta_lru_b_r', 'delta_lru_w_i', 'delta_lru_b_i', 'delta_lru_lambda', 'delta_hg_lower_bounds', 'delta_hg_norm_w', 'delta_ev_w_out', 'delta_ssd_w_in', 'delta_ssd_conv_w', 'delta_ssd_conv_b', 'delta_ssd_dt_bias', 'delta_ssd_a_log', 'delta_ssd_d', 'delta_ssd_norm_w', 'delta_ssd_w_out', 'delta_ffn_w_up', 'delta_ffn_conv_w', 'delta_ffn_conv_b', 'delta_ffn_w_down', 'new_m_norm_mix_w', 'new_m_norm_ffn_w', 'new_m_norm_final_w', 'new_m_ev_w_in', 'new_m_lru_conv_w', 'new_m_lru_conv_b', 'new_m_lru_w_r', 'new_m_lru_b_r', 'new_m_lru_w_i', 'new_m_lru_b_i', 'new_m_lru_lambda', 'new_m_hg_lower_bounds', 'new_m_hg_norm_w', 'new_m_ev_w_out', 'new_m_ssd_w_in', 'new_m_ssd_conv_w', 'new_m_ssd_conv_b', 'new_m_ssd_dt_bias', 'new_m_ssd_a_log', 'new_m_ssd_d', 'new_m_ssd_norm_w', 'new_m_ssd_w_out', 'new_m_ffn_w_up', 'new_m_ffn_conv_w', 'new_m_ffn_conv_b', 'new_m_ffn_w_down', 'new_v_norm_mix_w', 'new_v_norm_ffn_w', 'new_v_norm_final_w', 'new_v_ev_w_in', 'new_v_lru_conv_w', 'new_v_lru_conv_b', 'new_v_lru_w_r', 'new_v_lru_b_r', 'new_v_lru_w_i', 'new_v_lru_b_i', 'new_v_lru_lambda', 'new_v_hg_lower_bounds', 'new_v_hg_norm_w', 'new_v_ev_w_out', 'new_v_ssd_w_in', 'new_v_ssd_conv_w', 'new_v_ssd_conv_b', 'new_v_ssd_dt_bias', 'new_v_ssd_a_log', 'new_v_ssd_d', 'new_v_ssd_norm_w', 'new_v_ssd_w_out', 'new_v_ffn_w_up', 'new_v_ffn_conv_w', 'new_v_ffn_conv_b', 'new_v_ffn_w_down']
TWIN_LEAF_KINDS = {'loss': 'loss', 'grad_x': 'grad_x', 'grad_norm_mix_w': 'grad_w', 'grad_norm_ffn_w': 'grad_w', 'grad_norm_final_w': 'grad_w', 'grad_ev_w_in': 'grad_w', 'grad_lru_conv_w': 'grad_w', 'grad_lru_conv_b': 'grad_w', 'grad_lru_w_r': 'grad_w', 'grad_lru_b_r': 'grad_w', 'grad_lru_w_i': 'grad_w', 'grad_lru_b_i': 'grad_w', 'grad_lru_lambda': 'grad_w', 'grad_hg_lower_bounds': 'grad_w', 'grad_hg_norm_w': 'grad_w', 'grad_ev_w_out': 'grad_w', 'grad_ssd_w_in': 'grad_w', 'grad_ssd_conv_w': 'grad_w', 'grad_ssd_conv_b': 'grad_w', 'grad_ssd_dt_bias': 'grad_w', 'grad_ssd_a_log': 'grad_w', 'grad_ssd_d': 'grad_w', 'grad_ssd_norm_w': 'grad_w', 'grad_ssd_w_out': 'grad_w', 'grad_ffn_w_up': 'grad_w', 'grad_ffn_conv_w': 'grad_w', 'grad_ffn_conv_b': 'grad_w', 'grad_ffn_w_down': 'grad_w', 'delta_norm_mix_w': 'delta_w', 'delta_norm_ffn_w': 'delta_w', 'delta_norm_final_w': 'delta_w', 'delta_ev_w_in': 'delta_w', 'delta_lru_conv_w': 'delta_w', 'delta_lru_conv_b': 'delta_w', 'delta_lru_w_r': 'delta_w', 'delta_lru_b_r': 'delta_w', 'delta_lru_w_i': 'delta_w', 'delta_lru_b_i': 'delta_w', 'delta_lru_lambda': 'delta_w', 'delta_hg_lower_bounds': 'delta_w', 'delta_hg_norm_w': 'delta_w', 'delta_ev_w_out': 'delta_w', 'delta_ssd_w_in': 'delta_w', 'delta_ssd_conv_w': 'delta_w', 'delta_ssd_conv_b': 'delta_w', 'delta_ssd_dt_bias': 'delta_w', 'delta_ssd_a_log': 'delta_w', 'delta_ssd_d': 'delta_w', 'delta_ssd_norm_w': 'delta_w', 'delta_ssd_w_out': 'delta_w', 'delta_ffn_w_up': 'delta_w', 'delta_ffn_conv_w': 'delta_w', 'delta_ffn_conv_b': 'delta_w', 'delta_ffn_w_down': 'delta_w', 'new_m_norm_mix_w': 'new_m', 'new_m_norm_ffn_w': 'new_m', 'new_m_norm_final_w': 'new_m', 'new_m_ev_w_in': 'new_m', 'new_m_lru_conv_w': 'new_m', 'new_m_lru_conv_b': 'new_m', 'new_m_lru_w_r': 'new_m', 'new_m_lru_b_r': 'new_m', 'new_m_lru_w_i': 'new_m', 'new_m_lru_b_i': 'new_m', 'new_m_lru_lambda': 'new_m', 'new_m_hg_lower_bounds': 'new_m', 'new_m_hg_norm_w': 'new_m', 'new_m_ev_w_out': 'new_m', 'new_m_ssd_w_in': 'new_m', 'new_m_ssd_conv_w': 'new_m', 'new_m_ssd_conv_b': 'new_m', 'new_m_ssd_dt_bias': 'new_m', 'new_m_ssd_a_log': 'new_m', 'new_m_ssd_d': 'new_m', 'new_m_ssd_norm_w': 'new_m', 'new_m_ssd_w_out': 'new_m', 'new_m_ffn_w_up': 'new_m', 'new_m_ffn_conv_w': 'new_m', 'new_m_ffn_conv_b': 'new_m', 'new_m_ffn_w_down': 'new_m', 'new_v_norm_mix_w': 'new_v', 'new_v_norm_ffn_w': 'new_v', 'new_v_norm_final_w': 'new_v', 'new_v_ev_w_in': 'new_v', 'new_v_lru_conv_w': 'new_v', 'new_v_lru_conv_b': 'new_v', 'new_v_lru_w_r': 'new_v', 'new_v_lru_b_r': 'new_v', 'new_v_lru_w_i': 'new_v', 'new_v_lru_b_i': 'new_v', 'new_v_lru_lambda': 'new_v', 'new_v_hg_lower_bounds': 'new_v', 'new_v_hg_norm_w': 'new_v', 'new_v_ev_w_out': 'new_v', 'new_v_ssd_w_in': 'new_v', 'new_v_ssd_conv_w': 'new_v', 'new_v_ssd_conv_b': 'new_v', 'new_v_ssd_dt_bias': 'new_v', 'new_v_ssd_a_log': 'new_v', 'new_v_ssd_d': 'new_v', 'new_v_ssd_norm_w': 'new_v', 'new_v_ssd_w_out': 'new_v', 'new_v_ffn_w_up': 'new_v', 'new_v_ffn_conv_w': 'new_v', 'new_v_ffn_conv_b': 'new_v', 'new_v_ffn_w_down': 'new_v'}


def _forward(args):
    return _fwd_reference(*[args[k] for k in FWD_PARAMS])


def _output_shape():
    def fwd():
        inp = _fwd_setup_inputs(0)
        return _fwd_reference(*[inp[k] for k in FWD_PARAMS])
    out = _jax.eval_shape(fwd)
    return out.shape, out.dtype

N_MICROBATCH = 1
ADAM_LR = 0.001
ADAM_B1 = 0.9
ADAM_B2 = 0.999
ADAM_EPS = 1e-08
ADAM_WD = 0.01
ADAM_STEP = 10
PER_EXAMPLE_BATCH_AXIS = {'x': 0, 'loss_target': 0}
SHARED_INPUTS = []
_WEIGHT_DTYPES = {'norm_mix_w': _jnp.float32, 'norm_ffn_w': _jnp.float32, 'norm_final_w': _jnp.float32, 'ev_w_in': _jnp.float32, 'lru_conv_w': _jnp.float32, 'lru_conv_b': _jnp.float32, 'lru_w_r': _jnp.float32, 'lru_b_r': _jnp.float32, 'lru_w_i': _jnp.float32, 'lru_b_i': _jnp.float32, 'lru_lambda': _jnp.float32, 'hg_lower_bounds': _jnp.float32, 'hg_norm_w': _jnp.float32, 'ev_w_out': _jnp.float32, 'ssd_w_in': _jnp.float32, 'ssd_conv_w': _jnp.float32, 'ssd_conv_b': _jnp.float32, 'ssd_dt_bias': _jnp.float32, 'ssd_a_log': _jnp.float32, 'ssd_d': _jnp.float32, 'ssd_norm_w': _jnp.float32, 'ssd_w_out': _jnp.float32, 'ffn_w_up': _jnp.float32, 'ffn_conv_w': _jnp.float32, 'ffn_conv_b': _jnp.float32, 'ffn_w_down': _jnp.float32}
MOMENT_SCALE = {'norm_mix_w': 1.228583e-01, 'norm_ffn_w': 9.341748e-02, 'norm_final_w': 3.198727e+01, 'ev_w_in': 6.587903e-02, 'lru_conv_w': 6.968263e-02, 'lru_conv_b': 3.117878e-01, 'lru_w_r': 1.512002e-02, 'lru_b_r': 1.581660e-02, 'lru_w_i': 2.656966e-02, 'lru_b_i': 2.207936e-02, 'lru_lambda': 3.200193e-02, 'hg_lower_bounds': 5.099307e-03, 'hg_norm_w': 2.702696e-01, 'ev_w_out': 7.843366e-02, 'ssd_w_in': 5.703515e-02, 'ssd_conv_w': 5.299756e-02, 'ssd_conv_b': 7.189211e-02, 'ssd_dt_bias': 1.082229e-01, 'ssd_a_log': 2.403262e-01, 'ssd_d': 3.504335e-01, 'ssd_norm_w': 6.185250e-02, 'ssd_w_out': 8.657569e-02, 'ffn_w_up': 3.939774e-02, 'ffn_conv_w': 3.954010e-02, 'ffn_conv_b': 3.908203e-02, 'ffn_w_down': 6.430130e-02}


def _to_microbatches(a, axis):
    t = _jnp.moveaxis(a, axis, 0)
    t = t.reshape((N_MICROBATCH, t.shape[0] // N_MICROBATCH) + t.shape[1:])
    return _jnp.moveaxis(t, 1, axis + 1)


def setup_inputs(seed: int = 0) -> dict:
    inp = _fwd_setup_inputs(seed)
    key = _jax.random.fold_in(_jax.random.key(seed), 7919)
    shape, _ = _output_shape()
    out = dict(inp)
    out["loss_target"] = _jax.random.normal(_jax.random.fold_in(key, 0), shape, _jnp.float32)
    for i, name in enumerate(TWIN_WEIGHTS):
        w = inp[name].astype(_jnp.float32)
        if MOMENT_SCALE is None:
            s = _jnp.sqrt(_jnp.mean(_jnp.square(w)) + 1e-30)
        else:
            s = MOMENT_SCALE[name]
        km, kv = _jax.random.split(_jax.random.fold_in(key, i + 1))
        out[name] = w
        out["m_" + name] = s * _jax.random.normal(km, w.shape, _jnp.float32)
        out["v_" + name] = (s * s) * _jax.random.uniform(kv, w.shape, _jnp.float32, 0.5, 1.5)
    if N_MICROBATCH > 1:
        for name, axis in PER_EXAMPLE_BATCH_AXIS.items():
            out[name] = _to_microbatches(out[name], axis)
    return {'x': out['x'], 'norm_mix_w': out['norm_mix_w'], 'norm_ffn_w': out['norm_ffn_w'], 'norm_final_w': out['norm_final_w'], 'ev_w_in': out['ev_w_in'], 'lru_conv_w': out['lru_conv_w'], 'lru_conv_b': out['lru_conv_b'], 'lru_w_r': out['lru_w_r'], 'lru_b_r': out['lru_b_r'], 'lru_w_i': out['lru_w_i'], 'lru_b_i': out['lru_b_i'], 'lru_lambda': out['lru_lambda'], 'hg_lower_bounds': out['hg_lower_bounds'], 'hg_norm_w': out['hg_norm_w'], 'ev_w_out': out['ev_w_out'], 'ssd_w_in': out['ssd_w_in'], 'ssd_conv_w': out['ssd_conv_w'], 'ssd_conv_b': out['ssd_conv_b'], 'ssd_dt_bias': out['ssd_dt_bias'], 'ssd_a_log': out['ssd_a_log'], 'ssd_d': out['ssd_d'], 'ssd_norm_w': out['ssd_norm_w'], 'ssd_w_out': out['ssd_w_out'], 'ffn_w_up': out['ffn_w_up'], 'ffn_conv_w': out['ffn_conv_w'], 'ffn_conv_b': out['ffn_conv_b'], 'ffn_w_down': out['ffn_w_down'], 'loss_target': out['loss_target'], 'm_norm_mix_w': out['m_norm_mix_w'], 'm_norm_ffn_w': out['m_norm_ffn_w'], 'm_norm_final_w': out['m_norm_final_w'], 'm_ev_w_in': out['m_ev_w_in'], 'm_lru_conv_w': out['m_lru_conv_w'], 'm_lru_conv_b': out['m_lru_conv_b'], 'm_lru_w_r': out['m_lru_w_r'], 'm_lru_b_r': out['m_lru_b_r'], 'm_lru_w_i': out['m_lru_w_i'], 'm_lru_b_i': out['m_lru_b_i'], 'm_lru_lambda': out['m_lru_lambda'], 'm_hg_lower_bounds': out['m_hg_lower_bounds'], 'm_hg_norm_w': out['m_hg_norm_w'], 'm_ev_w_out': out['m_ev_w_out'], 'm_ssd_w_in': out['m_ssd_w_in'], 'm_ssd_conv_w': out['m_ssd_conv_w'], 'm_ssd_conv_b': out['m_ssd_conv_b'], 'm_ssd_dt_bias': out['m_ssd_dt_bias'], 'm_ssd_a_log': out['m_ssd_a_log'], 'm_ssd_d': out['m_ssd_d'], 'm_ssd_norm_w': out['m_ssd_norm_w'], 'm_ssd_w_out': out['m_ssd_w_out'], 'm_ffn_w_up': out['m_ffn_w_up'], 'm_ffn_conv_w': out['m_ffn_conv_w'], 'm_ffn_conv_b': out['m_ffn_conv_b'], 'm_ffn_w_down': out['m_ffn_w_down'], 'v_norm_mix_w': out['v_norm_mix_w'], 'v_norm_ffn_w': out['v_norm_ffn_w'], 'v_norm_final_w': out['v_norm_final_w'], 'v_ev_w_in': out['v_ev_w_in'], 'v_lru_conv_w': out['v_lru_conv_w'], 'v_lru_conv_b': out['v_lru_conv_b'], 'v_lru_w_r': out['v_lru_w_r'], 'v_lru_b_r': out['v_lru_b_r'], 'v_lru_w_i': out['v_lru_w_i'], 'v_lru_b_i': out['v_lru_b_i'], 'v_lru_lambda': out['v_lru_lambda'], 'v_hg_lower_bounds': out['v_hg_lower_bounds'], 'v_hg_norm_w': out['v_hg_norm_w'], 'v_ev_w_out': out['v_ev_w_out'], 'v_ssd_w_in': out['v_ssd_w_in'], 'v_ssd_conv_w': out['v_ssd_conv_w'], 'v_ssd_conv_b': out['v_ssd_conv_b'], 'v_ssd_dt_bias': out['v_ssd_dt_bias'], 'v_ssd_a_log': out['v_ssd_a_log'], 'v_ssd_d': out['v_ssd_d'], 'v_ssd_norm_w': out['v_ssd_norm_w'], 'v_ssd_w_out': out['v_ssd_w_out'], 'v_ffn_w_up': out['v_ffn_w_up'], 'v_ffn_conv_w': out['v_ffn_conv_w'], 'v_ffn_conv_b': out['v_ffn_conv_b'], 'v_ffn_w_down': out['v_ffn_w_down']}


def _loss(weights, diff, rest, loss_target):
    with _jax.named_scope("forward"):
        args = {**rest, TWIN_DIFF_INPUT: diff, **{k: w.astype(_WEIGHT_DTYPES[k]) for k, w in weights.items()}}
        y = _forward(args)
    with _jax.named_scope("loss_head"):
        err = _jnp.square(y.astype(_jnp.float32) - loss_target)
        return 0.5 * _jnp.sum(_jnp.mean(err, axis=-1)) if err.ndim else 0.5 * err


def _adamw(w, g, m, v):
    m = ADAM_B1 * m + (1.0 - ADAM_B1) * g
    v = ADAM_B2 * v + (1.0 - ADAM_B2) * _jnp.square(g)
    m_hat = m / (1.0 - ADAM_B1 ** ADAM_STEP)
    v_hat = v / (1.0 - ADAM_B2 ** ADAM_STEP)
    delta = -ADAM_LR * (m_hat / (_jnp.sqrt(v_hat) + ADAM_EPS) + ADAM_WD * w)
    return delta, m, v


def reference(x, norm_mix_w, norm_ffn_w, norm_final_w, ev_w_in, lru_conv_w, lru_conv_b, lru_w_r, lru_b_r, lru_w_i, lru_b_i, lru_lambda, hg_lower_bounds, hg_norm_w, ev_w_out, ssd_w_in, ssd_conv_w, ssd_conv_b, ssd_dt_bias, ssd_a_log, ssd_d, ssd_norm_w, ssd_w_out, ffn_w_up, ffn_conv_w, ffn_conv_b, ffn_w_down, loss_target, m_norm_mix_w, m_norm_ffn_w, m_norm_final_w, m_ev_w_in, m_lru_conv_w, m_lru_conv_b, m_lru_w_r, m_lru_b_r, m_lru_w_i, m_lru_b_i, m_lru_lambda, m_hg_lower_bounds, m_hg_norm_w, m_ev_w_out, m_ssd_w_in, m_ssd_conv_w, m_ssd_conv_b, m_ssd_dt_bias, m_ssd_a_log, m_ssd_d, m_ssd_norm_w, m_ssd_w_out, m_ffn_w_up, m_ffn_conv_w, m_ffn_conv_b, m_ffn_w_down, v_norm_mix_w, v_norm_ffn_w, v_norm_final_w, v_ev_w_in, v_lru_conv_w, v_lru_conv_b, v_lru_w_r, v_lru_b_r, v_lru_w_i, v_lru_b_i, v_lru_lambda, v_hg_lower_bounds, v_hg_norm_w, v_ev_w_out, v_ssd_w_in, v_ssd_conv_w, v_ssd_conv_b, v_ssd_dt_bias, v_ssd_a_log, v_ssd_d, v_ssd_norm_w, v_ssd_w_out, v_ffn_w_up, v_ffn_conv_w, v_ffn_conv_b, v_ffn_w_down):
    given = dict(x=x, norm_mix_w=norm_mix_w, norm_ffn_w=norm_ffn_w, norm_final_w=norm_final_w, ev_w_in=ev_w_in, lru_conv_w=lru_conv_w, lru_conv_b=lru_conv_b, lru_w_r=lru_w_r, lru_b_r=lru_b_r, lru_w_i=lru_w_i, lru_b_i=lru_b_i, lru_lambda=lru_lambda, hg_lower_bounds=hg_lower_bounds, hg_norm_w=hg_norm_w, ev_w_out=ev_w_out, ssd_w_in=ssd_w_in, ssd_conv_w=ssd_conv_w, ssd_conv_b=ssd_conv_b, ssd_dt_bias=ssd_dt_bias, ssd_a_log=ssd_a_log, ssd_d=ssd_d, ssd_norm_w=ssd_norm_w, ssd_w_out=ssd_w_out, ffn_w_up=ffn_w_up, ffn_conv_w=ffn_conv_w, ffn_conv_b=ffn_conv_b, ffn_w_down=ffn_w_down, loss_target=loss_target, m_norm_mix_w=m_norm_mix_w, m_norm_ffn_w=m_norm_ffn_w, m_norm_final_w=m_norm_final_w, m_ev_w_in=m_ev_w_in, m_lru_conv_w=m_lru_conv_w, m_lru_conv_b=m_lru_conv_b, m_lru_w_r=m_lru_w_r, m_lru_b_r=m_lru_b_r, m_lru_w_i=m_lru_w_i, m_lru_b_i=m_lru_b_i, m_lru_lambda=m_lru_lambda, m_hg_lower_bounds=m_hg_lower_bounds, m_hg_norm_w=m_hg_norm_w, m_ev_w_out=m_ev_w_out, m_ssd_w_in=m_ssd_w_in, m_ssd_conv_w=m_ssd_conv_w, m_ssd_conv_b=m_ssd_conv_b, m_ssd_dt_bias=m_ssd_dt_bias, m_ssd_a_log=m_ssd_a_log, m_ssd_d=m_ssd_d, m_ssd_norm_w=m_ssd_norm_w, m_ssd_w_out=m_ssd_w_out, m_ffn_w_up=m_ffn_w_up, m_ffn_conv_w=m_ffn_conv_w, m_ffn_conv_b=m_ffn_conv_b, m_ffn_w_down=m_ffn_w_down, v_norm_mix_w=v_norm_mix_w, v_norm_ffn_w=v_norm_ffn_w, v_norm_final_w=v_norm_final_w, v_ev_w_in=v_ev_w_in, v_lru_conv_w=v_lru_conv_w, v_lru_conv_b=v_lru_conv_b, v_lru_w_r=v_lru_w_r, v_lru_b_r=v_lru_b_r, v_lru_w_i=v_lru_w_i, v_lru_b_i=v_lru_b_i, v_lru_lambda=v_lru_lambda, v_hg_lower_bounds=v_hg_lower_bounds, v_hg_norm_w=v_hg_norm_w, v_ev_w_out=v_ev_w_out, v_ssd_w_in=v_ssd_w_in, v_ssd_conv_w=v_ssd_conv_w, v_ssd_conv_b=v_ssd_conv_b, v_ssd_dt_bias=v_ssd_dt_bias, v_ssd_a_log=v_ssd_a_log, v_ssd_d=v_ssd_d, v_ssd_norm_w=v_ssd_norm_w, v_ssd_w_out=v_ssd_w_out, v_ffn_w_up=v_ffn_w_up, v_ffn_conv_w=v_ffn_conv_w, v_ffn_conv_b=v_ffn_conv_b, v_ffn_w_down=v_ffn_w_down)
    weights = {n: given[n] for n in TWIN_WEIGHTS}
    shared = {n: given[n] for n in SHARED_INPUTS}
    per_example = {n: given[n] for n in ['x']}
    grad_fn = _jax.value_and_grad(_loss, argnums=(0, 1))

    def one_microbatch(ex, loss_target):
        ex = dict(ex)
        diff = ex.pop(TWIN_DIFF_INPUT)
        return grad_fn(weights, diff, {**shared, **ex}, loss_target)

    if N_MICROBATCH == 1:
        loss, (grad_w, grad_x) = one_microbatch(per_example, given["loss_target"])
    else:
        def body(carry, xs):
            loss_sum, grad_sum = carry
            l_k, (gw_k, gx_k) = one_microbatch(xs[0], xs[1])
            with _jax.named_scope("update"):
                return (loss_sum + l_k, _jax.tree.map(_jnp.add, grad_sum, gw_k)), gx_k

        init = (_jnp.zeros((), _jnp.float32), _jax.tree.map(_jnp.zeros_like, weights))
        (loss, grad_w), grad_x = _jax.lax.scan(body, init, (per_example, given["loss_target"]))
    with _jax.named_scope("update"):
        delta_w, new_m, new_v = {}, {}, {}
        for n in TWIN_WEIGHTS:
            delta_w[n], new_m[n], new_v[n] = _adamw(weights[n], grad_w[n], given["m_" + n], given["v_" + n])
    return (loss, grad_x, *[grad_w[n] for n in TWIN_WEIGHTS], *[delta_w[n] for n in TWIN_WEIGHTS],
            *[new_m[n] for n in TWIN_WEIGHTS], *[new_v[n] for n in TWIN_WEIGHTS])
```

```python
import functools
import math

import jax
import jax.numpy as jnp
from jax import lax
from jax.experimental import pallas as pl
from jax.experimental.pallas import tpu as pltpu

f32 = jnp.float32
bf16 = jnp.bfloat16
_MXU = jnp.bfloat16
_HI = lax.Precision.HIGHEST

EPS = 1e-6
LRU_C = 8.0
LRU_BLOCK = 64
LRU_CONV = 4
HG_CHUNK = 64
SSD_GROUPS = 8
SSD_STATE = 128
SSD_CONV = 4
SSD_CHUNK = 64
FFN_CONV = 3
ADAM_LR, ADAM_B1, ADAM_B2, ADAM_EPS, ADAM_WD, ADAM_STEP = 0.001, 0.9, 0.999, 1e-08, 0.01, 10

_VMEM_LIMIT = 56 * 1024 * 1024
_LANES = 128
_HALO = 8

_TB_NORM = 512
_TB_FFN = 512
_TB_LRU = 256
_TB_HG = 512
_TB_SSD = 256
_TC_FFN = 512
_TC_LRU = 256
_MM_TM, _MM_TN, _MM_TK = 1024, 1024, 512
_COMM_COLS = 1024
_ADAM_ROWS = 512


def _pick(n, prefs):
    for p in prefs:
        if p <= n and n % p == 0:
            return p
    return n


def _pcall(body, *, name, grid, in_specs, out_specs, out_shape, scratch=(), sem=None, aliases=None, prefetch=0):
    params = pltpu.CompilerParams(dimension_semantics=sem, vmem_limit_bytes=_VMEM_LIMIT)
    if prefetch:
        gs = pltpu.PrefetchScalarGridSpec(num_scalar_prefetch=prefetch, grid=grid, in_specs=in_specs,
                                          out_specs=out_specs, scratch_shapes=list(scratch))
        return pl.pallas_call(body, name=name, grid_spec=gs, out_shape=out_shape,
                              input_output_aliases=aliases or {}, compiler_params=params)
    return pl.pallas_call(body, name=name, grid=grid, in_specs=in_specs, out_specs=out_specs, out_shape=out_shape,
                          scratch_shapes=list(scratch), input_output_aliases=aliases or {}, compiler_params=params)


def _sig(x):
    return jax.nn.sigmoid(x)


def _softplus(x):
    return jnp.maximum(x, 0.0) + jnp.log(1.0 + jnp.exp(-jnp.abs(x)))


def _neg_expm1(x):
    series = -x * (1.0 + x * (0.5 + x * (1.0 / 6.0 + x * (1.0 / 24.0 + x * (1.0 / 120.0)))))
    return jnp.where(x > -0.05, series, 1.0 - jnp.exp(x))


_GELU_C = math.sqrt(2.0 / math.pi)


def _gelu_and_grad(x):
    inner = _GELU_C * (x + 0.044715 * x * x * x)
    t = jnp.tanh(inner)
    g = 0.5 * x * (1.0 + t)
    dg = 0.5 * (1.0 + t) + 0.5 * x * (1.0 - t * t) * _GELU_C * (1.0 + 3.0 * 0.044715 * x * x)
    return g, dg


def _silu_and_grad(x):
    s = _sig(x)
    return x * s, s * (1.0 + x * (1.0 - s))


def _dot(a, b, dims=(((1,), (0,)), ((), ()))):
    return lax.dot_general(a.astype(_MXU), b.astype(_MXU), dims, preferred_element_type=f32)


def _dot_nt(a, b):
    return _dot(a, b, (((1,), (1,)), ((), ())))


def _dot_tn(a, b):
    return _dot(a, b, (((0,), (0,)), ((), ())))


def _dot_hi(a, b):
    return jnp.dot(a, b, preferred_element_type=f32, precision=_HI)


def _tri(n, lower=True):
    r = lax.broadcasted_iota(jnp.int32, (n, n), 0)
    c = lax.broadcasted_iota(jnp.int32, (n, n), 1)
    return (r >= c) if lower else (r <= c)


def _shift_down(cur, prev, s):
    if s == 0:
        return cur
    n = cur.shape[0]
    ext = jnp.concatenate([prev, cur], axis=0)
    return pltpu.roll(ext, s, axis=0)[_HALO:_HALO + n]


def _shift_up(cur, nxt, s):
    if s == 0:
        return cur
    n = cur.shape[0]
    ext = jnp.concatenate([cur, nxt], axis=0)
    return pltpu.roll(ext, n + _HALO - s, axis=0)[0:n]


def _conv_fwd(cur, prev, w, b, k):
    y = b + w[k - 1] * cur
    for j in range(k - 1):
        y = y + w[j] * _shift_down(cur, prev, k - 1 - j)
    return y


def _conv_bwd(dy, nxt_dy, cur, prev, w, k):
    dx = w[k - 1] * dy
    dws = []
    for j in range(k - 1):
        s = k - 1 - j
        dx = dx + w[j] * _shift_up(dy, nxt_dy, s)
        dws.append(jnp.sum(dy * _shift_down(cur, prev, s), axis=0, keepdims=True))
    dws.append(jnp.sum(dy * cur, axis=0, keepdims=True))
    return dx, dws, jnp.sum(dy, axis=0, keepdims=True)


def _rows(ref, k, cols=None):
    if cols is None:
        return [ref[j:j + 1, :] for j in range(k)]
    return [ref[j:j + 1, cols[0]:cols[1]] for j in range(k)]


def _add_rows(ref, rows, cols=None):
    for j, r in enumerate(rows):
        if cols is None:
            ref[j:j + 1, :] += r
        else:
            ref[j:j + 1, cols[0]:cols[1]] += r


def _mm(a, b, *, M, N, K, ta=False, tb=False, a_off=(0, 0), b_off=(0, 0), resid=None, out_dtype=f32, name):
    tm = _pick(M, (_MM_TM, 512, 256, 128))
    tn = _pick(N, (_MM_TN, 512, 256, 128))
    tk = _pick(K, (_MM_TK, 256, 128))
    nk = K // tk
    if ta:
        assert a_off[0] % tk == 0 and a_off[1] % tm == 0
        a_spec = pl.BlockSpec((tk, tm), lambda i, j, k: (k + a_off[0] // tk, i + a_off[1] // tm))
        a_dim = 0
    else:
        assert a_off[0] % tm == 0 and a_off[1] % tk == 0
        a_spec = pl.BlockSpec((tm, tk), lambda i, j, k: (i + a_off[0] // tm, k + a_off[1] // tk))
        a_dim = 1
    if tb:
        assert b_off[0] % tn == 0 and b_off[1] % tk == 0
        b_spec = pl.BlockSpec((tn, tk), lambda i, j, k: (j + b_off[0] // tn, k + b_off[1] // tk))
        b_dim = 1
    else:
        assert b_off[0] % tk == 0 and b_off[1] % tn == 0
        b_spec = pl.BlockSpec((tk, tn), lambda i, j, k: (k + b_off[0] // tk, j + b_off[1] // tn))
        b_dim = 0
    dims = (((a_dim,), (b_dim,)), ((), ()))
    has_resid = resid is not None

    def body(*refs):
        if has_resid:
            a_ref, b_ref, r_ref, o_ref, acc_ref = refs
        else:
            a_ref, b_ref, o_ref, acc_ref = refs
        kk = pl.program_id(2)

        @pl.when(kk == 0)
        def _():
            acc_ref[...] = jnp.zeros_like(acc_ref)

        acc_ref[...] += _dot(a_ref[...], b_ref[...], dims)

        @pl.when(kk == nk - 1)
        def _():
            r = acc_ref[...]
            if has_resid:
                r = r + r_ref[...].astype(f32)
            o_ref[...] = r.astype(out_dtype)

    in_specs = [a_spec, b_spec]
    args = [a, b]
    if has_resid:
        in_specs.append(pl.BlockSpec((tm, tn), lambda i, j, k: (i, j)))
        args.append(resid)
    return _pcall(body, name=name, grid=(M // tm, N // tn, nk), in_specs=in_specs,
                  out_specs=pl.BlockSpec((tm, tn), lambda i, j, k: (i, j)),
                  out_shape=jax.ShapeDtypeStruct((M, N), out_dtype),
                  scratch=[pltpu.VMEM((tm, tn), f32)], sem=("parallel", "parallel", "arbitrary"))(*args)


def _rmsnorm_fwd(x, w, name):
    T, D = x.shape
    tb = _pick(T, (_TB_NORM, 256, 128, 64))

    def body(x_ref, w_ref, o_ref):
        xv = x_ref[...]
        rstd = lax.rsqrt(jnp.mean(xv * xv, axis=-1, keepdims=True) + EPS)
        o_ref[...] = (xv * rstd * w_ref[...]).astype(o_ref.dtype)

    return _pcall(body, name=name, grid=(T // tb,),
                  in_specs=[pl.BlockSpec((tb, D), lambda i: (i, 0)), pl.BlockSpec((1, D), lambda i: (0, 0))],
                  out_specs=pl.BlockSpec((tb, D), lambda i: (i, 0)),
                  out_shape=jax.ShapeDtypeStruct((T, D), _MXU), sem=("parallel",))(x, w.reshape(1, D))


def _rmsnorm_bwd(dh, x, w, dres, name):
    T, D = x.shape
    tb = _pick(T, (_TB_NORM, 256, 128, 64))

    def body(dh_ref, x_ref, w_ref, dres_ref, dx_ref, dw_ref):
        xv = x_ref[...]
        rstd = lax.rsqrt(jnp.mean(xv * xv, axis=-1, keepdims=True) + EPS)
        xhat = xv * rstd
        dhv = dh_ref[...]
        g = dhv * w_ref[...]
        dx_ref[...] = dres_ref[...] + rstd * (g - xhat * jnp.mean(g * xhat, axis=-1, keepdims=True))

        @pl.when(pl.program_id(0) == 0)
        def _():
            dw_ref[...] = jnp.zeros_like(dw_ref)

        dw_ref[...] += jnp.sum(dhv * xhat, axis=0, keepdims=True)

    row = pl.BlockSpec((tb, D), lambda i: (i, 0))
    vec = pl.BlockSpec((1, D), lambda i: (0, 0))
    return _pcall(body, name=name, grid=(T // tb,), in_specs=[row, row, vec, row], out_specs=[row, vec],
                  out_shape=[jax.ShapeDtypeStruct((T, D), f32), jax.ShapeDtypeStruct((1, D), f32)],
                  sem=("arbitrary",))(dh, x, w.reshape(1, D), dres)


def _loss_head(x, w, target):
    T, D = x.shape
    tb = _pick(T, (_TB_NORM, 256, 128, 64))

    def body(x_ref, w_ref, t_ref, loss_ref, dx_ref, dw_ref):
        xv = x_ref[...]
        rstd = lax.rsqrt(jnp.mean(xv * xv, axis=-1, keepdims=True) + EPS)
        xhat = xv * rstd
        err = xhat * w_ref[...] - t_ref[...]
        dy = err * (1.0 / D)
        g = dy * w_ref[...]
        dx_ref[...] = rstd * (g - xhat * jnp.mean(g * xhat, axis=-1, keepdims=True))

        @pl.when(pl.program_id(0) == 0)
        def _():
            dw_ref[...] = jnp.zeros_like(dw_ref)
            loss_ref[...] = jnp.zeros_like(loss_ref)

        dw_ref[...] += jnp.sum(dy * xhat, axis=0, keepdims=True)
        part = 0.5 * jnp.sum(jnp.sum(err * err, axis=-1, keepdims=True) * (1.0 / D), axis=0, keepdims=True)
        loss_ref[...] += jnp.broadcast_to(part, loss_ref.shape)

    row = pl.BlockSpec((tb, D), lambda i: (i, 0))
    vec = pl.BlockSpec((1, D), lambda i: (0, 0))
    return _pcall(body, name="loss_head", grid=(T // tb,), in_specs=[row, vec, row],
                  out_specs=[pl.BlockSpec((1, _LANES), lambda i: (0, 0)), row, vec],
                  out_shape=[jax.ShapeDtypeStruct((1, _LANES), f32), jax.ShapeDtypeStruct((T, D), f32),
                             jax.ShapeDtypeStruct((1, D), f32)],
                  sem=("arbitrary",))(x, w.reshape(1, D), target)


def _halo_spec(tb, width, col_fn):
    r = tb // _HALO
    return pl.BlockSpec((_HALO, width), lambda j, i: (jnp.maximum(i * r - 1, 0), col_fn(j)))


def _ffn_mid_fwd(up, conv_w, conv_b, name):
    T, F2 = up.shape
    tc = _pick(F2 // 2, (_TC_FFN, 256, 128))
    tb = _pick(T, (_TB_FFN, 256, 128, 64))
    nj = F2 // (2 * tc)

    def body(cur_ref, prev_ref, w_ref, b_ref, o_ref):
        cur = cur_ref[...]
        prev = jnp.where(pl.program_id(1) > 0, prev_ref[...], 0.0)
        u = _conv_fwd(cur, prev, _rows(w_ref, FFN_CONV), b_ref[...], FFN_CONV)
        gate, val = u[:, :tc], u[:, tc:]
        o_ref[...] = (gate * _sig(gate) * val).astype(o_ref.dtype)

    return _pcall(body, name=name, grid=(nj, T // tb),
                  in_specs=[pl.BlockSpec((tb, 2 * tc), lambda j, i: (i, j)), _halo_spec(tb, 2 * tc, lambda j: j),
                            pl.BlockSpec((FFN_CONV, 2 * tc), lambda j, i: (0, j)),
                            pl.BlockSpec((1, 2 * tc), lambda j, i: (0, j))],
                  out_specs=pl.BlockSpec((tb, tc), lambda j, i: (i, j)),
                  out_shape=jax.ShapeDtypeStruct((T, F2 // 2), _MXU), sem=("parallel", "parallel"))(
        up, up, conv_w, conv_b)


def _ffn_mid_bwd(up, dact, conv_w, conv_b, name):
    T, F2 = up.shape
    tc = _pick(F2 // 2, (_TC_FFN, 256, 128))
    tb = _pick(T, (_TB_FFN, 256, 128, 64))
    nj, nt = F2 // (2 * tc), T // tb
    r = tb // _HALO

    def body(cur_ref, prev_ref, da_ref, w_ref, b_ref, dup_ref, dw_ref, db_ref, carry_ref):
        i = pl.program_id(1)
        ti = nt - 1 - i
        cur = cur_ref[...]
        prev = jnp.where(ti > 0, prev_ref[...], 0.0)
        w = _rows(w_ref, FFN_CONV)
        u = _conv_fwd(cur, prev, w, b_ref[...], FFN_CONV)
        gate, val = u[:, :tc], u[:, tc:]
        da = da_ref[...]
        sil, dsil = _silu_and_grad(gate)
        du = jnp.concatenate([da * val * dsil, da * sil], axis=1)

        @pl.when(i == 0)
        def _():
            carry_ref[...] = jnp.zeros_like(carry_ref)
            dw_ref[...] = jnp.zeros_like(dw_ref)
            db_ref[...] = jnp.zeros_like(db_ref)

        dx, dw, db = _conv_bwd(du, carry_ref[...], cur, prev, w, FFN_CONV)
        dup_ref[...] = dx.astype(dup_ref.dtype)
        _add_rows(dw_ref, dw)
        db_ref[...] += db
        carry_ref[...] = du[0:_HALO]

    return _pcall(body, name=name, grid=(nj, nt),
                  in_specs=[pl.BlockSpec((tb, 2 * tc), lambda j, i: (nt - 1 - i, j)),
                            pl.BlockSpec((_HALO, 2 * tc), lambda j, i: (jnp.maximum((nt - 1 - i) * r - 1, 0), j)),
                            pl.BlockSpec((tb, tc), lambda j, i: (nt - 1 - i, j)),
                            pl.BlockSpec((FFN_CONV, 2 * tc), lambda j, i: (0, j)),
                            pl.BlockSpec((1, 2 * tc), lambda j, i: (0, j))],
                  out_specs=[pl.BlockSpec((tb, 2 * tc), lambda j, i: (nt - 1 - i, j)),
                             pl.BlockSpec((FFN_CONV, 2 * tc), lambda j, i: (0, j)),
                             pl.BlockSpec((1, 2 * tc), lambda j, i: (0, j))],
                  out_shape=[jax.ShapeDtypeStruct((T, F2), _MXU), jax.ShapeDtypeStruct((FFN_CONV, F2), f32),
                             jax.ShapeDtypeStruct((1, F2), f32)],
                  scratch=[pltpu.VMEM((_HALO, 2 * tc), f32)], sem=("parallel", "arbitrary"))(
        up, up, dact, conv_w, conv_b)


def _scan_fwd(a, u):
    n = a.shape[0]
    row = lax.broadcasted_iota(jnp.int32, a.shape, 0)
    d = 1
    while d < n:
        a_s = jnp.where(row >= d, pltpu.roll(a, d, axis=0), 1.0)
        u_s = jnp.where(row >= d, pltpu.roll(u, d, axis=0), 0.0)
        u = u + a * u_s
        a = a * a_s
        d *= 2
    return a, u


def _scan_bwd(m, v):
    n = m.shape[0]
    row = lax.broadcasted_iota(jnp.int32, m.shape, 0)
    d = 1
    while d < n:
        ok = row + d < n
        m_s = jnp.where(ok, pltpu.roll(m, n - d, axis=0), 1.0)
        v_s = jnp.where(ok, pltpu.roll(v, n - d, axis=0), 0.0)
        v = v + m * v_s
        m = m * m_s
        d *= 2
    return v


def _lru_gates(xc, wr, br, wi, bi, lam):
    r = _sig(_dot(xc, wr) + br)
    gi = _sig(_dot(xc, wi) + bi)
    sp = _softplus(-lam)
    log_a = -LRU_C * r * sp
    a = jnp.exp(log_a)
    mult = jnp.sqrt(_neg_expm1(2.0 * log_a))
    return r, gi, sp, a, mult


def _lru_fwd(proj, conv_w, conv_b, wr, br, wi, bi, lam, lw, name):
    T = proj.shape[0]
    tc = _pick(lw, (_TC_LRU, 128))
    tb = _pick(T, (_TB_LRU, 128, 64))
    nj = lw // tc

    def body(cur_ref, prev_ref, cw_ref, cb_ref, wr_ref, br_ref, wi_ref, bi_ref, lam_ref, ya_ref, h_ref, carry_ref):
        i = pl.program_id(1)

        @pl.when(i == 0)
        def _():
            carry_ref[...] = jnp.zeros_like(carry_ref)

        xa = cur_ref[:, :tc]
        ga = cur_ref[:, tc:]
        prev = jnp.where(i > 0, prev_ref[:, :tc], 0.0)
        xc = _conv_fwd(xa, prev, _rows(cw_ref, LRU_CONV), cb_ref[...], LRU_CONV)
        r, gi, sp, a, mult = _lru_gates(xc, wr_ref[0], br_ref[...], wi_ref[0], bi_ref[...], lam_ref[...])
        acum, h = _scan_fwd(a, mult * gi * xc)
        h = h + acum * carry_ref[0:1, :]
        h_ref[...] = h
        gel, _ = _gelu_and_grad(ga)
        ya_ref[...] = (h * gel).astype(ya_ref.dtype)
        carry_ref[...] = pltpu.roll(h, 1, axis=0)[0:_HALO]

    vec = pl.BlockSpec((1, tc), lambda j, i: (0, j))
    mat = pl.BlockSpec((1, tc, tc), lambda j, i: (j, 0, 0))
    out = pl.BlockSpec((tb, tc), lambda j, i: (i, j))
    return _pcall(body, name=name, grid=(nj, T // tb),
                  in_specs=[pl.BlockSpec((tb, 2 * tc), lambda j, i: (i, j)), _halo_spec(tb, 2 * tc, lambda j: j),
                            pl.BlockSpec((LRU_CONV, tc), lambda j, i: (0, j)), vec, mat, vec, mat, vec, vec],
                  out_specs=[out, out],
                  out_shape=[jax.ShapeDtypeStruct((T, lw), _MXU), jax.ShapeDtypeStruct((T, lw), f32)],
                  scratch=[pltpu.VMEM((_HALO, tc), f32)], sem=("parallel", "arbitrary"))(
        proj, proj, conv_w, conv_b, wr, br, wi, bi, lam)


def _lru_bwd(proj, h, dy, conv_w, conv_b, wr, br, wi, bi, lam, lw, name):
    T, width = proj.shape
    tc = _pick(lw, (_TC_LRU, 128))
    tb = _pick(T, (_TB_LRU, 128, 64))
    nj, nt = lw // tc, T // tb
    rr = tb // _HALO

    def body(cur_ref, prev_ref, h_ref, hp_ref, dy_ref, cw_ref, cb_ref, wr_ref, br_ref, wi_ref, bi_ref, lam_ref,
             dp_ref, dcw_ref, dcb_ref, dwr_ref, dbr_ref, dwi_ref, dbi_ref, dlam_ref, cl_ref, cx_ref):
        i = pl.program_id(1)
        ti = nt - 1 - i

        @pl.when(i == 0)
        def _():
            for ref in (cl_ref, cx_ref, dcw_ref, dcb_ref, dwr_ref, dbr_ref, dwi_ref, dbi_ref, dlam_ref):
                ref[...] = jnp.zeros_like(ref)

        xa = cur_ref[:, :tc]
        ga = cur_ref[:, tc:]
        prev = jnp.where(ti > 0, prev_ref[:, :tc], 0.0)
        cw = _rows(cw_ref, LRU_CONV)
        xc = _conv_fwd(xa, prev, cw, cb_ref[...], LRU_CONV)
        wrv, wiv, lam_v = wr_ref[0], wi_ref[0], lam_ref[...]
        r, gi, sp, a, mult = _lru_gates(xc, wrv, br_ref[...], wiv, bi_ref[...], lam_v)
        hcur = h_ref[...]
        hprev = _shift_down(hcur, jnp.where(ti > 0, hp_ref[...], 0.0), 1)
        gel, dgel = _gelu_and_grad(ga)
        dya = dy_ref[...]
        dga = dya * hcur * dgel
        row = lax.broadcasted_iota(jnp.int32, a.shape, 0)
        v = dya * gel + jnp.where(row == tb - 1, cl_ref[0:1, :], 0.0)
        m = jnp.where(row < tb - 1, pltpu.roll(a, tb - 1, axis=0), 0.0)
        lamb = _scan_bwd(m, v)
        cl_ref[...] = (a * lamb)[0:_HALO]
        da = lamb * hprev
        dmult = lamb * gi * xc
        dgi = lamb * mult * xc
        dxc = lamb * mult * gi
        dlog_a = da * a - dmult * (a * a) / mult
        dr = dlog_a * (-LRU_C * sp)
        dlam_ref[...] += jnp.sum(dlog_a * (-LRU_C) * r, axis=0, keepdims=True) * (-_sig(-lam_v))
        dpr = dr * r * (1.0 - r)
        dpi = dgi * gi * (1.0 - gi)
        dbr_ref[...] += jnp.sum(dpr, axis=0, keepdims=True)
        dbi_ref[...] += jnp.sum(dpi, axis=0, keepdims=True)
        dwr_ref[0] += _dot_tn(xc, dpr)
        dwi_ref[0] += _dot_tn(xc, dpi)
        dxc = dxc + _dot_nt(dpr, wrv) + _dot_nt(dpi, wiv)
        dxa, dws, db = _conv_bwd(dxc, cx_ref[...], xa, prev, cw, LRU_CONV)
        cx_ref[...] = dxc[0:_HALO]
        _add_rows(dcw_ref, dws)
        dcb_ref[...] += db
        dp_ref[...] = jnp.concatenate([dxa, dga], axis=1).astype(dp_ref.dtype)

    vec = pl.BlockSpec((1, tc), lambda j, i: (0, j))
    mat = pl.BlockSpec((1, tc, tc), lambda j, i: (j, 0, 0))
    cwspec = pl.BlockSpec((LRU_CONV, tc), lambda j, i: (0, j))
    blk = pl.BlockSpec((tb, tc), lambda j, i: (nt - 1 - i, j))
    vshape = jax.ShapeDtypeStruct((1, lw), f32)
    mshape = jax.ShapeDtypeStruct((nj, tc, tc), f32)
    return _pcall(body, name=name, grid=(nj, nt),
                  in_specs=[pl.BlockSpec((tb, 2 * tc), lambda j, i: (nt - 1 - i, j)),
                            pl.BlockSpec((_HALO, 2 * tc), lambda j, i: (jnp.maximum((nt - 1 - i) * rr - 1, 0), j)),
                            blk,
                            pl.BlockSpec((_HALO, tc), lambda j, i: (jnp.maximum((nt - 1 - i) * rr - 1, 0), j)),
                            blk, cwspec, vec, mat, vec, mat, vec, vec],
                  out_specs=[pl.BlockSpec((tb, 2 * tc), lambda j, i: (nt - 1 - i, j)), cwspec, vec, mat, vec, mat, vec,
                             vec],
                  out_shape=[jax.ShapeDtypeStruct((T, width), _MXU), jax.ShapeDtypeStruct((LRU_CONV, lw), f32),
                             vshape, mshape, vshape, mshape, vshape, vshape],
                  scratch=[pltpu.VMEM((_HALO, tc), f32), pltpu.VMEM((_HALO, tc), f32)],
                  sem=("parallel", "arbitrary"))(
        proj, proj, h, h, dy, conv_w, conv_b, wr, br, wi, bi, lam)


def _hg_chunk_fwd(blk, lb, hd):
    q, fr, v, gt = blk[:, 0:hd], blk[:, hd:2 * hd], blk[:, 2 * hd:3 * hd], blk[:, 3 * hd:4 * hd]
    sg = _sig(fr)
    f = lb + (1.0 - lb) * sg
    k = (1.0 - lb) * (1.0 - sg)
    g = jnp.log(f)
    n = HG_CHUNK
    cum = _dot_hi(_tri(n).astype(f32), g)
    row = lax.broadcasted_iota(jnp.int32, g.shape, 0)
    tot = jnp.sum(g, axis=0, keepdims=True)
    mid = jnp.sum(jnp.where(row < n // 2, g, 0.0), axis=0, keepdims=True)
    sq = _sig(q)
    qs = q * sq * (hd ** -0.5)
    e_qd, e_kd, e_qc, e_kt = jnp.exp(cum - mid), jnp.exp(mid - cum), jnp.exp(cum), jnp.exp(tot - cum)
    qd, kd, qc, kt = qs * e_qd, k * e_kd, qs * e_qc, k * e_kt
    p = jnp.where(_tri(n), _dot_nt(qd, kd), 0.0)
    return dict(q=q, v=v, gt=gt, sg=sg, f=f, k=k, sq=sq, qs=qs, tot=tot, e_qd=e_qd, e_kd=e_kd, e_qc=e_qc,
                e_kt=e_kt, qd=qd, kd=kd, qc=qc, kt=kt, p=p)


def _hgrn_fwd(proj, lb, nw, col0, nh, hd, name):
    T = proj.shape[0]
    tb = _pick(T, (_TB_HG, 256, 128, 64))
    ncb = tb // HG_CHUNK
    cb0 = col0 // (4 * hd)

    def body(p_ref, lb_ref, nw_ref, y_ref, sc_ref, st_ref):
        @pl.when(pl.program_id(1) == 0)
        def _():
            st_ref[...] = jnp.zeros_like(st_ref)

        lbv, nwv = lb_ref[...], nw_ref[...]

        def chunk(c, carry):
            r0 = pl.multiple_of(c * HG_CHUNK, HG_CHUNK)
            d = _hg_chunk_fwd(p_ref[pl.ds(r0, HG_CHUNK), :], lbv, hd)
            st = st_ref[...]
            sc_ref[c, 0] = st
            o = _dot(d["p"], d["v"]) + _dot_nt(d["qc"], st)
            st_ref[...] = st * jnp.exp(d["tot"]) + _dot_tn(d["v"], d["kt"])
            rstd = lax.rsqrt(jnp.mean(o * o, axis=-1, keepdims=True) + EPS)
            sil, _ = _silu_and_grad(d["gt"])
            y_ref[pl.ds(r0, HG_CHUNK), :] = (o * rstd * nwv * sil).astype(y_ref.dtype)
            return carry

        lax.fori_loop(0, ncb, chunk, 0)

    return _pcall(body, name=name, grid=(nh, T // tb),
                  in_specs=[pl.BlockSpec((tb, 4 * hd), lambda h, i: (i, cb0 + h)),
                            pl.BlockSpec((1, hd), lambda h, i: (0, h)), pl.BlockSpec((1, hd), lambda h, i: (0, 0))],
                  out_specs=[pl.BlockSpec((tb, hd), lambda h, i: (i, h)),
                             pl.BlockSpec((ncb, 1, hd, hd), lambda h, i: (i, h, 0, 0))],
                  out_shape=[jax.ShapeDtypeStruct((T, nh * hd), _MXU),
                             jax.ShapeDtypeStruct((T // HG_CHUNK, nh, hd, hd), f32)],
                  scratch=[pltpu.VMEM((hd, hd), f32)], sem=("parallel", "arbitrary"))(proj, lb, nw)


def _hgrn_bwd(proj, st_chunks, dy, dproj, lb, nw, col0, dycol0, nh, hd, name):
    T = proj.shape[0]
    tb = _pick(T, (_TB_HG, 256, 128, 64))
    ncb = tb // HG_CHUNK
    nt = T // tb
    cb0 = col0 // (4 * hd)
    dyb0 = dycol0 // hd
    n = HG_CHUNK

    def body(p_ref, sc_ref, dy_ref, dpin_ref, lb_ref, nw_ref, dp_ref, dlb_ref, dnw_ref, dst_ref):
        del dpin_ref

        @pl.when(pl.program_id(1) == 0)
        def _():
            dst_ref[...] = jnp.zeros_like(dst_ref)
            dlb_ref[...] = jnp.zeros_like(dlb_ref)
            dnw_ref[...] = jnp.zeros_like(dnw_ref)

        lbv, nwv = lb_ref[...], nw_ref[...]
        tril = _tri(n)

        def chunk(cc, carry):
            c = ncb - 1 - cc
            r0 = pl.multiple_of(c * HG_CHUNK, HG_CHUNK)
            d = _hg_chunk_fwd(p_ref[pl.ds(r0, HG_CHUNK), :], lbv, hd)
            st = sc_ref[c, 0]
            v, gt = d["v"], d["gt"]
            o = _dot(d["p"], v) + _dot_nt(d["qc"], st)
            rstd = lax.rsqrt(jnp.mean(o * o, axis=-1, keepdims=True) + EPS)
            sil, dsil = _silu_and_grad(gt)
            dyv = dy_ref[pl.ds(r0, HG_CHUNK), :]
            xhat = o * rstd
            don = dyv * sil
            dgt = dyv * xhat * nwv * dsil
            dnw_ref[0] += jnp.sum(don * xhat, axis=0, keepdims=True)
            g2 = don * nwv
            do = rstd * (g2 - xhat * jnp.mean(g2 * xhat, axis=-1, keepdims=True))
            dstn = dst_ref[...]
            dp = jnp.where(tril, _dot_nt(do, v), 0.0)
            dv = _dot_tn(d["p"], do) + _dot_nt(d["kt"], dstn)
            dqd = _dot(dp, d["kd"])
            dkd = _dot_tn(dp, d["qd"])
            dqc = _dot(do, st)
            dkt = _dot(v, dstn)
            e_tot = jnp.exp(d["tot"])
            dst_ref[...] = dstn * e_tot + _dot_tn(do, d["qc"])
            dtot = jnp.sum(e_tot * st * dstn, axis=0, keepdims=True) + jnp.sum(dkt * d["kt"], axis=0, keepdims=True)
            dqs = dqd * d["e_qd"] + dqc * d["e_qc"]
            dk = dkd * d["e_kd"] + dkt * d["e_kt"]
            dcum = dqd * d["qd"] - dkd * d["kd"] + dqc * d["qc"] - dkt * d["kt"]
            dg = _dot_hi(_tri(n, lower=False).astype(f32), dcum) + dtot
            df = dg / d["f"] - dk
            sg = d["sg"]
            dfr = df * (1.0 - lbv) * sg * (1.0 - sg)
            dlb_ref[...] += jnp.sum(df * (1.0 - sg), axis=0, keepdims=True)
            sq, q = d["sq"], d["q"]
            dq = dqs * (hd ** -0.5) * sq * (1.0 + q * (1.0 - sq))
            dp_ref[pl.ds(r0, HG_CHUNK), :] = jnp.concatenate([dq, dfr, dv, dgt], axis=1).astype(dp_ref.dtype)
            return carry

        lax.fori_loop(0, ncb, chunk, 0)

    width = dproj.shape[1]
    return _pcall(body, name=name, grid=(nh, nt),
                  in_specs=[pl.BlockSpec((tb, 4 * hd), lambda h, i: (nt - 1 - i, cb0 + h)),
                            pl.BlockSpec((ncb, 1, hd, hd), lambda h, i: (nt - 1 - i, h, 0, 0)),
                            pl.BlockSpec((tb, hd), lambda h, i: (nt - 1 - i, dyb0 + h)),
                            pl.BlockSpec(memory_space=pl.ANY),
                            pl.BlockSpec((1, hd), lambda h, i: (0, h)), pl.BlockSpec((1, hd), lambda h, i: (0, 0))],
                  out_specs=[pl.BlockSpec((tb, 4 * hd), lambda h, i: (nt - 1 - i, cb0 + h)),
                             pl.BlockSpec((1, hd), lambda h, i: (0, h)),
                             pl.BlockSpec((1, 1, hd), lambda h, i: (h, 0, 0))],
                  out_shape=[jax.ShapeDtypeStruct((T, width), _MXU), jax.ShapeDtypeStruct((1, nh * hd), f32),
                             jax.ShapeDtypeStruct((nh, 1, hd), f32)],
                  scratch=[pltpu.VMEM((hd, hd), f32)], sem=("parallel", "arbitrary"), aliases={3: 0})(
        proj, st_chunks, dy, dproj, lb, nw)


def _ssd_dt(dtraw_c, g, hpg, dtb_row, dtb_col, a_row, a_col):
    hp = dtraw_c.shape[1]
    sel = (lax.broadcasted_iota(jnp.int32, (hp, hpg), 0) == g * hpg + lax.broadcasted_iota(jnp.int32, (hp, hpg), 1))
    selt = (lax.broadcasted_iota(jnp.int32, (hpg, hp), 1) == g * hpg + lax.broadcasted_iota(jnp.int32, (hpg, hp), 0))
    pre = _dot_hi(dtraw_c, sel.astype(f32)) + dtb_row
    pre_t = lax.dot_general(selt.astype(f32), dtraw_c, (((1,), (1,)), ((), ())), preferred_element_type=f32,
                            precision=_HI) + dtb_col
    dt = _softplus(pre)
    dt_t = _softplus(pre_t)
    return pre, dt, dt * a_row, dt_t * a_col


def _ssd_chunk_fwd(xbc, z, dtraw_c, g, prm, s_ref, hpg, gw):
    n, ns, pp = SSD_CHUNK, SSD_STATE, 2 * (gw // hpg)
    dtb_row, dtb_col, a_row, a_col, d_row = prm
    bm, cm = xbc[:, gw:gw + ns], xbc[:, gw + ns:gw + 2 * ns]
    pre, dt, a, a_t = _ssd_dt(dtraw_c, g, hpg, dtb_row, dtb_col, a_row, a_col)
    cum = _dot_hi(_tri(n).astype(f32), a)
    cum_t = _dot_hi(a_t, _tri(n, lower=False).astype(f32))
    tot = jnp.sum(a, axis=0, keepdims=True)
    cb = _dot_nt(cm, bm)
    tril = _tri(n)
    lo = lax.broadcasted_iota(jnp.int32, (n, pp), 1) < pp // 2
    sub_lo = lax.broadcasted_iota(jnp.int32, (pp, ns), 0) < pp // 2

    def per_lane(m, h0):
        return jnp.where(lo[0:m.shape[0]], m[:, h0:h0 + 1], m[:, h0 + 1:h0 + 2])

    pairs = []
    for p in range(hpg // 2):
        h0 = 2 * p
        xp = xbc[:, pp * p:pp * (p + 1)]
        dtl = per_lane(dt, h0)
        xdt = xp * dtl
        lm, mm = [], []
        for h in (h0, h0 + 1):
            l_h = jnp.where(tril, jnp.exp(jnp.minimum(cum[:, h:h + 1] - cum_t[h:h + 1, :], 0.0)), 0.0)
            lm.append(l_h)
            mm.append(cb * l_h)
        y_diag = _dot(mm[0], jnp.where(lo, xdt, 0.0)) + _dot(mm[1], jnp.where(lo, 0.0, xdt))
        sp = s_ref[p]
        e = jnp.exp(per_lane(cum, h0))
        y_off = _dot_nt(cm, sp) * e
        fd = jnp.exp(per_lane(tot - cum, h0))
        u = xdt * fd
        etot = jnp.exp(tot)
        etot_col = jnp.where(sub_lo, etot[:, h0:h0 + 1], etot[:, h0 + 1:h0 + 2])
        dl = per_lane(d_row, h0)
        y = y_diag + y_off + xp * dl
        zp = z[:, pp * p:pp * (p + 1)]
        sil, dsil = _silu_and_grad(zp)
        pairs.append(dict(xp=xp, dtl=dtl, xdt=xdt, lm=lm, mm=mm, y_diag=y_diag, sp=sp, e=e, y_off=y_off, fd=fd, u=u,
                          etot_col=etot_col, dl=dl, y=y, sil=sil, dsil=dsil, y2=y * sil))
    ms = sum(jnp.sum(q["y2"] * q["y2"], axis=-1, keepdims=True) for q in pairs) * (1.0 / gw)
    rstd = lax.rsqrt(ms + EPS)
    return dict(bm=bm, cm=cm, pre=pre, dt=dt, a=a, cum=cum, cum_t=cum_t, tot=tot, cb=cb, lo=lo, sub_lo=sub_lo,
                pairs=pairs, rstd=rstd)


def _ssd_specs(tb, gwx, hpg, hp, cw, gw, tmap):
    r = tb // _HALO
    row = pl.BlockSpec((1, 1, hpg), lambda g, i: (g, 0, 0))
    col = pl.BlockSpec((1, hpg, 1), lambda g, i: (g, 0, 0))
    return [pl.BlockSpec((tb, gwx), lambda g, i: (tmap(i), g)),
            pl.BlockSpec((_HALO, gwx), lambda g, i: (jnp.maximum(tmap(i) * r - 1, 0), g)),
            pl.BlockSpec((tb, hp), lambda g, i: (tmap(i), 0)),
            pl.BlockSpec((1, SSD_CONV, cw), lambda g, i: (g, 0, 0)),
            pl.BlockSpec((1, 1, cw), lambda g, i: (g, 0, 0)),
            row, col, row, col, row,
            pl.BlockSpec((1, gw), lambda g, i: (0, g))]


def _ssd_fwd(proj, dt_raw, conv_w, conv_b, dtb_row, dtb_col, a_row, a_col, d_row, norm_w, hpg, name):
    T = proj.shape[0]
    G = SSD_GROUPS
    gwx = proj.shape[1] // G
    gw = (gwx - 2 * SSD_STATE) // 2
    cw = gw + 2 * SSD_STATE
    hp = dt_raw.shape[1]
    tb = _pick(T, (_TB_SSD, 128, 64))
    ncb = tb // SSD_CHUNK
    npair = hpg // 2
    pp = 2 * (gw // hpg)
    assert pp == _LANES and hpg % 2 == 0

    def body(p_ref, prev_ref, dt_ref, cw_ref, cb_ref, dtbr_ref, dtbc_ref, ar_ref, ac_ref, dr_ref, nw_ref,
             y_ref, sc_ref, s_ref, xbc_ref):
        g = pl.program_id(0)
        i = pl.program_id(1)

        @pl.when(i == 0)
        def _():
            s_ref[...] = jnp.zeros_like(s_ref)

        raw = p_ref[:, gw:]
        prev = jnp.where(i > 0, prev_ref[:, gw:], 0.0)
        pre = _conv_fwd(raw, prev, [cw_ref[0, j:j + 1, :] for j in range(SSD_CONV)], cb_ref[0], SSD_CONV)
        xbc_ref[...] = pre * _sig(pre)
        prm = (dtbr_ref[0], dtbc_ref[0], ar_ref[0], ac_ref[0], dr_ref[0])
        nwv = nw_ref[...]

        def chunk(c, carry):
            r0 = pl.multiple_of(c * SSD_CHUNK, SSD_CHUNK)
            rows = pl.ds(r0, SSD_CHUNK)
            sc_ref[c, 0] = s_ref[...]
            d = _ssd_chunk_fwd(xbc_ref[rows, :], p_ref[rows, 0:gw], dt_ref[rows, :], g, prm, s_ref, hpg, gw)
            for p, q in enumerate(d["pairs"]):
                s_ref[p] = q["sp"] * q["etot_col"] + _dot_tn(q["u"], d["bm"])
                y_ref[rows, pp * p:pp * (p + 1)] = (q["y2"] * d["rstd"] * nwv[:, pp * p:pp * (p + 1)]).astype(
                    y_ref.dtype)
            return carry

        lax.fori_loop(0, ncb, chunk, 0)

    return _pcall(body, name=name, grid=(G, T // tb),
                  in_specs=_ssd_specs(tb, gwx, hpg, hp, cw, gw, lambda i: i),
                  out_specs=[pl.BlockSpec((tb, gw), lambda g, i: (i, g)),
                             pl.BlockSpec((ncb, 1, npair, pp, SSD_STATE), lambda g, i: (i, g, 0, 0, 0))],
                  out_shape=[jax.ShapeDtypeStruct((T, G * gw), _MXU),
                             jax.ShapeDtypeStruct((T // SSD_CHUNK, G, npair, pp, SSD_STATE), f32)],
                  scratch=[pltpu.VMEM((npair, pp, SSD_STATE), f32), pltpu.VMEM((tb, cw), f32)],
                  sem=("parallel", "arbitrary"))(
        proj, proj, dt_raw, conv_w, conv_b, dtb_row, dtb_col, a_row, a_col, d_row, norm_w)


def _ssd_bwd(proj, dt_raw, s_chunks, dy, conv_w, conv_b, dtb_row, dtb_col, a_row, a_col, d_row, norm_w, hpg, name):
    T = proj.shape[0]
    G = SSD_GROUPS
    gwx = proj.shape[1] // G
    gw = (gwx - 2 * SSD_STATE) // 2
    ns = SSD_STATE
    cw = gw + 2 * ns
    hp = dt_raw.shape[1]
    tb = _pick(T, (_TB_SSD, 128, 64))
    ncb, nt = tb // SSD_CHUNK, T // tb
    npair = hpg // 2
    pp = 2 * (gw // hpg)
    n = SSD_CHUNK

    def body(p_ref, prev_ref, dt_ref, cw_ref, cb_ref, dtbr_ref, dtbc_ref, ar_ref, ac_ref, dr_ref, nw_ref,
             sc_ref, dy_ref,
             dp_ref, ddt_ref, dcw_ref, dcb_ref, ddtb_ref, da_ref, dd_ref, dnw_ref,
             ds_ref, s_ref, xbc_ref, dxbc_ref, carry_ref):
        g = pl.program_id(0)
        i = pl.program_id(1)
        ti = nt - 1 - i

        @pl.when(i == 0)
        def _():
            for ref in (ds_ref, carry_ref, dcw_ref, dcb_ref, ddtb_ref, da_ref, dd_ref, dnw_ref):
                ref[...] = jnp.zeros_like(ref)

        raw = p_ref[:, gw:]
        prev = jnp.where(ti > 0, prev_ref[:, gw:], 0.0)
        cwr = [cw_ref[0, j:j + 1, :] for j in range(SSD_CONV)]
        pre = _conv_fwd(raw, prev, cwr, cb_ref[0], SSD_CONV)
        xbc_ref[...] = pre * _sig(pre)
        prm = (dtbr_ref[0], dtbc_ref[0], ar_ref[0], ac_ref[0], dr_ref[0])
        a_row = ar_ref[0]
        nwv = nw_ref[...]
        tril = _tri(n)
        lane8 = lax.broadcasted_iota(jnp.int32, (n, hpg), 1)

        def chunk(cc, carry):
            c = ncb - 1 - cc
            r0 = pl.multiple_of(c * SSD_CHUNK, SSD_CHUNK)
            rows = pl.ds(r0, SSD_CHUNK)
            s_ref[...] = sc_ref[c, 0]
            xbc = xbc_ref[rows, :]
            d = _ssd_chunk_fwd(xbc, p_ref[rows, 0:gw], dt_ref[rows, :], g, prm, s_ref, hpg, gw)
            bm, cm, lo, sub_lo, rstd = d["bm"], d["cm"], d["lo"], d["sub_lo"], d["rstd"]
            dyo = dy_ref[rows, :]
            s1 = sum(jnp.sum(dyo[:, pp * p:pp * (p + 1)] * nwv[:, pp * p:pp * (p + 1)] * q["y2"], axis=-1,
                             keepdims=True) for p, q in enumerate(d["pairs"]))
            dcb = jnp.zeros((n, n), f32)
            cb_t = _dot_nt(bm, cm)
            triu = _tri(n, lower=False)
            dbm = jnp.zeros((n, ns), f32)
            dcm = jnp.zeros((n, ns), f32)
            dcum = jnp.zeros((n, hpg), f32)
            ddt_dir = jnp.zeros((n, hpg), f32)
            dtot = jnp.zeros((1, hpg), f32)
            ddv = jnp.zeros((1, hpg), f32)
            dxs = []
            for p, q in enumerate(d["pairs"]):
                h0 = 2 * p
                sl = slice(pp * p, pp * (p + 1))
                dyp = dyo[:, sl]
                nwp = nwv[:, sl]
                dnw_ref[:, sl] += jnp.sum(dyp * q["y2"] * rstd, axis=0, keepdims=True)
                dy2 = rstd * (dyp * nwp - q["y2"] * (rstd * rstd) * s1 * (1.0 / gw))
                dyv = dy2 * q["sil"]
                dz = dy2 * q["y"] * q["dsil"]
                dp_ref[rows, sl] = dz.astype(dp_ref.dtype)
                xp, xdt, mm, lm = q["xp"], q["xdt"], q["mm"], q["lm"]
                dyx = dyv * xp
                dx = dyv * q["dl"]
                dxdt_diag = jnp.where(lo, _dot_tn(mm[0], dyv), _dot_tn(mm[1], dyv))
                dy_lo, dy_hi = jnp.where(lo, dyv, 0.0), jnp.where(lo, 0.0, dyv)
                dm0 = jnp.where(tril, _dot_nt(dy_lo, xdt), 0.0)
                dm1 = jnp.where(tril, _dot_nt(dy_hi, xdt), 0.0)
                dcb = dcb + dm0 * lm[0] + dm1 * lm[1]
                seg = []
                for hh, (dm, dyh) in enumerate(((dm0, dy_lo), (dm1, dy_hi))):
                    h = h0 + hh
                    l_t = jnp.where(triu, jnp.exp(jnp.minimum(d["cum_t"][h:h + 1, :] - d["cum"][:, h:h + 1], 0.0)), 0.0)
                    dm_t = _dot_nt(xdt, dyh)
                    seg.append(jnp.sum(dm * mm[hh], axis=1, keepdims=True)
                               - jnp.sum(dm_t * (cb_t * l_t), axis=1, keepdims=True))
                dsn = ds_ref[p]
                dzz = dyv * q["e"]
                dcm = dcm + _dot(dzz, q["sp"])
                ds_ref[p] = dsn * q["etot_col"] + _dot_tn(dzz, cm)
                du = _dot_nt(bm, dsn)
                dbm = dbm + _dot(q["u"], dsn)
                dxdt = dxdt_diag + du * q["fd"]
                wv = du * q["u"]
                cterm = dyv * q["y_off"] - wv
                ddir = dxdt * xp
                sdec = q["etot_col"] * q["sp"] * dsn

                def halves(x):
                    lo_s = jnp.sum(jnp.where(lo, x, 0.0), axis=1, keepdims=True)
                    return lo_s, jnp.sum(x, axis=1, keepdims=True) - lo_s

                c_lo, c_hi = halves(cterm)
                d_lo, d_hi = halves(ddir)
                w_lo, w_hi = halves(wv)
                x_lo, x_hi = halves(dyx)
                s_all = jnp.sum(sdec, axis=1, keepdims=True)
                s_lo = jnp.sum(jnp.where(sub_lo[:, 0:1], s_all, 0.0), axis=0, keepdims=True)
                s_hi = jnp.sum(s_all, axis=0, keepdims=True) - s_lo
                for h, cv, dv_, wv_, xv_, sv_ in ((h0, c_lo + seg[0], d_lo, w_lo, x_lo, s_lo),
                                                  (h0 + 1, c_hi + seg[1], d_hi, w_hi, x_hi, s_hi)):
                    sel = lane8 == h
                    dcum = dcum + jnp.where(sel, cv, 0.0)
                    ddt_dir = ddt_dir + jnp.where(sel, dv_, 0.0)
                    dtot = dtot + jnp.where(sel[0:1], jnp.sum(wv_, axis=0, keepdims=True) + sv_, 0.0)
                    ddv = ddv + jnp.where(sel[0:1], jnp.sum(xv_, axis=0, keepdims=True), 0.0)
                dxs.append(dx + dxdt * q["dtl"])
            da_c = _dot_hi(_tri(n, lower=False).astype(f32), dcum) + dtot
            ddt = da_c * a_row + ddt_dir
            da_ref[0] += jnp.sum(da_c * d["dt"], axis=0, keepdims=True)
            dd_ref[0] += ddv
            ddt_raw = ddt * _sig(d["pre"])
            ddtb_ref[0] += jnp.sum(ddt_raw, axis=0, keepdims=True)
            ddt_ref[0, rows, :] = ddt_raw
            dcm = dcm + _dot(dcb, bm)
            dbm = dbm + _dot_tn(dcb, cm)
            dxbc_ref[rows, :] = jnp.concatenate(dxs + [dbm, dcm], axis=1)
            return carry

        lax.fori_loop(0, ncb, chunk, 0)
        _, dsil = _silu_and_grad(pre)
        dpre = dxbc_ref[...] * dsil
        draw, dws, db = _conv_bwd(dpre, carry_ref[...], raw, prev, cwr, SSD_CONV)
        carry_ref[...] = dpre[0:_HALO]
        for j, r_ in enumerate(dws):
            dcw_ref[0, j:j + 1, :] += r_
        dcb_ref[0] += db
        dp_ref[:, gw:] = draw.astype(dp_ref.dtype)

    tmap = lambda i: nt - 1 - i
    row = pl.BlockSpec((1, 1, hpg), lambda g, i: (g, 0, 0))
    rshape = jax.ShapeDtypeStruct((G, 1, hpg), f32)
    return _pcall(body, name=name, grid=(G, nt),
                  in_specs=_ssd_specs(tb, gwx, hpg, hp, cw, gw, tmap) + [
                      pl.BlockSpec((ncb, 1, npair, pp, ns), lambda g, i: (tmap(i), g, 0, 0, 0)),
                      pl.BlockSpec((tb, gw), lambda g, i: (tmap(i), g))],
                  out_specs=[pl.BlockSpec((tb, gwx), lambda g, i: (tmap(i), g)),
                             pl.BlockSpec((1, tb, hpg), lambda g, i: (g, tmap(i), 0)),
                             pl.BlockSpec((1, SSD_CONV, cw), lambda g, i: (g, 0, 0)),
                             pl.BlockSpec((1, 1, cw), lambda g, i: (g, 0, 0)),
                             row, row, row,
                             pl.BlockSpec((1, gw), lambda g, i: (0, g))],
                  out_shape=[jax.ShapeDtypeStruct((T, G * gwx), _MXU), jax.ShapeDtypeStruct((G, T, hpg), f32),
                             jax.ShapeDtypeStruct((G, SSD_CONV, cw), f32), jax.ShapeDtypeStruct((G, 1, cw), f32),
                             rshape, rshape, rshape, jax.ShapeDtypeStruct((1, G * gw), f32)],
                  scratch=[pltpu.VMEM((npair, pp, ns), f32), pltpu.VMEM((npair, pp, ns), f32),
                           pltpu.VMEM((tb, cw), f32), pltpu.VMEM((tb, cw), f32), pltpu.VMEM((_HALO, cw), f32)],
                  sem=("parallel", "arbitrary"))(
        proj, proj, dt_raw, conv_w, conv_b, dtb_row, dtb_col, a_row, a_col, d_row, norm_w, s_chunks, dy)


def _interleave(w, parts, blk):
    lead = w.shape[:-1]
    n = w.shape[-1] // (parts * blk)
    nd = len(lead)
    w = w.reshape(*lead, parts, n, blk)
    w = jnp.swapaxes(w, nd, nd + 1)
    return w.reshape(*lead, parts * n * blk)


def _deinterleave(w, parts, blk):
    lead = w.shape[:-1]
    n = w.shape[-1] // (parts * blk)
    nd = len(lead)
    w = w.reshape(*lead, n, parts, blk)
    w = jnp.swapaxes(w, nd, nd + 1)
    return w.reshape(*lead, parts * n * blk)


def _even_perm(w, lw, hd, tc):
    return jnp.concatenate([_interleave(w[..., :2 * lw], 2, tc), _interleave(w[..., 2 * lw:], 4, hd)], axis=-1)


def _even_unperm(w, lw, hd, tc):
    return jnp.concatenate([_deinterleave(w[..., :2 * lw], 2, tc), _deinterleave(w[..., 2 * lw:], 4, hd)], axis=-1)


def _ssd_split(w, inner, offset):
    G, N = SSD_GROUPS, SSD_STATE
    lead = w.shape[:-1]
    xs = w[..., offset:offset + inner].reshape(*lead, G, inner // G)
    bm = w[..., offset + inner:offset + inner + G * N].reshape(*lead, G, N)
    cm = w[..., offset + inner + G * N:offset + inner + 2 * G * N].reshape(*lead, G, N)
    return xs, bm, cm


def _ssd_in_perm(w, inner, heads, hp):
    G = SSD_GROUPS
    lead = w.shape[:-1]
    z = w[..., :inner].reshape(*lead, G, inner // G)
    xs, bm, cm = _ssd_split(w, inner, inner)
    main = jnp.concatenate([z, xs, bm, cm], axis=-1).reshape(*lead, -1)
    dt = w[..., w.shape[-1] - heads:]
    dt = jnp.pad(dt, [(0, 0)] * len(lead) + [(0, hp - heads)])
    return main, dt


def _ssd_in_unperm(main, dt, inner, heads):
    G, N = SSD_GROUPS, SSD_STATE
    gw = inner // G
    lead = main.shape[:-1]
    m = main.reshape(*lead, G, 2 * gw + 2 * N)
    parts = [m[..., :gw], m[..., gw:2 * gw], m[..., 2 * gw:2 * gw + N], m[..., 2 * gw + N:]]
    return jnp.concatenate([p.reshape(*lead, -1) for p in parts] + [dt[..., :heads]], axis=-1)


def _ssd_conv_perm(w, inner):
    return jnp.concatenate(_ssd_split(w, inner, 0), axis=-1)


def _ssd_conv_unperm(w, inner):
    G, N = SSD_GROUPS, SSD_STATE
    gw = inner // G
    lead = w.shape[:-2]
    parts = [w[..., :gw], w[..., gw:gw + N], w[..., gw + N:]]
    return jnp.concatenate([p.reshape(*lead, -1) for p in parts], axis=-1)


def _blockdiag(w, tc):
    nb, b, _ = w.shape
    per = tc // b
    eye = jnp.eye(per, dtype=w.dtype)
    w = w.reshape(nb // per, per, b, b)
    return jnp.einsum("jbik,bc->jbick", w, eye).reshape(nb // per, tc, tc)


def _blockdiag_extract(g, b):
    nj, tc, _ = g.shape
    per = tc // b
    g = g.reshape(nj, per, b, per, b)
    return jnp.einsum("jbick,bc->jbik", g, jnp.eye(per, dtype=g.dtype)).reshape(nj * per, b, b)


def _lower_bounds(hg_lower_bounds):
    lbp = jax.nn.softmax(hg_lower_bounds.astype(f32), axis=0)
    return jnp.cumsum(lbp, axis=0) - lbp[0]


def _local_step(x, target, w):
    T, D = x.shape
    depth = w["norm_mix_w"].shape[0]
    lw = w["lru_conv_b"].shape[1]
    hw = w["hg_lower_bounds"].shape[1]
    hd = w["hg_norm_w"].shape[1]
    nh = hw // hd
    heads = w["ssd_dt_bias"].shape[1]
    inner = w["ssd_norm_w"].shape[1]
    G, N = SSD_GROUPS, SSD_STATE
    hpg = heads // G
    gw = inner // G
    hp = -(-heads // _LANES) * _LANES
    F = w["ffn_w_down"].shape[1]
    tcl = _pick(lw, (_TC_LRU, 128))
    tcf = _pick(F, (_TC_FFN, 256, 128))
    even_in = 2 * lw + 4 * hw
    n_even, n_odd = w["ev_w_in"].shape[0], w["ssd_w_in"].shape[0]

    ev_w_in = _even_perm(w["ev_w_in"], lw, hd, tcl)
    ssd_main, ssd_dtw = _ssd_in_perm(w["ssd_w_in"], inner, heads, hp)
    ssd_cw = jnp.swapaxes(_ssd_conv_perm(w["ssd_conv_w"], inner), 1, 2)
    ssd_cb = _ssd_conv_perm(w["ssd_conv_b"], inner)[:, :, None, :]
    ffn_up = _interleave(w["ffn_w_up"], 2, tcf)
    ffn_cw = _interleave(w["ffn_conv_w"], 2, tcf)
    ffn_cb = _interleave(w["ffn_conv_b"], 2, tcf)
    lbs_fn = jax.vjp(_lower_bounds, w["hg_lower_bounds"])
    lbs = lbs_fn[0]
    a_neg = -jnp.exp(w["ssd_a_log"].astype(f32))

    def rowcol(v):
        return v.reshape(G, 1, hpg), v.reshape(G, hpg, 1)

    saved = []
    for l in range(depth):
        s = {"x": x}
        h = _rmsnorm_fwd(x, w["norm_mix_w"][l], f"norm_mix_fwd_{l}")
        s["h"] = h
        if l % 2 == 0:
            e = l // 2
            proj = _mm(h, ev_w_in[e], M=T, N=even_in, K=D, name=f"even_in_{l}")
            s["wr"] = _blockdiag(w["lru_w_r"][e], tcl).astype(_MXU)
            s["wi"] = _blockdiag(w["lru_w_i"][e], tcl).astype(_MXU)
            vecs = [w[k][e].reshape(1, lw) for k in ("lru_conv_b", "lru_b_r", "lru_b_i", "lru_lambda")]
            s["lru_args"] = (w["lru_conv_w"][e], vecs[0], s["wr"], vecs[1], s["wi"], vecs[2], vecs[3])
            ya, hst = _lru_fwd(proj, *s["lru_args"], lw, f"lru_fwd_{l}")
            s["hg_args"] = (lbs[e].reshape(1, hw), w["hg_norm_w"][e].reshape(1, hd))
            yb, stc = _hgrn_fwd(proj, *s["hg_args"], 2 * lw, nh, hd, f"hgrn_fwd_{l}")
            xm = _mm(ya, w["ev_w_out"][e], M=T, N=D, K=lw, resid=x, name=f"even_out_a_{l}")
            xm = _mm(yb, w["ev_w_out"][e], M=T, N=D, K=hw, b_off=(lw, 0), resid=xm, name=f"even_out_b_{l}")
            s.update(proj=proj, ya=ya, yb=yb, hst=hst, stc=stc)
        else:
            o = l // 2
            proj = _mm(h, ssd_main[o], M=T, N=ssd_main.shape[-1], K=D, name=f"ssd_in_{l}")
            dtr = _mm(h, ssd_dtw[o], M=T, N=hp, K=D, name=f"ssd_dt_{l}")
            dtb_r, dtb_c = rowcol(w["ssd_dt_bias"][o].astype(f32))
            a_r, a_c = rowcol(a_neg[o])
            d_r, _ = rowcol(w["ssd_d"][o].astype(f32))
            s["ssd_args"] = (ssd_cw[o], ssd_cb[o], dtb_r, dtb_c, a_r, a_c, d_r, w["ssd_norm_w"][o].reshape(1, inner))
            y, sc = _ssd_fwd(proj, dtr, *s["ssd_args"], hpg, f"ssd_fwd_{l}")
            xm = _mm(y, w["ssd_w_out"][o], M=T, N=D, K=inner, resid=x, name=f"ssd_out_{l}")
            s.update(proj=proj, dtr=dtr, y=y, sc=sc)
        hn = _rmsnorm_fwd(xm, w["norm_ffn_w"][l], f"norm_ffn_fwd_{l}")
        up = _mm(hn, ffn_up[l], M=T, N=2 * F, K=D, name=f"ffn_up_{l}")
        act = _ffn_mid_fwd(up, ffn_cw[l], ffn_cb[l].reshape(1, 2 * F), f"ffn_mid_fwd_{l}")
        x = _mm(act, w["ffn_w_down"][l], M=T, N=D, K=F, resid=xm, name=f"ffn_down_{l}")
        s.update(xm=xm, hn=hn, up=up, act=act)
        saved.append(s)

    loss, dx, dwf = _loss_head(x, w["norm_final_w"], target)
    g = {k: [None] * v.shape[0] for k, v in w.items() if k != "norm_final_w"}
    g["norm_final_w"] = dwf.reshape(D)

    for l in reversed(range(depth)):
        s = saved[l]
        g["ffn_w_down"][l] = _mm(s["act"], dx, ta=True, M=F, N=D, K=T, name=f"ffn_down_dw_{l}")
        dact = _mm(dx, w["ffn_w_down"][l], tb=True, M=T, N=F, K=D, name=f"ffn_down_dx_{l}")
        dup, dcw, dcb = _ffn_mid_bwd(s["up"], dact, ffn_cw[l], ffn_cb[l].reshape(1, 2 * F), f"ffn_mid_bwd_{l}")
        g["ffn_conv_w"][l] = _deinterleave(dcw, 2, tcf)
        g["ffn_conv_b"][l] = _deinterleave(dcb, 2, tcf).reshape(2 * F)
        g["ffn_w_up"][l] = _deinterleave(_mm(s["hn"], dup, ta=True, M=D, N=2 * F, K=T, name=f"ffn_up_dw_{l}"), 2, tcf)
        dhn = _mm(dup, ffn_up[l], tb=True, M=T, N=D, K=2 * F, name=f"ffn_up_dx_{l}")
        dxm, dnf = _rmsnorm_bwd(dhn, s["xm"], w["norm_ffn_w"][l], dx, f"norm_ffn_bwd_{l}")
        g["norm_ffn_w"][l] = dnf.reshape(D)
        if l % 2 == 0:
            e = l // 2
            dy = _mm(dxm, w["ev_w_out"][e], tb=True, M=T, N=lw + hw, K=D, name=f"even_out_dx_{l}")
            dwo_a = _mm(s["ya"], dxm, ta=True, M=lw, N=D, K=T, name=f"even_out_a_dw_{l}")
            dwo_b = _mm(s["yb"], dxm, ta=True, M=hw, N=D, K=T, name=f"even_out_b_dw_{l}")
            g["ev_w_out"][e] = jnp.concatenate([dwo_a, dwo_b], axis=0)
            dproj, dcw, dcb, dwr, dbr, dwi, dbi, dlam = _lru_bwd(s["proj"], s["hst"], dy, *s["lru_args"], lw,
                                                                 f"lru_bwd_{l}")
            dproj, dlb, dnw = _hgrn_bwd(s["proj"], s["stc"], dy, dproj, *s["hg_args"], 2 * lw, lw, nh, hd,
                                        f"hgrn_bwd_{l}")
            g["lru_conv_w"][e] = dcw
            g["lru_conv_b"][e] = dcb.reshape(lw)
            g["lru_w_r"][e] = _blockdiag_extract(dwr, LRU_BLOCK)
            g["lru_w_i"][e] = _blockdiag_extract(dwi, LRU_BLOCK)
            g["lru_b_r"][e] = dbr.reshape(lw)
            g["lru_b_i"][e] = dbi.reshape(lw)
            g["lru_lambda"][e] = dlam.reshape(lw)
            g["hg_lower_bounds"][e] = dlb.reshape(hw)
            g["hg_norm_w"][e] = jnp.sum(dnw, axis=(0, 1))
            g["ev_w_in"][e] = _even_unperm(_mm(s["h"], dproj, ta=True, M=D, N=even_in, K=T, name=f"even_in_dw_{l}"),
                                           lw, hd, tcl)
            dh = _mm(dproj, ev_w_in[e], tb=True, M=T, N=D, K=even_in, name=f"even_in_dx_{l}")
        else:
            o = l // 2
            dy = _mm(dxm, w["ssd_w_out"][o], tb=True, M=T, N=inner, K=D, name=f"ssd_out_dx_{l}")
            g["ssd_w_out"][o] = _mm(s["y"], dxm, ta=True, M=inner, N=D, K=T, name=f"ssd_out_dw_{l}")
            dproj, ddt, dcw, dcb, ddtb, da, dd, dnw = _ssd_bwd(s["proj"], s["dtr"], s["sc"], dy, *s["ssd_args"], hpg,
                                                               f"ssd_bwd_{l}")
            ddt = jnp.pad(jnp.swapaxes(ddt, 0, 1).reshape(T, heads), ((0, 0), (0, hp - heads)))
            g["ssd_conv_w"][o] = _ssd_conv_unperm(jnp.swapaxes(dcw, 0, 1), inner)
            g["ssd_conv_b"][o] = _ssd_conv_unperm(dcb[:, 0, :], inner)
            g["ssd_dt_bias"][o] = ddtb.reshape(heads)
            g["ssd_a_log"][o] = da.reshape(heads) * a_neg[o]
            g["ssd_d"][o] = dd.reshape(heads)
            g["ssd_norm_w"][o] = dnw.reshape(inner)
            dmain = _mm(s["h"], dproj, ta=True, M=D, N=dproj.shape[1], K=T, name=f"ssd_in_dw_{l}")
            ddtw = _mm(s["h"], ddt, ta=True, M=D, N=hp, K=T, name=f"ssd_dt_dw_{l}")
            g["ssd_w_in"][o] = _ssd_in_unperm(dmain, ddtw, inner, heads)
            dh = _mm(dproj, ssd_main[o], tb=True, M=T, N=D, K=dproj.shape[1], name=f"ssd_in_dx_{l}")
            dh = _mm(ddt, ssd_dtw[o], tb=True, M=T, N=D, K=hp, resid=dh, name=f"ssd_dt_dx_{l}")
        dx, dnm = _rmsnorm_bwd(dh, s["x"], w["norm_mix_w"][l], dxm, f"norm_mix_bwd_{l}")
        g["norm_mix_w"][l] = dnm.reshape(D)

    grads = {k: (jnp.stack(v) if isinstance(v, list) else v) for k, v in g.items()}
    grads["hg_lower_bounds"] = lbs_fn[1](grads["hg_lower_bounds"])[0]
    return loss[0, 0], dx, grads


def _best_div(n, cap, mult=8):
    best = mult
    for d in range(mult, min(n, cap) + 1, mult):
        if n % d == 0:
            best = d
    assert n % best == 0, (n, best)
    return best


def _adamw(wv, gv, mv, vv, name):
    R, C = wv.shape
    tr = _best_div(R, _ADAM_ROWS)

    def body(w_ref, g_ref, m_ref, v_ref, d_ref, nm_ref, nv_ref):
        g = g_ref[...]
        m = ADAM_B1 * m_ref[...] + (1.0 - ADAM_B1) * g
        v = ADAM_B2 * v_ref[...] + (1.0 - ADAM_B2) * (g * g)
        m_hat = m / (1.0 - ADAM_B1 ** ADAM_STEP)
        v_hat = v / (1.0 - ADAM_B2 ** ADAM_STEP)
        d_ref[...] = -ADAM_LR * (m_hat / (jnp.sqrt(v_hat) + ADAM_EPS) + ADAM_WD * w_ref[...])
        nm_ref[...] = m
        nv_ref[...] = v

    blk = pl.BlockSpec((tr, C), lambda i: (i, 0))
    shp = jax.ShapeDtypeStruct((R, C), f32)
    return _pcall(body, name=name, grid=(R // tr,), in_specs=[blk] * 4, out_specs=[blk] * 3, out_shape=[shp] * 3,
                  sem=("parallel",))(wv, gv, mv, vv)


_ANY = pl.BlockSpec(memory_space=pl.ANY)
_MESH = pl.DeviceIdType.MESH


def _where_am_i():
    x, y, c = lax.axis_index("x"), lax.axis_index("y"), lax.axis_index("c")
    chips = [(1 - x, y), (x, 1 - y), (1 - x, 1 - y)]
    return x, y, c, chips


def _rcopy(src, dst, send_sems, recv_sems, j, dev):
    return pltpu.make_async_remote_copy(src_ref=src, dst_ref=dst, send_sem=send_sems.at[j], recv_sem=recv_sems.at[j],
                                        device_id=dev, device_id_type=_MESH)


def _comm_call(body, name, out_shape, n_sems, n_in, local_sem=False):
    scratch = [pltpu.SemaphoreType.DMA((n_sems,)), pltpu.SemaphoreType.DMA((n_sems,))]
    if local_sem:
        scratch.append(pltpu.SemaphoreType.DMA)
    return pl.pallas_call(body, name=name, out_shape=out_shape, in_specs=[_ANY] * n_in, out_specs=_ANY,
                          scratch_shapes=scratch)


def _allgather_chips(shard, name):
    R, C = shard.shape
    rh = R // 2

    def body(x_ref, out_ref, send_sems, recv_sems, local_sem):
        x, y, c, chips = _where_am_i()
        k = 2 * x + y
        sib = (x, y, 1 - c)

        def blk(kk, half):
            return out_ref.at[kk, pl.ds(half * rh, rh)]

        mine = pltpu.make_async_copy(x_ref, out_ref.at[k], local_sem)
        mine.start()
        first = [_rcopy(x_ref.at[pl.ds(c * rh, rh)], blk(k, c), send_sems, recv_sems, j, (cx, cy, c))
                 for j, (cx, cy) in enumerate(chips)]
        for cp in first:
            cp.start()
        passed = []
        for j, (cx, cy) in enumerate(chips):
            kk = 2 * cx + cy
            _rcopy(blk(kk, c), blk(kk, c), send_sems, recv_sems, j, (cx, cy, c)).wait_recv()
            fwd = _rcopy(blk(kk, c), blk(kk, c), send_sems, recv_sems, 3 + j, sib)
            fwd.start()
            passed.append(fwd)
        for j, (cx, cy) in enumerate(chips):
            kk = 2 * cx + cy
            _rcopy(blk(kk, 1 - c), blk(kk, 1 - c), send_sems, recv_sems, 3 + j, sib).wait_recv()
        for cp in first + passed:
            cp.wait_send()
        mine.wait()

    return _comm_call(body, name, jax.ShapeDtypeStruct((4, R, C), shard.dtype), 6, 1, local_sem=True)(shard)


def _d2d_exchange(g, name):
    _, R, C = g.shape
    rh = R // 2

    def body(g_ref, a_ref, send_sems, recv_sems):
        x, y, c, _ = _where_am_i()
        cp = _rcopy(g_ref.at[:, pl.ds((1 - c) * rh, rh)], a_ref, send_sems, recv_sems, 0, (x, y, 1 - c))
        cp.start()
        cp.wait()

    return _comm_call(body, name, jax.ShapeDtypeStruct((4, rh, C), g.dtype), 1, 1)(g)


def _ici_scatter(p, name):
    _, rh, C = p.shape

    def body(p_ref, b_ref, send_sems, recv_sems):
        x, y, c, chips = _where_am_i()
        cps = [_rcopy(p_ref.at[2 * cx + cy], b_ref.at[j], send_sems, recv_sems, j, (cx, cy, c))
               for j, (cx, cy) in enumerate(chips)]
        for cp in cps:
            cp.start()
        for cp in cps:
            cp.wait()

    return _comm_call(body, name, jax.ShapeDtypeStruct((3, rh, C), p.dtype), 3, 1)(p)


def _d2d_swap(q, name):
    rh, C = q.shape

    def body(q_ref, o_ref, send_sems, recv_sems, local_sem):
        x, y, c, _ = _where_am_i()
        mine = pltpu.make_async_copy(q_ref, o_ref.at[pl.ds(c * rh, rh)], local_sem)
        mine.start()
        cp = _rcopy(q_ref, o_ref.at[pl.ds(c * rh, rh)], send_sems, recv_sems, 0, (x, y, 1 - c))
        cp.start()
        cp.wait()
        mine.wait()

    return _comm_call(body, name, jax.ShapeDtypeStruct((2 * rh, C), q.dtype), 1, 1, local_sem=True)(q)


def _add_halves(g, a, c_idx, name):
    _, R, C = g.shape
    rh = R // 2
    tr = _best_div(rh, _ADAM_ROWS)
    nb = rh // tr

    def body(c_ref, g_ref, a_ref, o_ref):
        del c_ref
        o_ref[...] = g_ref[...] + a_ref[...]

    return _pcall(body, name=name, grid=(4, nb), prefetch=1,
                  in_specs=[pl.BlockSpec((1, tr, C), lambda j, i, c: (j, c[0] * nb + i, 0)),
                            pl.BlockSpec((1, tr, C), lambda j, i, c: (j, i, 0))],
                  out_specs=pl.BlockSpec((1, tr, C), lambda j, i, c: (j, i, 0)),
                  out_shape=jax.ShapeDtypeStruct((4, rh, C), f32), sem=("parallel", "parallel"))(c_idx, g, a)


def _add_chips(p, b, k_idx, name):
    _, rh, C = p.shape
    tr = _best_div(rh, _ADAM_ROWS)

    def body(k_ref, p_ref, b_ref, o_ref):
        del k_ref
        o_ref[...] = ((p_ref[0] + b_ref[0]) + b_ref[1]) + b_ref[2]

    return _pcall(body, name=name, grid=(rh // tr,), prefetch=1,
                  in_specs=[pl.BlockSpec((1, tr, C), lambda i, k: (k[0], i, 0)),
                            pl.BlockSpec((3, tr, C), lambda i, k: (0, i, 0))],
                  out_specs=pl.BlockSpec((tr, C), lambda i, k: (i, 0)),
                  out_shape=jax.ShapeDtypeStruct((rh, C), f32), sem=("parallel",))(k_idx, p, b)


def _reduce_scatter(g, c_idx, k_idx, name):
    a = _d2d_exchange(g, name + "_d2d")
    p = _add_halves(g, a, c_idx, name + "_add2")
    b = _ici_scatter(p, name + "_ici")
    q = _add_chips(p, b, k_idx, name + "_add4")
    return _d2d_swap(q, name + "_swap")


def _allreduce_small(buf, name):
    R, C = buf.shape

    def body(x_ref, o_ref, gat_ref, send_sems, recv_sems):
        x, y, c, _ = _where_am_i()
        me = 4 * x + 2 * y + c
        gat_ref[me] = x_ref[...]
        cps = []
        for f in range(1, 8):
            fx, fy, fc = (f >> 2) & 1, (f >> 1) & 1, f & 1
            peer = ((1 - x) if fx else x, (1 - y) if fy else y, (1 - c) if fc else c)
            cps.append(_rcopy(x_ref, gat_ref.at[me], send_sems, recv_sems, f - 1, peer))
        for cp in cps:
            cp.start()
        for cp in cps:
            cp.wait()
        acc = gat_ref[0]
        for d in range(1, 8):
            acc = acc + gat_ref[d]
        o_ref[...] = acc

    vm = pl.BlockSpec(memory_space=pltpu.VMEM)
    return pl.pallas_call(body, name=name, out_shape=jax.ShapeDtypeStruct((R, C), f32), in_specs=[vm], out_specs=vm,
                          scratch_shapes=[pltpu.VMEM((8, R, C), f32), pltpu.SemaphoreType.DMA((7,)),
                                          pltpu.SemaphoreType.DMA((7,))],
                          compiler_params=pltpu.CompilerParams(vmem_limit_bytes=_VMEM_LIMIT))(buf)


def _pack(arrs, cols=_COMM_COLS):
    flat = jnp.concatenate([a.reshape(-1).astype(f32) for a in arrs])
    per = 8 * cols
    n = -(-flat.shape[0] // per) * per
    return jnp.pad(flat, (0, n - flat.shape[0])).reshape(n // cols, cols)


def _unpack(buf, shapes):
    flat = buf.reshape(-1)
    out, off = [], 0
    for s in shapes:
        n = math.prod(s)
        out.append(flat[off:off + n].reshape(s))
        off += n
    return out


def _to_slab(a, cols=_COMM_COLS, row_mult=32):
    flat = a.reshape(-1)
    per = row_mult * cols
    n = -(-flat.shape[0] // per) * per
    if n != flat.shape[0]:
        flat = jnp.pad(flat, (0, n - flat.shape[0]))
    return flat.reshape(n // cols, cols)


def _from_slab(s, shape):
    n = math.prod(shape)
    flat = s.reshape(-1)
    if flat.shape[0] != n:
        flat = flat[:n]
    return flat.reshape(shape)


_WEIGHTS = ['norm_mix_w', 'norm_ffn_w', 'norm_final_w', 'ev_w_in', 'lru_conv_w', 'lru_conv_b', 'lru_w_r', 'lru_b_r',
            'lru_w_i', 'lru_b_i', 'lru_lambda', 'hg_lower_bounds', 'hg_norm_w', 'ev_w_out', 'ssd_w_in', 'ssd_conv_w',
            'ssd_conv_b', 'ssd_dt_bias', 'ssd_a_log', 'ssd_d', 'ssd_norm_w', 'ssd_w_out', 'ffn_w_up', 'ffn_conv_w',
            'ffn_conv_b', 'ffn_w_down']
_BIG = {'ev_w_in': 2, 'ev_w_out': 1, 'ssd_w_in': 2, 'ssd_w_out': 1, 'ffn_w_up': 2, 'ffn_w_down': 1}
_SMALL_SHARDED = {'lru_conv_w': 2, 'ssd_conv_w': 2, 'ssd_conv_b': 1, 'ssd_norm_w': 1, 'ffn_conv_w': 2}


def _full_shape(shape, ax):
    return tuple(4 * d if i == ax else d for i, d in enumerate(shape))


def kernel(x, norm_mix_w, norm_ffn_w, norm_final_w, ev_w_in, lru_conv_w, lru_conv_b, lru_w_r, lru_b_r, lru_w_i, lru_b_i, lru_lambda, hg_lower_bounds, hg_norm_w, ev_w_out, ssd_w_in, ssd_conv_w, ssd_conv_b, ssd_dt_bias, ssd_a_log, ssd_d, ssd_norm_w, ssd_w_out, ffn_w_up, ffn_conv_w, ffn_conv_b, ffn_w_down, loss_target, m_norm_mix_w, m_norm_ffn_w, m_norm_final_w, m_ev_w_in, m_lru_conv_w, m_lru_conv_b, m_lru_w_r, m_lru_b_r, m_lru_w_i, m_lru_b_i, m_lru_lambda, m_hg_lower_bounds, m_hg_norm_w, m_ev_w_out, m_ssd_w_in, m_ssd_conv_w, m_ssd_conv_b, m_ssd_dt_bias, m_ssd_a_log, m_ssd_d, m_ssd_norm_w, m_ssd_w_out, m_ffn_w_up, m_ffn_conv_w, m_ffn_conv_b, m_ffn_w_down, v_norm_mix_w, v_norm_ffn_w, v_norm_final_w, v_ev_w_in, v_lru_conv_w, v_lru_conv_b, v_lru_w_r, v_lru_b_r, v_lru_w_i, v_lru_b_i, v_lru_lambda, v_hg_lower_bounds, v_hg_norm_w, v_ev_w_out, v_ssd_w_in, v_ssd_conv_w, v_ssd_conv_b, v_ssd_dt_bias, v_ssd_a_log, v_ssd_d, v_ssd_norm_w, v_ssd_w_out, v_ffn_w_up, v_ffn_conv_w, v_ffn_conv_b, v_ffn_w_down):
    given = dict(locals())
    w = {n: given[n] for n in _WEIGHTS}
    m = {n: given["m_" + n] for n in _WEIGHTS}
    v = {n: given["v_" + n] for n in _WEIGHTS}
    cx, cy, cc = lax.axis_index("x"), lax.axis_index("y"), lax.axis_index("c")
    chip = 2 * cx + cy
    c_idx = jnp.reshape(cc, (1,)).astype(jnp.int32)
    k_idx = jnp.reshape(chip, (1,)).astype(jnp.int32)

    full = dict(w)
    for n, ax in _BIG.items():
        gat = _allgather_chips(_to_slab(w[n].astype(_MXU)), "gather_" + n)
        blocks = jnp.stack([_from_slab(gat[kk], w[n].shape) for kk in range(4)])
        full[n] = jnp.moveaxis(blocks, 0, ax).reshape(_full_shape(w[n].shape, ax))
    placed = []
    for n, ax in _SMALL_SHARDED.items():
        z = jnp.zeros(_full_shape(w[n].shape, ax), f32)
        z = lax.dynamic_update_slice_in_dim(z, w[n].astype(f32), chip * w[n].shape[ax], ax)
        placed.append(jnp.where(cc == 0, z, 0.0))
    small_full = _unpack(_allreduce_small(_pack(placed), "gather_small"), [p.shape for p in placed])
    for n, a in zip(_SMALL_SHARDED, small_full):
        full[n] = a

    loss_part, dx, grads = _local_step(x[0], loss_target[0], full)

    small = [n for n in _WEIGHTS if n not in _BIG]
    red = _allreduce_small(_pack([loss_part.reshape(1)] + [grads[n] for n in small]), "allreduce_small")
    red = _unpack(red, [(1,)] + [grads[n].shape for n in small])
    loss = red[0][0]
    gsum = {}
    for n, a in zip(small, red[1:]):
        if n in _SMALL_SHARDED:
            ax = _SMALL_SHARDED[n]
            a = lax.dynamic_slice_in_dim(a, chip * w[n].shape[ax], w[n].shape[ax], ax)
        gsum[n] = a
    outs = _adamw(_pack([w[n] for n in small]), _pack([gsum[n] for n in small]), _pack([m[n] for n in small]),
                  _pack([v[n] for n in small]), "adamw_small")
    shapes = [w[n].shape for n in small]
    delta, new_m, new_v = ({n: a for n, a in zip(small, _unpack(o, shapes))} for o in outs)

    for n, ax in _BIG.items():
        shp = w[n].shape
        g4 = jnp.moveaxis(grads[n].reshape(shp[:ax] + (4, shp[ax]) + shp[ax + 1:]), ax, 0)
        g4 = jnp.stack([_to_slab(g4[kk]) for kk in range(4)])
        gs = _reduce_scatter(g4, c_idx, k_idx, "rs_" + n)
        d_, m_, v_ = _adamw(_to_slab(w[n]), gs, _to_slab(m[n]), _to_slab(v[n]), "adamw_" + n)
        gsum[n] = _from_slab(gs, shp)
        delta[n], new_m[n], new_v[n] = _from_slab(d_, shp), _from_slab(m_, shp), _from_slab(v_, shp)

    return (loss, dx[None], *[gsum[n] for n in _WEIGHTS], *[delta[n] for n in _WEIGHTS],
            *[new_m[n] for n in _WEIGHTS], *[new_v[n] for n in _WEIGHTS])
```

```python
import functools
import math

import jax
import jax.numpy as jnp
from jax import lax
from jax.experimental import pallas as pl
from jax.experimental.pallas import tpu as pltpu

f32 = jnp.float32
bf16 = jnp.bfloat16
_MXU = jnp.bfloat16
_WIRE = jnp.bfloat16
_HI = lax.Precision.HIGHEST

EPS = 1e-6
LRU_C = 8.0
LRU_BLOCK = 64
LRU_CONV = 4
HG_CHUNK = 64
SSD_GROUPS = 8
SSD_STATE = 128
SSD_CONV = 4
SSD_CHUNK = 64
FFN_CONV = 3
ADAM_LR, ADAM_B1, ADAM_B2, ADAM_EPS, ADAM_WD, ADAM_STEP = 0.001, 0.9, 0.999, 1e-08, 0.01, 10

_VMEM_LIMIT = 56 * 1024 * 1024
_LANES = 128
_HALO = 8

_TB_NORM = 512
_TB_FFN = 512
_TB_LRU = 256
_TB_HG = 512
_TB_SSD = 256
_TC_FFN = 512
_TC_LRU = 256
_MM_TM, _MM_TN, _MM_TK = 1024, 1024, 1024
_MM_TK_ONE = 2048
_COMM_COLS = 1024
_SLAB_BLOCK_ELEMS = 384 * 1024


def _pick(n, prefs):
    for p in prefs:
        if p <= n and n % p == 0:
            return p
    return n


def _pcall(body, *, name, grid, in_specs, out_specs, out_shape, scratch=(), sem=None, aliases=None, prefetch=0):
    params = pltpu.CompilerParams(dimension_semantics=sem, vmem_limit_bytes=_VMEM_LIMIT)
    if prefetch:
        gs = pltpu.PrefetchScalarGridSpec(num_scalar_prefetch=prefetch, grid=grid, in_specs=in_specs,
                                          out_specs=out_specs, scratch_shapes=list(scratch))
        return pl.pallas_call(body, name=name, grid_spec=gs, out_shape=out_shape,
                              input_output_aliases=aliases or {}, compiler_params=params)
    return pl.pallas_call(body, name=name, grid=grid, in_specs=in_specs, out_specs=out_specs, out_shape=out_shape,
                          scratch_shapes=list(scratch), input_output_aliases=aliases or {}, compiler_params=params)


def _sig(x):
    return jax.nn.sigmoid(x)


def _softplus(x):
    return jnp.maximum(x, 0.0) + jnp.log(1.0 + jnp.exp(-jnp.abs(x)))


def _neg_expm1(x):
    series = -x * (1.0 + x * (0.5 + x * (1.0 / 6.0 + x * (1.0 / 24.0 + x * (1.0 / 120.0)))))
    return jnp.where(x > -0.05, series, 1.0 - jnp.exp(x))


_GELU_C = math.sqrt(2.0 / math.pi)


def _gelu_and_grad(x):
    inner = _GELU_C * (x + 0.044715 * x * x * x)
    t = jnp.tanh(inner)
    g = 0.5 * x * (1.0 + t)
    dg = 0.5 * (1.0 + t) + 0.5 * x * (1.0 - t * t) * _GELU_C * (1.0 + 3.0 * 0.044715 * x * x)
    return g, dg


def _silu_and_grad(x):
    s = _sig(x)
    return x * s, s * (1.0 + x * (1.0 - s))


def _dot(a, b, dims=(((1,), (0,)), ((), ()))):
    return lax.dot_general(a.astype(_MXU), b.astype(_MXU), dims, preferred_element_type=f32)


def _dot_nt(a, b):
    return _dot(a, b, (((1,), (1,)), ((), ())))


def _dot_tn(a, b):
    return _dot(a, b, (((0,), (0,)), ((), ())))


def _dot_hi(a, b):
    return jnp.dot(a, b, preferred_element_type=f32, precision=_HI)


def _tri(n, lower=True):
    r = lax.broadcasted_iota(jnp.int32, (n, n), 0)
    c = lax.broadcasted_iota(jnp.int32, (n, n), 1)
    return (r >= c) if lower else (r <= c)


def _shift_down(cur, prev, s):
    if s == 0:
        return cur
    n = cur.shape[0]
    ext = jnp.concatenate([prev, cur], axis=0)
    return pltpu.roll(ext, s, axis=0)[_HALO:_HALO + n]


def _shift_up(cur, nxt, s):
    if s == 0:
        return cur
    n = cur.shape[0]
    ext = jnp.concatenate([cur, nxt], axis=0)
    return pltpu.roll(ext, n + _HALO - s, axis=0)[0:n]


def _conv_fwd(cur, prev, w, b, k):
    y = b + w[k - 1] * cur
    for j in range(k - 1):
        y = y + w[j] * _shift_down(cur, prev, k - 1 - j)
    return y


def _conv_bwd(dy, nxt_dy, cur, prev, w, k):
    dx = w[k - 1] * dy
    dws = []
    for j in range(k - 1):
        s = k - 1 - j
        dx = dx + w[j] * _shift_up(dy, nxt_dy, s)
        dws.append(jnp.sum(dy * _shift_down(cur, prev, s), axis=0, keepdims=True))
    dws.append(jnp.sum(dy * cur, axis=0, keepdims=True))
    return dx, dws, jnp.sum(dy, axis=0, keepdims=True)


def _rows(ref, k, cols=None):
    if cols is None:
        return [ref[j:j + 1, :] for j in range(k)]
    return [ref[j:j + 1, cols[0]:cols[1]] for j in range(k)]


def _add_rows(ref, rows, cols=None):
    for j, r in enumerate(rows):
        if cols is None:
            ref[j:j + 1, :] += r
        else:
            ref[j:j + 1, cols[0]:cols[1]] += r


def _mm(a, b, *, M, N, K, ta=False, tb=False, a_off=(0, 0), b_off=(0, 0), resid=None, out_dtype=f32, name):
    tm = _pick(M, (_MM_TM, 512, 256, 128))
    tn = _pick(N, (_MM_TN, 512, 256, 128))
    narrow = a.dtype.itemsize <= 2 and b.dtype.itemsize <= 2
    tk = _pick(K, ((_MM_TK_ONE, _MM_TK, 512, 256, 128) if narrow else (_MM_TK // 2, 256, 128)))
    nk = K // tk
    if ta:
        assert a_off[0] % tk == 0 and a_off[1] % tm == 0
        a_spec = pl.BlockSpec((tk, tm), lambda i, j, k: (k + a_off[0] // tk, i + a_off[1] // tm))
        a_dim = 0
    else:
        assert a_off[0] % tm == 0 and a_off[1] % tk == 0
        a_spec = pl.BlockSpec((tm, tk), lambda i, j, k: (i + a_off[0] // tm, k + a_off[1] // tk))
        a_dim = 1
    if tb:
        assert b_off[0] % tn == 0 and b_off[1] % tk == 0
        b_spec = pl.BlockSpec((tn, tk), lambda i, j, k: (j + b_off[0] // tn, k + b_off[1] // tk))
        b_dim = 1
    else:
        assert b_off[0] % tk == 0 and b_off[1] % tn == 0
        b_spec = pl.BlockSpec((tk, tn), lambda i, j, k: (k + b_off[0] // tk, j + b_off[1] // tn))
        b_dim = 0
    dims = (((a_dim,), (b_dim,)), ((), ()))
    has_resid = resid is not None

    def body_one(*refs):
        a_ref, b_ref = refs[0], refs[1]
        o_ref = refs[-1]
        r = _dot(a_ref[...], b_ref[...], dims)
        if has_resid:
            r = r + refs[2][...].astype(f32)
        o_ref[...] = r.astype(out_dtype)

    def body_acc(*refs):
        a_ref, b_ref = refs[0], refs[1]
        o_ref, acc_ref = refs[-2], refs[-1]
        kk = pl.program_id(2)

        @pl.when(kk == 0)
        def _():
            acc_ref[...] = _dot(a_ref[...], b_ref[...], dims)

        @pl.when(kk > 0)
        def _():
            acc_ref[...] += _dot(a_ref[...], b_ref[...], dims)

        @pl.when(kk == nk - 1)
        def _():
            r = acc_ref[...]
            if has_resid:
                r = r + refs[2][...].astype(f32)
            o_ref[...] = r.astype(out_dtype)

    in_specs = [a_spec, b_spec]
    args = [a, b]
    if has_resid:
        in_specs.append(pl.BlockSpec((tm, tn), lambda i, j, k: (i, j)))
        args.append(resid)
    return _pcall(body_one if nk == 1 else body_acc, name=name, grid=(M // tm, N // tn, nk), in_specs=in_specs,
                  out_specs=pl.BlockSpec((tm, tn), lambda i, j, k: (i, j)),
                  out_shape=jax.ShapeDtypeStruct((M, N), out_dtype),
                  scratch=[] if nk == 1 else [pltpu.VMEM((tm, tn), f32)],
                  sem=("parallel", "parallel", "arbitrary"))(*args)


def _rmsnorm_fwd(x, w, name):
    T, D = x.shape
    tb = _pick(T, (_TB_NORM, 256, 128, 64))

    def body(x_ref, w_ref, o_ref):
        xv = x_ref[...]
        rstd = lax.rsqrt(jnp.mean(xv * xv, axis=-1, keepdims=True) + EPS)
        o_ref[...] = (xv * rstd * w_ref[...]).astype(o_ref.dtype)

    return _pcall(body, name=name, grid=(T // tb,),
                  in_specs=[pl.BlockSpec((tb, D), lambda i: (i, 0)), pl.BlockSpec((1, D), lambda i: (0, 0))],
                  out_specs=pl.BlockSpec((tb, D), lambda i: (i, 0)),
                  out_shape=jax.ShapeDtypeStruct((T, D), _MXU), sem=("parallel",))(x, w.reshape(1, D))


def _rmsnorm_bwd(dh, x, w, dres, name):
    T, D = x.shape
    tb = _pick(T, (_TB_NORM, 256, 128, 64))

    def body(dh_ref, x_ref, w_ref, dres_ref, dx_ref, dxm_ref, dw_ref):
        xv = x_ref[...]
        rstd = lax.rsqrt(jnp.mean(xv * xv, axis=-1, keepdims=True) + EPS)
        xhat = xv * rstd
        dhv = dh_ref[...]
        g = dhv * w_ref[...]
        dx = dres_ref[...] + rstd * (g - xhat * jnp.mean(g * xhat, axis=-1, keepdims=True))
        dx_ref[...] = dx
        dxm_ref[...] = dx.astype(dxm_ref.dtype)

        @pl.when(pl.program_id(0) == 0)
        def _():
            dw_ref[...] = jnp.zeros_like(dw_ref)

        dw_ref[...] += jnp.sum(dhv * xhat, axis=0, keepdims=True)

    row = pl.BlockSpec((tb, D), lambda i: (i, 0))
    vec = pl.BlockSpec((1, D), lambda i: (0, 0))
    return _pcall(body, name=name, grid=(T // tb,), in_specs=[row, row, vec, row], out_specs=[row, row, vec],
                  out_shape=[jax.ShapeDtypeStruct((T, D), f32), jax.ShapeDtypeStruct((T, D), _MXU),
                             jax.ShapeDtypeStruct((1, D), f32)],
                  sem=("arbitrary",))(dh, x, w.reshape(1, D), dres)


def _loss_head(x, w, target):
    T, D = x.shape
    tb = _pick(T, (_TB_NORM, 256, 128, 64))

    def body(x_ref, w_ref, t_ref, loss_ref, dx_ref, dxm_ref, dw_ref):
        xv = x_ref[...]
        rstd = lax.rsqrt(jnp.mean(xv * xv, axis=-1, keepdims=True) + EPS)
        xhat = xv * rstd
        err = xhat * w_ref[...] - t_ref[...]
        dy = err * (1.0 / D)
        g = dy * w_ref[...]
        dx = rstd * (g - xhat * jnp.mean(g * xhat, axis=-1, keepdims=True))
        dx_ref[...] = dx
        dxm_ref[...] = dx.astype(dxm_ref.dtype)

        @pl.when(pl.program_id(0) == 0)
        def _():
            dw_ref[...] = jnp.zeros_like(dw_ref)
            loss_ref[...] = jnp.zeros_like(loss_ref)

        dw_ref[...] += jnp.sum(dy * xhat, axis=0, keepdims=True)
        part = 0.5 * jnp.sum(jnp.sum(err * err, axis=-1, keepdims=True) * (1.0 / D), axis=0, keepdims=True)
        loss_ref[...] += jnp.broadcast_to(part, loss_ref.shape)

    row = pl.BlockSpec((tb, D), lambda i: (i, 0))
    vec = pl.BlockSpec((1, D), lambda i: (0, 0))
    return _pcall(body, name="loss_head", grid=(T // tb,), in_specs=[row, vec, row],
                  out_specs=[pl.BlockSpec((1, _LANES), lambda i: (0, 0)), row, row, vec],
                  out_shape=[jax.ShapeDtypeStruct((1, _LANES), f32), jax.ShapeDtypeStruct((T, D), f32),
                             jax.ShapeDtypeStruct((T, D), _MXU), jax.ShapeDtypeStruct((1, D), f32)],
                  sem=("arbitrary",))(x, w.reshape(1, D), target)


def _halo_spec(tb, width, col_fn):
    r = tb // _HALO
    return pl.BlockSpec((_HALO, width), lambda j, i: (jnp.maximum(i * r - 1, 0), col_fn(j)))


def _ffn_mid_fwd(up, conv_w, conv_b, name):
    T, F2 = up.shape
    tc = _pick(F2 // 2, (_TC_FFN, 256, 128))
    tb = _pick(T, (_TB_FFN, 256, 128, 64))
    nj = F2 // (2 * tc)

    def body(cur_ref, prev_ref, w_ref, b_ref, o_ref):
        cur = cur_ref[...]
        prev = jnp.where(pl.program_id(1) > 0, prev_ref[...], 0.0)
        u = _conv_fwd(cur, prev, _rows(w_ref, FFN_CONV), b_ref[...], FFN_CONV)
        gate, val = u[:, :tc], u[:, tc:]
        o_ref[...] = (gate * _sig(gate) * val).astype(o_ref.dtype)

    return _pcall(body, name=name, grid=(nj, T // tb),
                  in_specs=[pl.BlockSpec((tb, 2 * tc), lambda j, i: (i, j)), _halo_spec(tb, 2 * tc, lambda j: j),
                            pl.BlockSpec((FFN_CONV, 2 * tc), lambda j, i: (0, j)),
                            pl.BlockSpec((1, 2 * tc), lambda j, i: (0, j))],
                  out_specs=pl.BlockSpec((tb, tc), lambda j, i: (i, j)),
                  out_shape=jax.ShapeDtypeStruct((T, F2 // 2), _MXU), sem=("parallel", "parallel"))(
        up, up, conv_w, conv_b)


def _ffn_mid_bwd(up, dact, conv_w, conv_b, name):
    T, F2 = up.shape
    tc = _pick(F2 // 2, (_TC_FFN, 256, 128))
    tb = _pick(T, (_TB_FFN, 256, 128, 64))
    nj, nt = F2 // (2 * tc), T // tb
    r = tb // _HALO

    def body(cur_ref, prev_ref, da_ref, w_ref, b_ref, dup_ref, dw_ref, db_ref, carry_ref):
        i = pl.program_id(1)
        ti = nt - 1 - i
        cur = cur_ref[...]
        prev = jnp.where(ti > 0, prev_ref[...], 0.0)
        w = _rows(w_ref, FFN_CONV)
        u = _conv_fwd(cur, prev, w, b_ref[...], FFN_CONV)
        gate, val = u[:, :tc], u[:, tc:]
        da = da_ref[...]
        sil, dsil = _silu_and_grad(gate)
        du = jnp.concatenate([da * val * dsil, da * sil], axis=1)

        @pl.when(i == 0)
        def _():
            carry_ref[...] = jnp.zeros_like(carry_ref)
            dw_ref[...] = jnp.zeros_like(dw_ref)
            db_ref[...] = jnp.zeros_like(db_ref)

        dx, dw, db = _conv_bwd(du, carry_ref[...], cur, prev, w, FFN_CONV)
        dup_ref[...] = dx.astype(dup_ref.dtype)
        _add_rows(dw_ref, dw)
        db_ref[...] += db
        carry_ref[...] = du[0:_HALO]

    return _pcall(body, name=name, grid=(nj, nt),
                  in_specs=[pl.BlockSpec((tb, 2 * tc), lambda j, i: (nt - 1 - i, j)),
                            pl.BlockSpec((_HALO, 2 * tc), lambda j, i: (jnp.maximum((nt - 1 - i) * r - 1, 0), j)),
                            pl.BlockSpec((tb, tc), lambda j, i: (nt - 1 - i, j)),
                            pl.BlockSpec((FFN_CONV, 2 * tc), lambda j, i: (0, j)),
                            pl.BlockSpec((1, 2 * tc), lambda j, i: (0, j))],
                  out_specs=[pl.BlockSpec((tb, 2 * tc), lambda j, i: (nt - 1 - i, j)),
                             pl.BlockSpec((FFN_CONV, 2 * tc), lambda j, i: (0, j)),
                             pl.BlockSpec((1, 2 * tc), lambda j, i: (0, j))],
                  out_shape=[jax.ShapeDtypeStruct((T, F2), _MXU), jax.ShapeDtypeStruct((FFN_CONV, F2), f32),
                             jax.ShapeDtypeStruct((1, F2), f32)],
                  scratch=[pltpu.VMEM((_HALO, 2 * tc), f32)], sem=("parallel", "arbitrary"))(
        up, up, dact, conv_w, conv_b)


def _scan_fwd(a, u):
    n = a.shape[0]
    row = lax.broadcasted_iota(jnp.int32, a.shape, 0)
    d = 1
    while d < n:
        a_s = jnp.where(row >= d, pltpu.roll(a, d, axis=0), 1.0)
        u_s = jnp.where(row >= d, pltpu.roll(u, d, axis=0), 0.0)
        u = u + a * u_s
        a = a * a_s
        d *= 2
    return a, u


def _scan_bwd(m, v):
    n = m.shape[0]
    row = lax.broadcasted_iota(jnp.int32, m.shape, 0)
    d = 1
    while d < n:
        ok = row + d < n
        m_s = jnp.where(ok, pltpu.roll(m, n - d, axis=0), 1.0)
        v_s = jnp.where(ok, pltpu.roll(v, n - d, axis=0), 0.0)
        v = v + m * v_s
        m = m * m_s
        d *= 2
    return v


def _lru_gates(xc, wr, br, wi, bi, lam):
    r = _sig(_dot(xc, wr) + br)
    gi = _sig(_dot(xc, wi) + bi)
    sp = _softplus(-lam)
    log_a = -LRU_C * r * sp
    a = jnp.exp(log_a)
    mult = jnp.sqrt(_neg_expm1(2.0 * log_a))
    return r, gi, sp, a, mult


def _lru_fwd(proj, conv_w, conv_b, wr, br, wi, bi, lam, lw, name):
    T = proj.shape[0]
    tc = _pick(lw, (_TC_LRU, 128))
    tb = _pick(T, (_TB_LRU, 128, 64))
    nj = lw // tc

    def body(cur_ref, prev_ref, cw_ref, cb_ref, wr_ref, br_ref, wi_ref, bi_ref, lam_ref, ya_ref, h_ref, carry_ref):
        i = pl.program_id(1)

        @pl.when(i == 0)
        def _():
            carry_ref[...] = jnp.zeros_like(carry_ref)

        xa = cur_ref[:, :tc]
        ga = cur_ref[:, tc:]
        prev = jnp.where(i > 0, prev_ref[:, :tc], 0.0)
        xc = _conv_fwd(xa, prev, _rows(cw_ref, LRU_CONV), cb_ref[...], LRU_CONV)
        r, gi, sp, a, mult = _lru_gates(xc, wr_ref[0], br_ref[...], wi_ref[0], bi_ref[...], lam_ref[...])
        acum, h = _scan_fwd(a, mult * gi * xc)
        h = h + acum * carry_ref[0:1, :]
        h_ref[...] = h
        gel, _ = _gelu_and_grad(ga)
        ya_ref[...] = (h * gel).astype(ya_ref.dtype)
        carry_ref[...] = pltpu.roll(h, 1, axis=0)[0:_HALO]

    vec = pl.BlockSpec((1, tc), lambda j, i: (0, j))
    mat = pl.BlockSpec((1, tc, tc), lambda j, i: (j, 0, 0))
    out = pl.BlockSpec((tb, tc), lambda j, i: (i, j))
    return _pcall(body, name=name, grid=(nj, T // tb),
                  in_specs=[pl.BlockSpec((tb, 2 * tc), lambda j, i: (i, j)), _halo_spec(tb, 2 * tc, lambda j: j),
                            pl.BlockSpec((LRU_CONV, tc), lambda j, i: (0, j)), vec, mat, vec, mat, vec, vec],
                  out_specs=[out, out],
                  out_shape=[jax.ShapeDtypeStruct((T, lw), _MXU), jax.ShapeDtypeStruct((T, lw), f32)],
                  scratch=[pltpu.VMEM((_HALO, tc), f32)], sem=("parallel", "arbitrary"))(
        proj, proj, conv_w, conv_b, wr, br, wi, bi, lam)


def _lru_bwd(proj, h, dy, conv_w, conv_b, wr, br, wi, bi, lam, lw, name):
    T, width = proj.shape
    tc = _pick(lw, (_TC_LRU, 128))
    tb = _pick(T, (_TB_LRU, 128, 64))
    nj, nt = lw // tc, T // tb
    rr = tb // _HALO

    def body(cur_ref, prev_ref, h_ref, hp_ref, dy_ref, cw_ref, cb_ref, wr_ref, br_ref, wi_ref, bi_ref, lam_ref,
             dp_ref, dcw_ref, dcb_ref, dwr_ref, dbr_ref, dwi_ref, dbi_ref, dlam_ref, cl_ref, cx_ref):
        i = pl.program_id(1)
        ti = nt - 1 - i

        @pl.when(i == 0)
        def _():
            for ref in (cl_ref, cx_ref, dcw_ref, dcb_ref, dwr_ref, dbr_ref, dwi_ref, dbi_ref, dlam_ref):
                ref[...] = jnp.zeros_like(ref)

        xa = cur_ref[:, :tc]
        ga = cur_ref[:, tc:]
        prev = jnp.where(ti > 0, prev_ref[:, :tc], 0.0)
        cw = _rows(cw_ref, LRU_CONV)
        xc = _conv_fwd(xa, prev, cw, cb_ref[...], LRU_CONV)
        wrv, wiv, lam_v = wr_ref[0], wi_ref[0], lam_ref[...]
        r, gi, sp, a, mult = _lru_gates(xc, wrv, br_ref[...], wiv, bi_ref[...], lam_v)
        hcur = h_ref[...]
        hprev = _shift_down(hcur, jnp.where(ti > 0, hp_ref[...], 0.0), 1)
        gel, dgel = _gelu_and_grad(ga)
        dya = dy_ref[...]
        dga = dya * hcur * dgel
        row = lax.broadcasted_iota(jnp.int32, a.shape, 0)
        v = dya * gel + jnp.where(row == tb - 1, cl_ref[0:1, :], 0.0)
        m = jnp.where(row < tb - 1, pltpu.roll(a, tb - 1, axis=0), 0.0)
        lamb = _scan_bwd(m, v)
        cl_ref[...] = (a * lamb)[0:_HALO]
        da = lamb * hprev
        dmult = lamb * gi * xc
        dgi = lamb * mult * xc
        dxc = lamb * mult * gi
        dlog_a = da * a - dmult * (a * a) / mult
        dr = dlog_a * (-LRU_C * sp)
        dlam_ref[...] += jnp.sum(dlog_a * (-LRU_C) * r, axis=0, keepdims=True) * (-_sig(-lam_v))
        dpr = dr * r * (1.0 - r)
        dpi = dgi * gi * (1.0 - gi)
        dbr_ref[...] += jnp.sum(dpr, axis=0, keepdims=True)
        dbi_ref[...] += jnp.sum(dpi, axis=0, keepdims=True)
        dwr_ref[0] += _dot_tn(xc, dpr)
        dwi_ref[0] += _dot_tn(xc, dpi)
        dxc = dxc + _dot_nt(dpr, wrv) + _dot_nt(dpi, wiv)
        dxa, dws, db = _conv_bwd(dxc, cx_ref[...], xa, prev, cw, LRU_CONV)
        cx_ref[...] = dxc[0:_HALO]
        _add_rows(dcw_ref, dws)
        dcb_ref[...] += db
        dp_ref[...] = jnp.concatenate([dxa, dga], axis=1).astype(dp_ref.dtype)

    vec = pl.BlockSpec((1, tc), lambda j, i: (0, j))
    mat = pl.BlockSpec((1, tc, tc), lambda j, i: (j, 0, 0))
    cwspec = pl.BlockSpec((LRU_CONV, tc), lambda j, i: (0, j))
    blk = pl.BlockSpec((tb, tc), lambda j, i: (nt - 1 - i, j))
    vshape = jax.ShapeDtypeStruct((1, lw), f32)
    mshape = jax.ShapeDtypeStruct((nj, tc, tc), f32)
    return _pcall(body, name=name, grid=(nj, nt),
                  in_specs=[pl.BlockSpec((tb, 2 * tc), lambda j, i: (nt - 1 - i, j)),
                            pl.BlockSpec((_HALO, 2 * tc), lambda j, i: (jnp.maximum((nt - 1 - i) * rr - 1, 0), j)),
                            blk,
                            pl.BlockSpec((_HALO, tc), lambda j, i: (jnp.maximum((nt - 1 - i) * rr - 1, 0), j)),
                            blk, cwspec, vec, mat, vec, mat, vec, vec],
                  out_specs=[pl.BlockSpec((tb, 2 * tc), lambda j, i: (nt - 1 - i, j)), cwspec, vec, mat, vec, mat, vec,
                             vec],
                  out_shape=[jax.ShapeDtypeStruct((T, width), _MXU), jax.ShapeDtypeStruct((LRU_CONV, lw), f32),
                             vshape, mshape, vshape, mshape, vshape, vshape],
                  scratch=[pltpu.VMEM((_HALO, tc), f32), pltpu.VMEM((_HALO, tc), f32)],
                  sem=("parallel", "arbitrary"))(
        proj, proj, h, h, dy, conv_w, conv_b, wr, br, wi, bi, lam)


def _hg_chunk_fwd(blk, lb, hd):
    q, fr, v, gt = blk[:, 0:hd], blk[:, hd:2 * hd], blk[:, 2 * hd:3 * hd], blk[:, 3 * hd:4 * hd]
    sg = _sig(fr)
    f = lb + (1.0 - lb) * sg
    k = (1.0 - lb) * (1.0 - sg)
    g = jnp.log(f)
    n = HG_CHUNK
    cum = _dot_hi(_tri(n).astype(f32), g)
    row = lax.broadcasted_iota(jnp.int32, g.shape, 0)
    tot = jnp.sum(g, axis=0, keepdims=True)
    mid = jnp.sum(jnp.where(row < n // 2, g, 0.0), axis=0, keepdims=True)
    sq = _sig(q)
    qs = q * sq * (hd ** -0.5)
    e_qd, e_kd, e_qc, e_kt = jnp.exp(cum - mid), jnp.exp(mid - cum), jnp.exp(cum), jnp.exp(tot - cum)
    qd, kd, qc, kt = qs * e_qd, k * e_kd, qs * e_qc, k * e_kt
    p = jnp.where(_tri(n), _dot_nt(qd, kd), 0.0)
    return dict(q=q, v=v, gt=gt, sg=sg, f=f, k=k, sq=sq, qs=qs, tot=tot, e_qd=e_qd, e_kd=e_kd, e_qc=e_qc,
                e_kt=e_kt, qd=qd, kd=kd, qc=qc, kt=kt, p=p)


def _hgrn_fwd(proj, lb, nw, col0, nh, hd, name):
    T = proj.shape[0]
    tb = _pick(T, (_TB_HG, 256, 128, 64))
    ncb = tb // HG_CHUNK
    cb0 = col0 // (4 * hd)

    def body(p_ref, lb_ref, nw_ref, y_ref, sc_ref, st_ref):
        @pl.when(pl.program_id(1) == 0)
        def _():
            st_ref[...] = jnp.zeros_like(st_ref)

        lbv, nwv = lb_ref[...], nw_ref[...]

        def chunk(c, carry):
            r0 = pl.multiple_of(c * HG_CHUNK, HG_CHUNK)
            d = _hg_chunk_fwd(p_ref[pl.ds(r0, HG_CHUNK), :], lbv, hd)
            st = st_ref[...]
            sc_ref[c, 0] = st
            o = _dot(d["p"], d["v"]) + _dot_nt(d["qc"], st)
            st_ref[...] = st * jnp.exp(d["tot"]) + _dot_tn(d["v"], d["kt"])
            rstd = lax.rsqrt(jnp.mean(o * o, axis=-1, keepdims=True) + EPS)
            sil, _ = _silu_and_grad(d["gt"])
            y_ref[pl.ds(r0, HG_CHUNK), :] = (o * rstd * nwv * sil).astype(y_ref.dtype)
            return carry

        lax.fori_loop(0, ncb, chunk, 0)

    return _pcall(body, name=name, grid=(nh, T // tb),
                  in_specs=[pl.BlockSpec((tb, 4 * hd), lambda h, i: (i, cb0 + h)),
                            pl.BlockSpec((1, hd), lambda h, i: (0, h)), pl.BlockSpec((1, hd), lambda h, i: (0, 0))],
                  out_specs=[pl.BlockSpec((tb, hd), lambda h, i: (i, h)),
                             pl.BlockSpec((ncb, 1, hd, hd), lambda h, i: (i, h, 0, 0))],
                  out_shape=[jax.ShapeDtypeStruct((T, nh * hd), _MXU),
                             jax.ShapeDtypeStruct((T // HG_CHUNK, nh, hd, hd), f32)],
                  scratch=[pltpu.VMEM((hd, hd), f32)], sem=("parallel", "arbitrary"))(proj, lb, nw)


def _hgrn_bwd(proj, st_chunks, dy, dproj, lb, nw, col0, dycol0, nh, hd, name):
    T = proj.shape[0]
    tb = _pick(T, (_TB_HG, 256, 128, 64))
    ncb = tb // HG_CHUNK
    nt = T // tb
    cb0 = col0 // (4 * hd)
    dyb0 = dycol0 // hd
    n = HG_CHUNK

    def body(p_ref, sc_ref, dy_ref, dpin_ref, lb_ref, nw_ref, dp_ref, dlb_ref, dnw_ref, dst_ref):
        del dpin_ref

        @pl.when(pl.program_id(1) == 0)
        def _():
            dst_ref[...] = jnp.zeros_like(dst_ref)
            dlb_ref[...] = jnp.zeros_like(dlb_ref)
            dnw_ref[...] = jnp.zeros_like(dnw_ref)

        lbv, nwv = lb_ref[...], nw_ref[...]
        tril = _tri(n)

        def chunk(cc, carry):
            c = ncb - 1 - cc
            r0 = pl.multiple_of(c * HG_CHUNK, HG_CHUNK)
            d = _hg_chunk_fwd(p_ref[pl.ds(r0, HG_CHUNK), :], lbv, hd)
            st = sc_ref[c, 0]
            v, gt = d["v"], d["gt"]
            o = _dot(d["p"], v) + _dot_nt(d["qc"], st)
            rstd = lax.rsqrt(jnp.mean(o * o, axis=-1, keepdims=True) + EPS)
            sil, dsil = _silu_and_grad(gt)
            dyv = dy_ref[pl.ds(r0, HG_CHUNK), :]
            xhat = o * rstd
            don = dyv * sil
            dgt = dyv * xhat * nwv * dsil
            dnw_ref[0] += jnp.sum(don * xhat, axis=0, keepdims=True)
            g2 = don * nwv
            do = rstd * (g2 - xhat * jnp.mean(g2 * xhat, axis=-1, keepdims=True))
            dstn = dst_ref[...]
            dp = jnp.where(tril, _dot_nt(do, v), 0.0)
            dv = _dot_tn(d["p"], do) + _dot_nt(d["kt"], dstn)
            dqd = _dot(dp, d["kd"])
            dkd = _dot_tn(dp, d["qd"])
            dqc = _dot(do, st)
            dkt = _dot(v, dstn)
            e_tot = jnp.exp(d["tot"])
            dst_ref[...] = dstn * e_tot + _dot_tn(do, d["qc"])
            dtot = jnp.sum(e_tot * st * dstn, axis=0, keepdims=True) + jnp.sum(dkt * d["kt"], axis=0, keepdims=True)
            dqs = dqd * d["e_qd"] + dqc * d["e_qc"]
            dk = dkd * d["e_kd"] + dkt * d["e_kt"]
            dcum = dqd * d["qd"] - dkd * d["kd"] + dqc * d["qc"] - dkt * d["kt"]
            dg = _dot_hi(_tri(n, lower=False).astype(f32), dcum) + dtot
            df = dg / d["f"] - dk
            sg = d["sg"]
            dfr = df * (1.0 - lbv) * sg * (1.0 - sg)
            dlb_ref[...] += jnp.sum(df * (1.0 - sg), axis=0, keepdims=True)
            sq, q = d["sq"], d["q"]
            dq = dqs * (hd ** -0.5) * sq * (1.0 + q * (1.0 - sq))
            dp_ref[pl.ds(r0, HG_CHUNK), :] = jnp.concatenate([dq, dfr, dv, dgt], axis=1).astype(dp_ref.dtype)
            return carry

        lax.fori_loop(0, ncb, chunk, 0)

    width = dproj.shape[1]
    return _pcall(body, name=name, grid=(nh, nt),
                  in_specs=[pl.BlockSpec((tb, 4 * hd), lambda h, i: (nt - 1 - i, cb0 + h)),
                            pl.BlockSpec((ncb, 1, hd, hd), lambda h, i: (nt - 1 - i, h, 0, 0)),
                            pl.BlockSpec((tb, hd), lambda h, i: (nt - 1 - i, dyb0 + h)),
                            pl.BlockSpec(memory_space=pl.ANY),
                            pl.BlockSpec((1, hd), lambda h, i: (0, h)), pl.BlockSpec((1, hd), lambda h, i: (0, 0))],
                  out_specs=[pl.BlockSpec((tb, 4 * hd), lambda h, i: (nt - 1 - i, cb0 + h)),
                             pl.BlockSpec((1, hd), lambda h, i: (0, h)),
                             pl.BlockSpec((1, 1, hd), lambda h, i: (h, 0, 0))],
                  out_shape=[jax.ShapeDtypeStruct((T, width), _MXU), jax.ShapeDtypeStruct((1, nh * hd), f32),
                             jax.ShapeDtypeStruct((nh, 1, hd), f32)],
                  scratch=[pltpu.VMEM((hd, hd), f32)], sem=("parallel", "arbitrary"), aliases={3: 0})(
        proj, st_chunks, dy, dproj, lb, nw)


def _ssd_dt(dtraw_c, g, hpg, dtb_row, dtb_col, a_row, a_col):
    hp = dtraw_c.shape[1]
    sel = (lax.broadcasted_iota(jnp.int32, (hp, hpg), 0) == g * hpg + lax.broadcasted_iota(jnp.int32, (hp, hpg), 1))
    selt = (lax.broadcasted_iota(jnp.int32, (hpg, hp), 1) == g * hpg + lax.broadcasted_iota(jnp.int32, (hpg, hp), 0))
    pre = _dot_hi(dtraw_c, sel.astype(f32)) + dtb_row
    pre_t = lax.dot_general(selt.astype(f32), dtraw_c, (((1,), (1,)), ((), ())), preferred_element_type=f32,
                            precision=_HI) + dtb_col
    dt = _softplus(pre)
    dt_t = _softplus(pre_t)
    return pre, dt, dt * a_row, dt_t * a_col


def _ssd_chunk_fwd(xbc, z, dtraw_c, g, prm, s_ref, hpg, gw):
    n, ns, pp = SSD_CHUNK, SSD_STATE, 2 * (gw // hpg)
    dtb_row, dtb_col, a_row, a_col, d_row = prm
    bm, cm = xbc[:, gw:gw + ns], xbc[:, gw + ns:gw + 2 * ns]
    pre, dt, a, a_t = _ssd_dt(dtraw_c, g, hpg, dtb_row, dtb_col, a_row, a_col)
    cum = _dot_hi(_tri(n).astype(f32), a)
    cum_t = _dot_hi(a_t, _tri(n, lower=False).astype(f32))
    tot = jnp.sum(a, axis=0, keepdims=True)
    cb = _dot_nt(cm, bm)
    tril = _tri(n)
    lo = lax.broadcasted_iota(jnp.int32, (n, pp), 1) < pp // 2
    sub_lo = lax.broadcasted_iota(jnp.int32, (pp, ns), 0) < pp // 2

    def per_lane(m, h0):
        return jnp.where(lo[0:m.shape[0]], m[:, h0:h0 + 1], m[:, h0 + 1:h0 + 2])

    pairs = []
    for p in range(hpg // 2):
        h0 = 2 * p
        xp = xbc[:, pp * p:pp * (p + 1)]
        dtl = per_lane(dt, h0)
        xdt = xp * dtl
        lm, mm = [], []
        for h in (h0, h0 + 1):
            l_h = jnp.where(tril, jnp.exp(jnp.minimum(cum[:, h:h + 1] - cum_t[h:h + 1, :], 0.0)), 0.0)
            lm.append(l_h)
            mm.append(cb * l_h)
        y_diag = _dot(mm[0], jnp.where(lo, xdt, 0.0)) + _dot(mm[1], jnp.where(lo, 0.0, xdt))
        sp = s_ref[p]
        e = jnp.exp(per_lane(cum, h0))
        y_off = _dot_nt(cm, sp) * e
        fd = jnp.exp(per_lane(tot - cum, h0))
        u = xdt * fd
        etot = jnp.exp(tot)
        etot_col = jnp.where(sub_lo, etot[:, h0:h0 + 1], etot[:, h0 + 1:h0 + 2])
        dl = per_lane(d_row, h0)
        y = y_diag + y_off + xp * dl
        zp = z[:, pp * p:pp * (p + 1)]
        sil, dsil = _silu_and_grad(zp)
        pairs.append(dict(xp=xp, dtl=dtl, xdt=xdt, lm=lm, mm=mm, y_diag=y_diag, sp=sp, e=e, y_off=y_off, fd=fd, u=u,
                          etot_col=etot_col, dl=dl, y=y, sil=sil, dsil=dsil, y2=y * sil))
    ms = sum(jnp.sum(q["y2"] * q["y2"], axis=-1, keepdims=True) for q in pairs) * (1.0 / gw)
    rstd = lax.rsqrt(ms + EPS)
    return dict(bm=bm, cm=cm, pre=pre, dt=dt, a=a, cum=cum, cum_t=cum_t, tot=tot, cb=cb, lo=lo, sub_lo=sub_lo,
                pairs=pairs, rstd=rstd)


def _ssd_specs(tb, gwx, hpg, hp, cw, gw, tmap):
    r = tb // _HALO
    row = pl.BlockSpec((1, 1, hpg), lambda g, i: (g, 0, 0))
    col = pl.BlockSpec((1, hpg, 1), lambda g, i: (g, 0, 0))
    return [pl.BlockSpec((tb, gwx), lambda g, i: (tmap(i), g)),
            pl.BlockSpec((_HALO, gwx), lambda g, i: (jnp.maximum(tmap(i) * r - 1, 0), g)),
            pl.BlockSpec((tb, hp), lambda g, i: (tmap(i), 0)),
            pl.BlockSpec((1, SSD_CONV, cw), lambda g, i: (g, 0, 0)),
            pl.BlockSpec((1, 1, cw), lambda g, i: (g, 0, 0)),
            row, col, row, col, row,
            pl.BlockSpec((1, gw), lambda g, i: (0, g))]


def _ssd_fwd(proj, dt_raw, conv_w, conv_b, dtb_row, dtb_col, a_row, a_col, d_row, norm_w, hpg, name):
    T = proj.shape[0]
    G = SSD_GROUPS
    gwx = proj.shape[1] // G
    gw = (gwx - 2 * SSD_STATE) // 2
    cw = gw + 2 * SSD_STATE
    hp = dt_raw.shape[1]
    tb = _pick(T, (_TB_SSD, 128, 64))
    ncb = tb // SSD_CHUNK
    npair = hpg // 2
    pp = 2 * (gw // hpg)
    assert pp == _LANES and hpg % 2 == 0

    def body(p_ref, prev_ref, dt_ref, cw_ref, cb_ref, dtbr_ref, dtbc_ref, ar_ref, ac_ref, dr_ref, nw_ref,
             y_ref, sc_ref, s_ref, xbc_ref):
        g = pl.program_id(0)
        i = pl.program_id(1)

        @pl.when(i == 0)
        def _():
            s_ref[...] = jnp.zeros_like(s_ref)

        raw = p_ref[:, gw:]
        prev = jnp.where(i > 0, prev_ref[:, gw:], 0.0)
        pre = _conv_fwd(raw, prev, [cw_ref[0, j:j + 1, :] for j in range(SSD_CONV)], cb_ref[0], SSD_CONV)
        xbc_ref[...] = pre * _sig(pre)
        prm = (dtbr_ref[0], dtbc_ref[0], ar_ref[0], ac_ref[0], dr_ref[0])
        nwv = nw_ref[...]

        def chunk(c, carry):
            r0 = pl.multiple_of(c * SSD_CHUNK, SSD_CHUNK)
            rows = pl.ds(r0, SSD_CHUNK)
            sc_ref[c, 0] = s_ref[...]
            d = _ssd_chunk_fwd(xbc_ref[rows, :], p_ref[rows, 0:gw], dt_ref[rows, :], g, prm, s_ref, hpg, gw)
            for p, q in enumerate(d["pairs"]):
                s_ref[p] = q["sp"] * q["etot_col"] + _dot_tn(q["u"], d["bm"])
                y_ref[rows, pp * p:pp * (p + 1)] = (q["y2"] * d["rstd"] * nwv[:, pp * p:pp * (p + 1)]).astype(
                    y_ref.dtype)
            return carry

        lax.fori_loop(0, ncb, chunk, 0)

    return _pcall(body, name=name, grid=(G, T // tb),
                  in_specs=_ssd_specs(tb, gwx, hpg, hp, cw, gw, lambda i: i),
                  out_specs=[pl.BlockSpec((tb, gw), lambda g, i: (i, g)),
                             pl.BlockSpec((ncb, 1, npair, pp, SSD_STATE), lambda g, i: (i, g, 0, 0, 0))],
                  out_shape=[jax.ShapeDtypeStruct((T, G * gw), _MXU),
                             jax.ShapeDtypeStruct((T // SSD_CHUNK, G, npair, pp, SSD_STATE), f32)],
                  scratch=[pltpu.VMEM((npair, pp, SSD_STATE), f32), pltpu.VMEM((tb, cw), f32)],
                  sem=("parallel", "arbitrary"))(
        proj, proj, dt_raw, conv_w, conv_b, dtb_row, dtb_col, a_row, a_col, d_row, norm_w)


def _ssd_bwd(proj, dt_raw, s_chunks, dy, conv_w, conv_b, dtb_row, dtb_col, a_row, a_col, d_row, norm_w, hpg, name):
    T = proj.shape[0]
    G = SSD_GROUPS
    gwx = proj.shape[1] // G
    gw = (gwx - 2 * SSD_STATE) // 2
    ns = SSD_STATE
    cw = gw + 2 * ns
    hp = dt_raw.shape[1]
    tb = _pick(T, (_TB_SSD, 128, 64))
    ncb, nt = tb // SSD_CHUNK, T // tb
    npair = hpg // 2
    pp = 2 * (gw // hpg)
    n = SSD_CHUNK

    def body(p_ref, prev_ref, dt_ref, cw_ref, cb_ref, dtbr_ref, dtbc_ref, ar_ref, ac_ref, dr_ref, nw_ref,
             sc_ref, dy_ref,
             dp_ref, ddt_ref, dcw_ref, dcb_ref, ddtb_ref, da_ref, dd_ref, dnw_ref,
             ds_ref, s_ref, xbc_ref, dxbc_ref, carry_ref):
        g = pl.program_id(0)
        i = pl.program_id(1)
        ti = nt - 1 - i

        @pl.when(i == 0)
        def _():
            for ref in (ds_ref, carry_ref, dcw_ref, dcb_ref, ddtb_ref, da_ref, dd_ref, dnw_ref):
                ref[...] = jnp.zeros_like(ref)

        raw = p_ref[:, gw:]
        prev = jnp.where(ti > 0, prev_ref[:, gw:], 0.0)
        cwr = [cw_ref[0, j:j + 1, :] for j in range(SSD_CONV)]
        pre = _conv_fwd(raw, prev, cwr, cb_ref[0], SSD_CONV)
        xbc_ref[...] = pre * _sig(pre)
        prm = (dtbr_ref[0], dtbc_ref[0], ar_ref[0], ac_ref[0], dr_ref[0])
        a_row = ar_ref[0]
        nwv = nw_ref[...]
        tril = _tri(n)
        lane8 = lax.broadcasted_iota(jnp.int32, (n, hpg), 1)

        def chunk(cc, carry):
            c = ncb - 1 - cc
            r0 = pl.multiple_of(c * SSD_CHUNK, SSD_CHUNK)
            rows = pl.ds(r0, SSD_CHUNK)
            s_ref[...] = sc_ref[c, 0]
            xbc = xbc_ref[rows, :]
            d = _ssd_chunk_fwd(xbc, p_ref[rows, 0:gw], dt_ref[rows, :], g, prm, s_ref, hpg, gw)
            bm, cm, lo, sub_lo, rstd = d["bm"], d["cm"], d["lo"], d["sub_lo"], d["rstd"]
            dyo = dy_ref[rows, :]
            s1 = sum(jnp.sum(dyo[:, pp * p:pp * (p + 1)] * nwv[:, pp * p:pp * (p + 1)] * q["y2"], axis=-1,
                             keepdims=True) for p, q in enumerate(d["pairs"]))
            dcb = jnp.zeros((n, n), f32)
            cb_t = _dot_nt(bm, cm)
            triu = _tri(n, lower=False)
            dbm = jnp.zeros((n, ns), f32)
            dcm = jnp.zeros((n, ns), f32)
            dcum = jnp.zeros((n, hpg), f32)
            ddt_dir = jnp.zeros((n, hpg), f32)
            dtot = jnp.zeros((1, hpg), f32)
            ddv = jnp.zeros((1, hpg), f32)
            dxs = []
            for p, q in enumerate(d["pairs"]):
                h0 = 2 * p
                sl = slice(pp * p, pp * (p + 1))
                dyp = dyo[:, sl]
                nwp = nwv[:, sl]
                dnw_ref[:, sl] += jnp.sum(dyp * q["y2"] * rstd, axis=0, keepdims=True)
                dy2 = rstd * (dyp * nwp - q["y2"] * (rstd * rstd) * s1 * (1.0 / gw))
                dyv = dy2 * q["sil"]
                dz = dy2 * q["y"] * q["dsil"]
                dp_ref[rows, sl] = dz.astype(dp_ref.dtype)
                xp, xdt, mm, lm = q["xp"], q["xdt"], q["mm"], q["lm"]
                dyx = dyv * xp
                dx = dyv * q["dl"]
                dxdt_diag = jnp.where(lo, _dot_tn(mm[0], dyv), _dot_tn(mm[1], dyv))
                dy_lo, dy_hi = jnp.where(lo, dyv, 0.0), jnp.where(lo, 0.0, dyv)
                dm0 = jnp.where(tril, _dot_nt(dy_lo, xdt), 0.0)
                dm1 = jnp.where(tril, _dot_nt(dy_hi, xdt), 0.0)
                dcb = dcb + dm0 * lm[0] + dm1 * lm[1]
                seg = []
                for hh, (dm, dyh) in enumerate(((dm0, dy_lo), (dm1, dy_hi))):
                    h = h0 + hh
                    l_t = jnp.where(triu, jnp.exp(jnp.minimum(d["cum_t"][h:h + 1, :] - d["cum"][:, h:h + 1], 0.0)), 0.0)
                    dm_t = _dot_nt(xdt, dyh)
                    seg.append(jnp.sum(dm * mm[hh], axis=1, keepdims=True)
                               - jnp.sum(dm_t * (cb_t * l_t), axis=1, keepdims=True))
                dsn = ds_ref[p]
                dzz = dyv * q["e"]
                dcm = dcm + _dot(dzz, q["sp"])
                ds_ref[p] = dsn * q["etot_col"] + _dot_tn(dzz, cm)
                du = _dot_nt(bm, dsn)
                dbm = dbm + _dot(q["u"], dsn)
                dxdt = dxdt_diag + du * q["fd"]
                wv = du * q["u"]
                cterm = dyv * q["y_off"] - wv
                ddir = dxdt * xp
                sdec = q["etot_col"] * q["sp"] * dsn

                def halves(x):
                    lo_s = jnp.sum(jnp.where(lo, x, 0.0), axis=1, keepdims=True)
                    return lo_s, jnp.sum(x, axis=1, keepdims=True) - lo_s

                c_lo, c_hi = halves(cterm)
                d_lo, d_hi = halves(ddir)
                w_lo, w_hi = halves(wv)
                x_lo, x_hi = halves(dyx)
                s_all = jnp.sum(sdec, axis=1, keepdims=True)
                s_lo = jnp.sum(jnp.where(sub_lo[:, 0:1], s_all, 0.0), axis=0, keepdims=True)
                s_hi = jnp.sum(s_all, axis=0, keepdims=True) - s_lo
                for h, cv, dv_, wv_, xv_, sv_ in ((h0, c_lo + seg[0], d_lo, w_lo, x_lo, s_lo),
                                                  (h0 + 1, c_hi + seg[1], d_hi, w_hi, x_hi, s_hi)):
                    sel = lane8 == h
                    dcum = dcum + jnp.where(sel, cv, 0.0)
                    ddt_dir = ddt_dir + jnp.where(sel, dv_, 0.0)
                    dtot = dtot + jnp.where(sel[0:1], jnp.sum(wv_, axis=0, keepdims=True) + sv_, 0.0)
                    ddv = ddv + jnp.where(sel[0:1], jnp.sum(xv_, axis=0, keepdims=True), 0.0)
                dxs.append(dx + dxdt * q["dtl"])
            da_c = _dot_hi(_tri(n, lower=False).astype(f32), dcum) + dtot
            ddt = da_c * a_row + ddt_dir
            da_ref[0] += jnp.sum(da_c * d["dt"], axis=0, keepdims=True)
            dd_ref[0] += ddv
            ddt_raw = ddt * _sig(d["pre"])
            ddtb_ref[0] += jnp.sum(ddt_raw, axis=0, keepdims=True)
            ddt_ref[0, rows, :] = ddt_raw
            dcm = dcm + _dot(dcb, bm)
            dbm = dbm + _dot_tn(dcb, cm)
            dxbc_ref[rows, :] = jnp.concatenate(dxs + [dbm, dcm], axis=1)
            return carry

        lax.fori_loop(0, ncb, chunk, 0)
        _, dsil = _silu_and_grad(pre)
        dpre = dxbc_ref[...] * dsil
        draw, dws, db = _conv_bwd(dpre, carry_ref[...], raw, prev, cwr, SSD_CONV)
        carry_ref[...] = dpre[0:_HALO]
        for j, r_ in enumerate(dws):
            dcw_ref[0, j:j + 1, :] += r_
        dcb_ref[0] += db
        dp_ref[:, gw:] = draw.astype(dp_ref.dtype)

    tmap = lambda i: nt - 1 - i
    row = pl.BlockSpec((1, 1, hpg), lambda g, i: (g, 0, 0))
    rshape = jax.ShapeDtypeStruct((G, 1, hpg), f32)
    return _pcall(body, name=name, grid=(G, nt),
                  in_specs=_ssd_specs(tb, gwx, hpg, hp, cw, gw, tmap) + [
                      pl.BlockSpec((ncb, 1, npair, pp, ns), lambda g, i: (tmap(i), g, 0, 0, 0)),
                      pl.BlockSpec((tb, gw), lambda g, i: (tmap(i), g))],
                  out_specs=[pl.BlockSpec((tb, gwx), lambda g, i: (tmap(i), g)),
                             pl.BlockSpec((1, tb, hpg), lambda g, i: (g, tmap(i), 0)),
                             pl.BlockSpec((1, SSD_CONV, cw), lambda g, i: (g, 0, 0)),
                             pl.BlockSpec((1, 1, cw), lambda g, i: (g, 0, 0)),
                             row, row, row,
                             pl.BlockSpec((1, gw), lambda g, i: (0, g))],
                  out_shape=[jax.ShapeDtypeStruct((T, G * gwx), _MXU), jax.ShapeDtypeStruct((G, T, hpg), f32),
                             jax.ShapeDtypeStruct((G, SSD_CONV, cw), f32), jax.ShapeDtypeStruct((G, 1, cw), f32),
                             rshape, rshape, rshape, jax.ShapeDtypeStruct((1, G * gw), f32)],
                  scratch=[pltpu.VMEM((npair, pp, ns), f32), pltpu.VMEM((npair, pp, ns), f32),
                           pltpu.VMEM((tb, cw), f32), pltpu.VMEM((tb, cw), f32), pltpu.VMEM((_HALO, cw), f32)],
                  sem=("parallel", "arbitrary"))(
        proj, proj, dt_raw, conv_w, conv_b, dtb_row, dtb_col, a_row, a_col, d_row, norm_w, s_chunks, dy)


def _interleave(w, parts, blk):
    lead = w.shape[:-1]
    n = w.shape[-1] // (parts * blk)
    nd = len(lead)
    w = w.reshape(*lead, parts, n, blk)
    w = jnp.swapaxes(w, nd, nd + 1)
    return w.reshape(*lead, parts * n * blk)


def _deinterleave(w, parts, blk):
    lead = w.shape[:-1]
    n = w.shape[-1] // (parts * blk)
    nd = len(lead)
    w = w.reshape(*lead, n, parts, blk)
    w = jnp.swapaxes(w, nd, nd + 1)
    return w.reshape(*lead, parts * n * blk)


def _even_perm(w, lw, hd, tc):
    return jnp.concatenate([_interleave(w[..., :2 * lw], 2, tc), _interleave(w[..., 2 * lw:], 4, hd)], axis=-1)


def _even_unperm(w, lw, hd, tc):
    return jnp.concatenate([_deinterleave(w[..., :2 * lw], 2, tc), _deinterleave(w[..., 2 * lw:], 4, hd)], axis=-1)


def _ssd_split(w, inner, offset):
    G, N = SSD_GROUPS, SSD_STATE
    lead = w.shape[:-1]
    xs = w[..., offset:offset + inner].reshape(*lead, G, inner // G)
    bm = w[..., offset + inner:offset + inner + G * N].reshape(*lead, G, N)
    cm = w[..., offset + inner + G * N:offset + inner + 2 * G * N].reshape(*lead, G, N)
    return xs, bm, cm


def _ssd_in_perm(w, inner, heads, hp):
    G = SSD_GROUPS
    lead = w.shape[:-1]
    z = w[..., :inner].reshape(*lead, G, inner // G)
    xs, bm, cm = _ssd_split(w, inner, inner)
    main = jnp.concatenate([z, xs, bm, cm], axis=-1).reshape(*lead, -1)
    dt = w[..., w.shape[-1] - heads:]
    dt = jnp.pad(dt, [(0, 0)] * len(lead) + [(0, hp - heads)])
    return main, dt


def _ssd_in_unperm(main, dt, inner, heads):
    G, N = SSD_GROUPS, SSD_STATE
    gw = inner // G
    lead = main.shape[:-1]
    m = main.reshape(*lead, G, 2 * gw + 2 * N)
    parts = [m[..., :gw], m[..., gw:2 * gw], m[..., 2 * gw:2 * gw + N], m[..., 2 * gw + N:]]
    return jnp.concatenate([p.reshape(*lead, -1) for p in parts] + [dt[..., :heads]], axis=-1)


def _ssd_conv_perm(w, inner):
    return jnp.concatenate(_ssd_split(w, inner, 0), axis=-1)


def _ssd_conv_unperm(w, inner):
    G, N = SSD_GROUPS, SSD_STATE
    gw = inner // G
    lead = w.shape[:-2]
    parts = [w[..., :gw], w[..., gw:gw + N], w[..., gw + N:]]
    return jnp.concatenate([p.reshape(*lead, -1) for p in parts], axis=-1)


def _blockdiag(w, tc):
    nb, b, _ = w.shape
    per = tc // b
    eye = jnp.eye(per, dtype=w.dtype)
    w = w.reshape(nb // per, per, b, b)
    return jnp.einsum("jbik,bc->jbick", w, eye).reshape(nb // per, tc, tc)


def _blockdiag_extract(g, b):
    nj, tc, _ = g.shape
    per = tc // b
    g = g.reshape(nj, per, b, per, b)
    return jnp.einsum("jbick,bc->jbik", g, jnp.eye(per, dtype=g.dtype)).reshape(nj * per, b, b)


def _lower_bounds(hg_lower_bounds):
    lbp = jax.nn.softmax(hg_lower_bounds.astype(f32), axis=0)
    return jnp.cumsum(lbp, axis=0) - lbp[0]


def _local_step(x, target, w):
    T, D = x.shape
    depth = w["norm_mix_w"].shape[0]
    lw = w["lru_conv_b"].shape[1]
    hw = w["hg_lower_bounds"].shape[1]
    hd = w["hg_norm_w"].shape[1]
    nh = hw // hd
    heads = w["ssd_dt_bias"].shape[1]
    inner = w["ssd_norm_w"].shape[1]
    G, N = SSD_GROUPS, SSD_STATE
    hpg = heads // G
    gw = inner // G
    hp = -(-heads // _LANES) * _LANES
    F = w["ffn_w_down"].shape[1]
    tcl = _pick(lw, (_TC_LRU, 128))
    tcf = _pick(F, (_TC_FFN, 256, 128))
    even_in = 2 * lw + 4 * hw
    n_even, n_odd = w["ev_w_in"].shape[0], w["ssd_w_in"].shape[0]

    ev_w_in = _even_perm(w["ev_w_in"], lw, hd, tcl)
    ssd_main, ssd_dtw = _ssd_in_perm(w["ssd_w_in"], inner, heads, hp)
    ssd_cw = jnp.swapaxes(_ssd_conv_perm(w["ssd_conv_w"], inner), 1, 2)
    ssd_cb = _ssd_conv_perm(w["ssd_conv_b"], inner)[:, :, None, :]
    ffn_up = _interleave(w["ffn_w_up"], 2, tcf)
    ffn_cw = _interleave(w["ffn_conv_w"], 2, tcf)
    ffn_cb = _interleave(w["ffn_conv_b"], 2, tcf)
    lbs_fn = jax.vjp(_lower_bounds, w["hg_lower_bounds"])
    lbs = lbs_fn[0]
    a_neg = -jnp.exp(w["ssd_a_log"].astype(f32))

    def rowcol(v):
        return v.reshape(G, 1, hpg), v.reshape(G, hpg, 1)

    saved = []
    for l in range(depth):
        s = {"x": x}
        h = _rmsnorm_fwd(x, w["norm_mix_w"][l], f"norm_mix_fwd_{l}")
        s["h"] = h
        if l % 2 == 0:
            e = l // 2
            proj = _mm(h, ev_w_in[e], M=T, N=even_in, K=D, name=f"even_in_{l}")
            s["wr"] = _blockdiag(w["lru_w_r"][e], tcl).astype(_MXU)
            s["wi"] = _blockdiag(w["lru_w_i"][e], tcl).astype(_MXU)
            vecs = [w[k][e].reshape(1, lw) for k in ("lru_conv_b", "lru_b_r", "lru_b_i", "lru_lambda")]
            s["lru_args"] = (w["lru_conv_w"][e], vecs[0], s["wr"], vecs[1], s["wi"], vecs[2], vecs[3])
            ya, hst = _lru_fwd(proj, *s["lru_args"], lw, f"lru_fwd_{l}")
            s["hg_args"] = (lbs[e].reshape(1, hw), w["hg_norm_w"][e].reshape(1, hd))
            yb, stc = _hgrn_fwd(proj, *s["hg_args"], 2 * lw, nh, hd, f"hgrn_fwd_{l}")
            xm = _mm(ya, w["ev_w_out"][e], M=T, N=D, K=lw, resid=x, name=f"even_out_a_{l}")
            xm = _mm(yb, w["ev_w_out"][e], M=T, N=D, K=hw, b_off=(lw, 0), resid=xm, name=f"even_out_b_{l}")
            s.update(proj=proj, ya=ya, yb=yb, hst=hst, stc=stc)
        else:
            o = l // 2
            proj = _mm(h, ssd_main[o], M=T, N=ssd_main.shape[-1], K=D, name=f"ssd_in_{l}")
            dtr = _mm(h, ssd_dtw[o], M=T, N=hp, K=D, name=f"ssd_dt_{l}")
            dtb_r, dtb_c = rowcol(w["ssd_dt_bias"][o].astype(f32))
            a_r, a_c = rowcol(a_neg[o])
            d_r, _ = rowcol(w["ssd_d"][o].astype(f32))
            s["ssd_args"] = (ssd_cw[o], ssd_cb[o], dtb_r, dtb_c, a_r, a_c, d_r, w["ssd_norm_w"][o].reshape(1, inner))
            y, sc = _ssd_fwd(proj, dtr, *s["ssd_args"], hpg, f"ssd_fwd_{l}")
            xm = _mm(y, w["ssd_w_out"][o], M=T, N=D, K=inner, resid=x, name=f"ssd_out_{l}")
            s.update(proj=proj, dtr=dtr, y=y, sc=sc)
        hn = _rmsnorm_fwd(xm, w["norm_ffn_w"][l], f"norm_ffn_fwd_{l}")
        up = _mm(hn, ffn_up[l], M=T, N=2 * F, K=D, name=f"ffn_up_{l}")
        act = _ffn_mid_fwd(up, ffn_cw[l], ffn_cb[l].reshape(1, 2 * F), f"ffn_mid_fwd_{l}")
        x = _mm(act, w["ffn_w_down"][l], M=T, N=D, K=F, resid=xm, name=f"ffn_down_{l}")
        s.update(xm=xm, hn=hn, up=up, act=act)
        saved.append(s)

    loss, dx, dxb, dwf = _loss_head(x, w["norm_final_w"], target)
    g = {k: [None] * v.shape[0] for k, v in w.items() if k != "norm_final_w"}
    g["norm_final_w"] = dwf.reshape(D)

    for l in reversed(range(depth)):
        s = saved[l]
        g["ffn_w_down"][l] = _mm(s["act"], dxb, ta=True, M=F, N=D, K=T, name=f"ffn_down_dw_{l}")
        dact = _mm(dxb, w["ffn_w_down"][l], tb=True, M=T, N=F, K=D, name=f"ffn_down_dx_{l}")
        dup, dcw, dcb = _ffn_mid_bwd(s["up"], dact, ffn_cw[l], ffn_cb[l].reshape(1, 2 * F), f"ffn_mid_bwd_{l}")
        g["ffn_conv_w"][l] = _deinterleave(dcw, 2, tcf)
        g["ffn_conv_b"][l] = _deinterleave(dcb, 2, tcf).reshape(2 * F)
        g["ffn_w_up"][l] = _deinterleave(_mm(s["hn"], dup, ta=True, M=D, N=2 * F, K=T, name=f"ffn_up_dw_{l}"), 2, tcf)
        dhn = _mm(dup, ffn_up[l], tb=True, M=T, N=D, K=2 * F, name=f"ffn_up_dx_{l}")
        dxm, dxmb, dnf = _rmsnorm_bwd(dhn, s["xm"], w["norm_ffn_w"][l], dx, f"norm_ffn_bwd_{l}")
        g["norm_ffn_w"][l] = dnf.reshape(D)
        if l % 2 == 0:
            e = l // 2
            dy = _mm(dxmb, w["ev_w_out"][e], tb=True, M=T, N=lw + hw, K=D, name=f"even_out_dx_{l}")
            dwo_a = _mm(s["ya"], dxmb, ta=True, M=lw, N=D, K=T, name=f"even_out_a_dw_{l}")
            dwo_b = _mm(s["yb"], dxmb, ta=True, M=hw, N=D, K=T, name=f"even_out_b_dw_{l}")
            g["ev_w_out"][e] = jnp.concatenate([dwo_a, dwo_b], axis=0)
            dproj, dcw, dcb, dwr, dbr, dwi, dbi, dlam = _lru_bwd(s["proj"], s["hst"], dy, *s["lru_args"], lw,
                                                                 f"lru_bwd_{l}")
            dproj, dlb, dnw = _hgrn_bwd(s["proj"], s["stc"], dy, dproj, *s["hg_args"], 2 * lw, lw, nh, hd,
                                        f"hgrn_bwd_{l}")
            g["lru_conv_w"][e] = dcw
            g["lru_conv_b"][e] = dcb.reshape(lw)
            g["lru_w_r"][e] = _blockdiag_extract(dwr, LRU_BLOCK)
            g["lru_w_i"][e] = _blockdiag_extract(dwi, LRU_BLOCK)
            g["lru_b_r"][e] = dbr.reshape(lw)
            g["lru_b_i"][e] = dbi.reshape(lw)
            g["lru_lambda"][e] = dlam.reshape(lw)
            g["hg_lower_bounds"][e] = dlb.reshape(hw)
            g["hg_norm_w"][e] = jnp.sum(dnw, axis=(0, 1))
            g["ev_w_in"][e] = _even_unperm(_mm(s["h"], dproj, ta=True, M=D, N=even_in, K=T, name=f"even_in_dw_{l}"),
                                           lw, hd, tcl)
            dh = _mm(dproj, ev_w_in[e], tb=True, M=T, N=D, K=even_in, name=f"even_in_dx_{l}")
        else:
            o = l // 2
            dy = _mm(dxmb, w["ssd_w_out"][o], tb=True, M=T, N=inner, K=D, name=f"ssd_out_dx_{l}")
            g["ssd_w_out"][o] = _mm(s["y"], dxmb, ta=True, M=inner, N=D, K=T, name=f"ssd_out_dw_{l}")
            dproj, ddt, dcw, dcb, ddtb, da, dd, dnw = _ssd_bwd(s["proj"], s["dtr"], s["sc"], dy, *s["ssd_args"], hpg,
                                                               f"ssd_bwd_{l}")
            ddt = jnp.pad(jnp.swapaxes(ddt, 0, 1).reshape(T, heads), ((0, 0), (0, hp - heads)))
            g["ssd_conv_w"][o] = _ssd_conv_unperm(jnp.swapaxes(dcw, 0, 1), inner)
            g["ssd_conv_b"][o] = _ssd_conv_unperm(dcb[:, 0, :], inner)
            g["ssd_dt_bias"][o] = ddtb.reshape(heads)
            g["ssd_a_log"][o] = da.reshape(heads) * a_neg[o]
            g["ssd_d"][o] = dd.reshape(heads)
            g["ssd_norm_w"][o] = dnw.reshape(inner)
            dmain = _mm(s["h"], dproj, ta=True, M=D, N=dproj.shape[1], K=T, name=f"ssd_in_dw_{l}")
            ddtw = _mm(s["h"], ddt, ta=True, M=D, N=hp, K=T, name=f"ssd_dt_dw_{l}")
            g["ssd_w_in"][o] = _ssd_in_unperm(dmain, ddtw, inner, heads)
            dh = _mm(dproj, ssd_main[o], tb=True, M=T, N=D, K=dproj.shape[1], name=f"ssd_in_dx_{l}")
            dh = _mm(ddt, ssd_dtw[o], tb=True, M=T, N=D, K=hp, resid=dh, name=f"ssd_dt_dx_{l}")
        dx, dxb, dnm = _rmsnorm_bwd(dh, s["x"], w["norm_mix_w"][l], dxm, f"norm_mix_bwd_{l}")
        g["norm_mix_w"][l] = dnm.reshape(D)

    grads = {k: (jnp.stack(v) if isinstance(v, list) else v) for k, v in g.items()}
    grads["hg_lower_bounds"] = lbs_fn[1](grads["hg_lower_bounds"])[0]
    return loss[0, 0], dx, grads


def _best_div(n, cap, mult=8):
    best = mult
    for d in range(mult, min(n, cap) + 1, mult):
        if n % d == 0:
            best = d
    assert n % best == 0, (n, best)
    return best


def _adamw(wv, gv, mv, vv, name):
    R, C = wv.shape
    tr = _best_div(R, max(8, _SLAB_BLOCK_ELEMS // C))

    def body(w_ref, g_ref, m_ref, v_ref, d_ref, nm_ref, nv_ref):
        g = g_ref[...]
        m = ADAM_B1 * m_ref[...] + (1.0 - ADAM_B1) * g
        v = ADAM_B2 * v_ref[...] + (1.0 - ADAM_B2) * (g * g)
        m_hat = m / (1.0 - ADAM_B1 ** ADAM_STEP)
        v_hat = v / (1.0 - ADAM_B2 ** ADAM_STEP)
        d_ref[...] = -ADAM_LR * (m_hat / (jnp.sqrt(v_hat) + ADAM_EPS) + ADAM_WD * w_ref[...])
        nm_ref[...] = m
        nv_ref[...] = v

    blk = pl.BlockSpec((tr, C), lambda i: (i, 0))
    shp = jax.ShapeDtypeStruct((R, C), f32)
    return _pcall(body, name=name, grid=(R // tr,), in_specs=[blk] * 4, out_specs=[blk] * 3, out_shape=[shp] * 3,
                  sem=("parallel",))(wv, gv, mv, vv)


_ANY = pl.BlockSpec(memory_space=pl.ANY)
_MESH = pl.DeviceIdType.MESH


def _where_am_i():
    x, y, c = lax.axis_index("x"), lax.axis_index("y"), lax.axis_index("c")
    chips = [(1 - x, y), (x, 1 - y), (1 - x, 1 - y)]
    return x, y, c, chips


def _rcopy(src, dst, send_sems, recv_sems, j, dev):
    return pltpu.make_async_remote_copy(src_ref=src, dst_ref=dst, send_sem=send_sems.at[j], recv_sem=recv_sems.at[j],
                                        device_id=dev, device_id_type=_MESH)


def _comm_call(body, name, out_shape, n_sems, n_in):
    scratch = [pltpu.SemaphoreType.DMA((n_sems,)), pltpu.SemaphoreType.DMA((n_sems,))]
    return pl.pallas_call(body, name=name, out_shape=out_shape, in_specs=[_ANY] * n_in, out_specs=_ANY,
                          scratch_shapes=scratch)


def _allgather_chips(shard, name):
    R, C = shard.shape
    rh = R // 2
    assert R % 2 == 0

    def body(x_ref, out_ref, send_sems, recv_sems):
        x, y, c, chips = _where_am_i()
        k = 2 * x + y
        sib = (x, y, 1 - c)

        def blk(kk, half):
            return out_ref.at[kk, pl.ds(half * rh, rh)]

        first =[_rcopy(x_ref.at[pl.ds(c * rh, rh)], blk(k, c), send_sems, recv_sems, j, (cx, cy, c))
                 for j, (cx, cy) in enumerate(chips)]
        for cp in first:
            cp.start()
        passed = []
        for j, (cx, cy) in enumerate(chips):
            kk = 2 * cx + cy
            _rcopy(blk(kk, c), blk(kk, c), send_sems, recv_sems, j, (cx, cy, c)).wait_recv()
            fwd = _rcopy(blk(kk, c), blk(kk, c), send_sems, recv_sems, 3 + j, sib)
            fwd.start()
            passed.append(fwd)
        for j, (cx, cy) in enumerate(chips):
            kk = 2 * cx + cy
            _rcopy(blk(kk, 1 - c), blk(kk, 1 - c), send_sems, recv_sems, 3 + j, sib).wait_recv()
        for cp in first + passed:
            cp.wait_send()

    return _comm_call(body, name, jax.ShapeDtypeStruct((4, R, C), shard.dtype), 6, 1)(shard)


def _d2d_exchange(g, name):
    _, R, C = g.shape
    rh = R // 2

    def body(g_ref, a_ref, send_sems, recv_sems):
        x, y, c, _ = _where_am_i()
        cp = _rcopy(g_ref.at[:, pl.ds((1 - c) * rh, rh)], a_ref, send_sems, recv_sems, 0, (x, y, 1 - c))
        cp.start()
        cp.wait()

    return _comm_call(body, name, jax.ShapeDtypeStruct((4, rh, C), g.dtype), 1, 1)(g)


def _ici_scatter(p, name):
    _, rh, C = p.shape

    def body(p_ref, b_ref, send_sems, recv_sems):
        x, y, c, chips = _where_am_i()
        cps = [_rcopy(p_ref.at[2 * cx + cy], b_ref.at[j], send_sems, recv_sems, j, (cx, cy, c))
               for j, (cx, cy) in enumerate(chips)]
        for cp in cps:
            cp.start()
        for cp in cps:
            cp.wait()

    return _comm_call(body, name, jax.ShapeDtypeStruct((3, rh, C), p.dtype), 3, 1)(p)


def _d2d_swap(q, name):
    R, C = q.shape
    rh = R // 2

    def body(q_ref, o_ref, send_sems, recv_sems):
        del q_ref
        x, y, c, _ = _where_am_i()
        mine = o_ref.at[pl.ds(c * rh, rh)]
        cp = _rcopy(mine, mine, send_sems, recv_sems, 0, (x, y, 1 - c))
        cp.start()
        cp.wait()

    return pl.pallas_call(body, name=name, out_shape=jax.ShapeDtypeStruct((R, C), q.dtype), in_specs=[_ANY],
                          out_specs=_ANY, input_output_aliases={0: 0},
                          scratch_shapes=[pltpu.SemaphoreType.DMA((1,)), pltpu.SemaphoreType.DMA((1,))])(q)


def _add_halves(g, a, c_idx, name):
    _, R, C = g.shape
    rh = R // 2
    tr = _best_div(rh, max(16, _SLAB_BLOCK_ELEMS // C), 16)
    nb = rh // tr

    def body(c_ref, g_ref, a_ref, o_ref):
        del c_ref
        o_ref[...] = (g_ref[...] + a_ref[...]).astype(o_ref.dtype)

    return _pcall(body, name=name, grid=(4, nb), prefetch=1,
                  in_specs=[pl.BlockSpec((1, tr, C), lambda j, i, c: (j, c[0] * nb + i, 0)),
                            pl.BlockSpec((1, tr, C), lambda j, i, c: (j, i, 0))],
                  out_specs=pl.BlockSpec((1, tr, C), lambda j, i, c: (j, i, 0)),
                  out_shape=jax.ShapeDtypeStruct((4, rh, C), _WIRE), sem=("parallel", "parallel"))(c_idx, g, a)


def _add_chips(p, b, ck_idx, name):
    _, rh, C = p.shape
    tr = _best_div(rh, max(16, _SLAB_BLOCK_ELEMS // C), 16)
    nb = rh // tr

    def body(ck_ref, p_ref, b_ref, o_ref):
        del ck_ref
        o_ref[...] = ((p_ref[0].astype(f32) + b_ref[0].astype(f32)) + b_ref[1].astype(f32)) + b_ref[2].astype(f32)

    return _pcall(body, name=name, grid=(nb,), prefetch=1,
                  in_specs=[pl.BlockSpec((1, tr, C), lambda i, ck: (ck[1], i, 0)),
                            pl.BlockSpec((3, tr, C), lambda i, ck: (0, i, 0))],
                  out_specs=pl.BlockSpec((tr, C), lambda i, ck: (ck[0] * nb + i, 0)),
                  out_shape=jax.ShapeDtypeStruct((2 * rh, C), f32), sem=("parallel",))(ck_idx, p, b)


def _reduce_scatter(g, c_idx, ck_idx, name):
    a = _d2d_exchange(g, name + "_d2d")
    p = _add_halves(g, a, c_idx, name + "_add2")
    b = _ici_scatter(p, name + "_ici")
    q = _add_chips(p, b, ck_idx, name + "_add4")
    return _d2d_swap(q, name + "_swap")


def _allreduce_small(buf, name):
    R, C = buf.shape

    def body(x_ref, o_ref, gat_ref, send_sems, recv_sems):
        x, y, c, _ = _where_am_i()
        me = 4 * x + 2 * y + c
        gat_ref[me] = x_ref[...]
        cps = []
        for f in range(1, 8):
            fx, fy, fc = (f >> 2) & 1, (f >> 1) & 1, f & 1
            peer = ((1 - x) if fx else x, (1 - y) if fy else y, (1 - c) if fc else c)
            cps.append(_rcopy(x_ref, gat_ref.at[me], send_sems, recv_sems, f - 1, peer))
        for cp in cps:
            cp.start()
        for cp in cps:
            cp.wait()
        acc = gat_ref[0]
        for d in range(1, 8):
            acc = acc + gat_ref[d]
        o_ref[...] = acc

    vm = pl.BlockSpec(memory_space=pltpu.VMEM)
    return pl.pallas_call(body, name=name, out_shape=jax.ShapeDtypeStruct((R, C), f32), in_specs=[vm], out_specs=vm,
                          scratch_shapes=[pltpu.VMEM((8, R, C), f32), pltpu.SemaphoreType.DMA((7,)),
                                          pltpu.SemaphoreType.DMA((7,))],
                          compiler_params=pltpu.CompilerParams(vmem_limit_bytes=_VMEM_LIMIT))(buf)


def _pack(arrs, cols=_COMM_COLS):
    flat = jnp.concatenate([a.reshape(-1).astype(f32) for a in arrs])
    per = 8 * cols
    n = -(-flat.shape[0] // per) * per
    return jnp.pad(flat, (0, n - flat.shape[0])).reshape(n // cols, cols)


def _unpack(buf, shapes):
    flat = buf.reshape(-1)
    out, off = [], 0
    for s in shapes:
        n = math.prod(s)
        out.append(flat[off:off + n].reshape(s))
        off += n
    return out


_WEIGHTS = ['norm_mix_w', 'norm_ffn_w', 'norm_final_w', 'ev_w_in', 'lru_conv_w', 'lru_conv_b', 'lru_w_r', 'lru_b_r',
            'lru_w_i', 'lru_b_i', 'lru_lambda', 'hg_lower_bounds', 'hg_norm_w', 'ev_w_out', 'ssd_w_in', 'ssd_conv_w',
            'ssd_conv_b', 'ssd_dt_bias', 'ssd_a_log', 'ssd_d', 'ssd_norm_w', 'ssd_w_out', 'ffn_w_up', 'ffn_conv_w',
            'ffn_conv_b', 'ffn_w_down']
_BIG = {'ev_w_in': 2, 'ev_w_out': 1, 'ssd_w_in': 2, 'ssd_w_out': 1, 'ffn_w_up': 2, 'ffn_w_down': 1}
_SMALL_SHARDED = {'lru_conv_w': 2, 'ssd_conv_w': 2, 'ssd_conv_b': 1, 'ssd_norm_w': 1, 'ffn_conv_w': 2}


def _full_shape(shape, ax):
    return tuple(4 * d if i == ax else d for i, d in enumerate(shape))


def kernel(x, norm_mix_w, norm_ffn_w, norm_final_w, ev_w_in, lru_conv_w, lru_conv_b, lru_w_r, lru_b_r, lru_w_i, lru_b_i, lru_lambda, hg_lower_bounds, hg_norm_w, ev_w_out, ssd_w_in, ssd_conv_w, ssd_conv_b, ssd_dt_bias, ssd_a_log, ssd_d, ssd_norm_w, ssd_w_out, ffn_w_up, ffn_conv_w, ffn_conv_b, ffn_w_down, loss_target, m_norm_mix_w, m_norm_ffn_w, m_norm_final_w, m_ev_w_in, m_lru_conv_w, m_lru_conv_b, m_lru_w_r, m_lru_b_r, m_lru_w_i, m_lru_b_i, m_lru_lambda, m_hg_lower_bounds, m_hg_norm_w, m_ev_w_out, m_ssd_w_in, m_ssd_conv_w, m_ssd_conv_b, m_ssd_dt_bias, m_ssd_a_log, m_ssd_d, m_ssd_norm_w, m_ssd_w_out, m_ffn_w_up, m_ffn_conv_w, m_ffn_conv_b, m_ffn_w_down, v_norm_mix_w, v_norm_ffn_w, v_norm_final_w, v_ev_w_in, v_lru_conv_w, v_lru_conv_b, v_lru_w_r, v_lru_b_r, v_lru_w_i, v_lru_b_i, v_lru_lambda, v_hg_lower_bounds, v_hg_norm_w, v_ev_w_out, v_ssd_w_in, v_ssd_conv_w, v_ssd_conv_b, v_ssd_dt_bias, v_ssd_a_log, v_ssd_d, v_ssd_norm_w, v_ssd_w_out, v_ffn_w_up, v_ffn_conv_w, v_ffn_conv_b, v_ffn_w_down):
    given = dict(locals())
    w = {n: given[n] for n in _WEIGHTS}
    m = {n: given["m_" + n] for n in _WEIGHTS}
    v = {n: given["v_" + n] for n in _WEIGHTS}
    cx, cy, cc = lax.axis_index("x"), lax.axis_index("y"), lax.axis_index("c")
    chip = 2 * cx + cy
    c_idx = jnp.reshape(cc, (1,)).astype(jnp.int32)
    ck_idx = jnp.stack([cc, chip]).astype(jnp.int32)

    def rows2d(a):
        return a.reshape(-1, a.shape[-1])

    full = dict(w)
    for n, ax in _BIG.items():
        own = rows2d(w[n].astype(_MXU))
        gat = _allgather_chips(own, "gather_" + n)
        gat = lax.dynamic_update_index_in_dim(gat, own, chip, 0)
        blocks = gat.reshape((4,) + w[n].shape)
        full[n] = jnp.moveaxis(blocks, 0, ax).reshape(_full_shape(w[n].shape, ax))
    placed = []
    for n, ax in _SMALL_SHARDED.items():
        z = jnp.zeros(_full_shape(w[n].shape, ax), f32)
        z = lax.dynamic_update_slice_in_dim(z, w[n].astype(f32), chip * w[n].shape[ax], ax)
        placed.append(jnp.where(cc == 0, z, 0.0))
    small_full = _unpack(_allreduce_small(_pack(placed), "gather_small"), [p.shape for p in placed])
    for n, a in zip(_SMALL_SHARDED, small_full):
        full[n] = a

    loss_part, dx, grads = _local_step(x[0], loss_target[0], full)

    small = [n for n in _WEIGHTS if n not in _BIG]
    red = _allreduce_small(_pack([loss_part.reshape(1)] + [grads[n] for n in small]), "allreduce_small")
    red = _unpack(red, [(1,)] + [grads[n].shape for n in small])
    loss = red[0][0]
    gsum = {}
    for n, a in zip(small, red[1:]):
        if n in _SMALL_SHARDED:
            ax = _SMALL_SHARDED[n]
            a = lax.dynamic_slice_in_dim(a, chip * w[n].shape[ax], w[n].shape[ax], ax)
        gsum[n] = a
    outs = _adamw(_pack([w[n] for n in small]), _pack([gsum[n] for n in small]), _pack([m[n] for n in small]),
                  _pack([v[n] for n in small]), "adamw_small")
    shapes = [w[n].shape for n in small]
    delta, new_m, new_v = ({n: a for n, a in zip(small, _unpack(o, shapes))} for o in outs)

    for n, ax in _BIG.items():
        shp = w[n].shape
        g4 = jnp.moveaxis(grads[n].reshape(shp[:ax] + (4, shp[ax]) + shp[ax + 1:]), ax, 0)
        gs = _reduce_scatter(g4.reshape(4, -1, shp[-1]), c_idx, ck_idx, "rs_" + n)
        d_, m_, v_ = _adamw(rows2d(w[n]), gs, rows2d(m[n]), rows2d(v[n]), "adamw_" + n)
        gsum[n] = gs.reshape(shp)
        delta[n], new_m[n], new_v[n] = d_.reshape(shp), m_.reshape(shp), v_.reshape(shp)

    return (loss, dx[None], *[gsum[n] for n in _WEIGHTS], *[delta[n] for n in _WEIGHTS],
            *[new_m[n] for n in _WEIGHTS], *[new_v[n] for n in _WEIGHTS])
```

```python
import functools
import math

import jax
import jax.numpy as jnp
from jax import lax
from jax.experimental import pallas as pl
from jax.experimental.pallas import tpu as pltpu

f32 = jnp.float32
bf16 = jnp.bfloat16
_MXU = jnp.bfloat16
_WIRE = jnp.bfloat16
_HI = lax.Precision.HIGHEST

EPS = 1e-6
LRU_C = 8.0
LRU_BLOCK = 64
LRU_CONV = 4
HG_CHUNK = 64
SSD_GROUPS = 8
SSD_STATE = 128
SSD_CONV = 4
SSD_CHUNK = 64
FFN_CONV = 3
ADAM_LR, ADAM_B1, ADAM_B2, ADAM_EPS, ADAM_WD, ADAM_STEP = 0.001, 0.9, 0.999, 1e-08, 0.01, 10

_VMEM_LIMIT = 56 * 1024 * 1024
_LANES = 128
_HALO = 8

_TB_NORM = 512
_TB_FFN = 512
_TB_LRU = 256
_TB_HG = 512
_TB_SSD = 256
_TC_FFN = 512
_TC_LRU = 256
_HG_UNROLL = 2
_MM_TM, _MM_TN, _MM_TK = 1024, 1024, 1024
_MM_TK_ONE = 2048
_COMM_COLS = 1024
_SLAB_BLOCK_ELEMS = 384 * 1024


def _pick(n, prefs):
    for p in prefs:
        if p <= n and n % p == 0:
            return p
    return n


def _pcall(body, *, name, grid, in_specs, out_specs, out_shape, scratch=(), sem=None, aliases=None, prefetch=0):
    params = pltpu.CompilerParams(dimension_semantics=sem, vmem_limit_bytes=_VMEM_LIMIT)
    if prefetch:
        gs = pltpu.PrefetchScalarGridSpec(num_scalar_prefetch=prefetch, grid=grid, in_specs=in_specs,
                                          out_specs=out_specs, scratch_shapes=list(scratch))
        return pl.pallas_call(body, name=name, grid_spec=gs, out_shape=out_shape,
                              input_output_aliases=aliases or {}, compiler_params=params)
    return pl.pallas_call(body, name=name, grid=grid, in_specs=in_specs, out_specs=out_specs, out_shape=out_shape,
                          scratch_shapes=list(scratch), input_output_aliases=aliases or {}, compiler_params=params)


def _sig(x):
    return jax.nn.sigmoid(x)


def _softplus(x):
    return jnp.maximum(x, 0.0) + jnp.log(1.0 + jnp.exp(-jnp.abs(x)))


def _neg_expm1(x):
    series = -x * (1.0 + x * (0.5 + x * (1.0 / 6.0 + x * (1.0 / 24.0 + x * (1.0 / 120.0)))))
    return jnp.where(x > -0.05, series, 1.0 - jnp.exp(x))


_GELU_C = math.sqrt(2.0 / math.pi)


def _gelu_and_grad(x):
    inner = _GELU_C * (x + 0.044715 * x * x * x)
    t = jnp.tanh(inner)
    g = 0.5 * x * (1.0 + t)
    dg = 0.5 * (1.0 + t) + 0.5 * x * (1.0 - t * t) * _GELU_C * (1.0 + 3.0 * 0.044715 * x * x)
    return g, dg


def _silu_and_grad(x):
    s = _sig(x)
    return x * s, s * (1.0 + x * (1.0 - s))


def _dot(a, b, dims=(((1,), (0,)), ((), ()))):
    return lax.dot_general(a.astype(_MXU), b.astype(_MXU), dims, preferred_element_type=f32)


def _dot_nt(a, b):
    return _dot(a, b, (((1,), (1,)), ((), ())))


def _dot_tn(a, b):
    return _dot(a, b, (((0,), (0,)), ((), ())))


def _dot_hi(a, b):
    return jnp.dot(a, b, preferred_element_type=f32, precision=_HI)


def _tri(n, lower=True):
    r = lax.broadcasted_iota(jnp.int32, (n, n), 0)
    c = lax.broadcasted_iota(jnp.int32, (n, n), 1)
    return (r >= c) if lower else (r <= c)


def _shift_down(cur, prev, s):
    if s == 0:
        return cur
    n = cur.shape[0]
    ext = jnp.concatenate([prev, cur], axis=0)
    return pltpu.roll(ext, s, axis=0)[_HALO:_HALO + n]


def _shift_up(cur, nxt, s):
    if s == 0:
        return cur
    n = cur.shape[0]
    ext = jnp.concatenate([cur, nxt], axis=0)
    return pltpu.roll(ext, n + _HALO - s, axis=0)[0:n]


def _conv_fwd(cur, prev, w, b, k):
    y = b + w[k - 1] * cur
    for j in range(k - 1):
        y = y + w[j] * _shift_down(cur, prev, k - 1 - j)
    return y


def _conv_bwd(dy, nxt_dy, cur, prev, w, k):
    del prev
    dx = w[k - 1] * dy
    dws = []
    for j in range(k - 1):
        up = _shift_up(dy, nxt_dy, k - 1 - j)
        dx = dx + w[j] * up
        dws.append(jnp.sum(up * cur, axis=0, keepdims=True))
    dws.append(jnp.sum(dy * cur, axis=0, keepdims=True))
    return dx, dws, jnp.sum(dy, axis=0, keepdims=True)


def _rows(ref, k, cols=None):
    if cols is None:
        return [ref[j:j + 1, :] for j in range(k)]
    return [ref[j:j + 1, cols[0]:cols[1]] for j in range(k)]


def _add_rows(ref, rows, cols=None):
    for j, r in enumerate(rows):
        if cols is None:
            ref[j:j + 1, :] += r
        else:
            ref[j:j + 1, cols[0]:cols[1]] += r


def _mm(a, b, *, M, N, K, ta=False, tb=False, a_off=(0, 0), b_off=(0, 0), resid=None, out_dtype=f32, name,
        b_lead=None, tn=None, out_perm=None):
    tm = _pick(M, (_MM_TM, 512, 256, 128))
    tn = tn or _pick(N, (_MM_TN, 512, 256, 128))
    narrow = a.dtype.itemsize <= 2 and b.dtype.itemsize <= 2
    tk = _pick(K, ((_MM_TK_ONE, _MM_TK, 512, 256, 128) if narrow else (_MM_TK // 2, 256, 128)))
    nk = K // tk
    if ta:
        assert a_off[0] % tk == 0 and a_off[1] % tm == 0
        a_spec = pl.BlockSpec((tk, tm), lambda i, j, k: (k + a_off[0] // tk, i + a_off[1] // tm))
        a_dim = 0
    else:
        assert a_off[0] % tm == 0 and a_off[1] % tk == 0
        a_spec = pl.BlockSpec((tm, tk), lambda i, j, k: (i + a_off[0] // tm, k + a_off[1] // tk))
        a_dim = 1
    lead = () if b_lead is None else (None,)
    pre = (lambda t: t) if b_lead is None else (lambda t: (b_lead,) + t)
    if tb:
        assert b_off[0] % tn == 0 and b_off[1] % tk == 0
        b_spec = pl.BlockSpec(lead + (tn, tk), lambda i, j, k: pre((j + b_off[0] // tn, k + b_off[1] // tk)))
        b_dim = 1
    else:
        assert b_off[0] % tk == 0 and b_off[1] % tn == 0
        b_spec = pl.BlockSpec(lead + (tk, tn), lambda i, j, k: pre((k + b_off[0] // tk, j + b_off[1] // tn)))
        b_dim = 0
    dims = (((a_dim,), (b_dim,)), ((), ()))
    out_col = (lambda j: j) if out_perm is None else out_perm
    has_resid = resid is not None

    def body_one(*refs):
        a_ref, b_ref = refs[0], refs[1]
        o_ref = refs[-1]
        r = _dot(a_ref[...], b_ref[...], dims)
        if has_resid:
            r = r + refs[2][...].astype(f32)
        o_ref[...] = r.astype(out_dtype)

    def body_acc(*refs):
        a_ref, b_ref = refs[0], refs[1]
        o_ref, acc_ref = refs[-2], refs[-1]
        kk = pl.program_id(2)

        @pl.when(kk == 0)
        def _():
            acc_ref[...] = _dot(a_ref[...], b_ref[...], dims)

        @pl.when(kk > 0)
        def _():
            acc_ref[...] += _dot(a_ref[...], b_ref[...], dims)

        @pl.when(kk == nk - 1)
        def _():
            r = acc_ref[...]
            if has_resid:
                r = r + refs[2][...].astype(f32)
            o_ref[...] = r.astype(out_dtype)

    in_specs = [a_spec, b_spec]
    args = [a, b]
    if has_resid:
        in_specs.append(pl.BlockSpec((tm, tn), lambda i, j, k: (i, j)))
        args.append(resid)
    return _pcall(body_one if nk == 1 else body_acc, name=name, grid=(M // tm, N // tn, nk), in_specs=in_specs,
                  out_specs=pl.BlockSpec((tm, tn), lambda i, j, k: (i, out_col(j))),
                  out_shape=jax.ShapeDtypeStruct((M, N), out_dtype),
                  scratch=[] if nk == 1 else [pltpu.VMEM((tm, tn), f32)],
                  sem=("parallel", "parallel", "arbitrary"))(*args)


def _rmsnorm_fwd(x, w, name):
    T, D = x.shape
    tb = _pick(T, (_TB_NORM, 256, 128, 64))

    def body(x_ref, w_ref, o_ref):
        xv = x_ref[...]
        rstd = lax.rsqrt(jnp.mean(xv * xv, axis=-1, keepdims=True) + EPS)
        o_ref[...] = (xv * rstd * w_ref[...]).astype(o_ref.dtype)

    return _pcall(body, name=name, grid=(T // tb,),
                  in_specs=[pl.BlockSpec((tb, D), lambda i: (i, 0)), pl.BlockSpec((1, D), lambda i: (0, 0))],
                  out_specs=pl.BlockSpec((tb, D), lambda i: (i, 0)),
                  out_shape=jax.ShapeDtypeStruct((T, D), _MXU), sem=("parallel",))(x, w.reshape(1, D))


def _rmsnorm_bwd(dh, x, w, dres, name):
    T, D = x.shape
    tb = _pick(T, (_TB_NORM, 256, 128, 64))

    def body(dh_ref, x_ref, w_ref, dres_ref, dx_ref, dxm_ref, dw_ref):
        xv = x_ref[...]
        rstd = lax.rsqrt(jnp.mean(xv * xv, axis=-1, keepdims=True) + EPS)
        xhat = xv * rstd
        dhv = dh_ref[...]
        g = dhv * w_ref[...]
        dx = dres_ref[...] + rstd * (g - xhat * jnp.mean(g * xhat, axis=-1, keepdims=True))
        dx_ref[...] = dx
        dxm_ref[...] = dx.astype(dxm_ref.dtype)

        @pl.when(pl.program_id(0) == 0)
        def _():
            dw_ref[...] = jnp.zeros_like(dw_ref)

        dw_ref[...] += jnp.sum(dhv * xhat, axis=0, keepdims=True)

    row = pl.BlockSpec((tb, D), lambda i: (i, 0))
    vec = pl.BlockSpec((1, D), lambda i: (0, 0))
    return _pcall(body, name=name, grid=(T // tb,), in_specs=[row, row, vec, row], out_specs=[row, row, vec],
                  out_shape=[jax.ShapeDtypeStruct((T, D), f32), jax.ShapeDtypeStruct((T, D), _MXU),
                             jax.ShapeDtypeStruct((1, D), f32)],
                  sem=("arbitrary",))(dh, x, w.reshape(1, D), dres)


def _loss_head(x, w, target):
    T, D = x.shape
    tb = _pick(T, (_TB_NORM, 256, 128, 64))

    def body(x_ref, w_ref, t_ref, loss_ref, dx_ref, dxm_ref, dw_ref):
        xv = x_ref[...]
        rstd = lax.rsqrt(jnp.mean(xv * xv, axis=-1, keepdims=True) + EPS)
        xhat = xv * rstd
        err = xhat * w_ref[...] - t_ref[...]
        dy = err * (1.0 / D)
        g = dy * w_ref[...]
        dx = rstd * (g - xhat * jnp.mean(g * xhat, axis=-1, keepdims=True))
        dx_ref[...] = dx
        dxm_ref[...] = dx.astype(dxm_ref.dtype)

        @pl.when(pl.program_id(0) == 0)
        def _():
            dw_ref[...] = jnp.zeros_like(dw_ref)
            loss_ref[...] = jnp.zeros_like(loss_ref)

        dw_ref[...] += jnp.sum(dy * xhat, axis=0, keepdims=True)
        part = 0.5 * jnp.sum(jnp.sum(err * err, axis=-1, keepdims=True) * (1.0 / D), axis=0, keepdims=True)
        loss_ref[...] += jnp.broadcast_to(part, loss_ref.shape)

    row = pl.BlockSpec((tb, D), lambda i: (i, 0))
    vec = pl.BlockSpec((1, D), lambda i: (0, 0))
    return _pcall(body, name="loss_head", grid=(T // tb,), in_specs=[row, vec, row],
                  out_specs=[pl.BlockSpec((1, _LANES), lambda i: (0, 0)), row, row, vec],
                  out_shape=[jax.ShapeDtypeStruct((1, _LANES), f32), jax.ShapeDtypeStruct((T, D), f32),
                             jax.ShapeDtypeStruct((T, D), _MXU), jax.ShapeDtypeStruct((1, D), f32)],
                  sem=("arbitrary",))(x, w.reshape(1, D), target)


def _halo_spec(tb, width, col_fn):
    r = tb // _HALO
    return pl.BlockSpec((_HALO, width), lambda j, i: (jnp.maximum(i * r - 1, 0), col_fn(j)))


def _ffn_mid_fwd(up, conv_w, conv_b, name):
    T, F2 = up.shape
    tc = _pick(F2 // 2, (_TC_FFN, 256, 128))
    tb = _pick(T, (_TB_FFN, 256, 128, 64))
    nj = F2 // (2 * tc)

    def body(cur_ref, prev_ref, w_ref, b_ref, o_ref):
        cur = cur_ref[...]
        prev = jnp.where(pl.program_id(1) > 0, prev_ref[...], 0.0)
        u = _conv_fwd(cur, prev, _rows(w_ref, FFN_CONV), b_ref[...], FFN_CONV)
        gate, val = u[:, :tc], u[:, tc:]
        o_ref[...] = (gate * _sig(gate) * val).astype(o_ref.dtype)

    return _pcall(body, name=name, grid=(nj, T // tb),
                  in_specs=[pl.BlockSpec((tb, 2 * tc), lambda j, i: (i, j)), _halo_spec(tb, 2 * tc, lambda j: j),
                            pl.BlockSpec((FFN_CONV, 2 * tc), lambda j, i: (0, j)),
                            pl.BlockSpec((1, 2 * tc), lambda j, i: (0, j))],
                  out_specs=pl.BlockSpec((tb, tc), lambda j, i: (i, j)),
                  out_shape=jax.ShapeDtypeStruct((T, F2 // 2), _MXU), sem=("parallel", "parallel"))(
        up, up, conv_w, conv_b)


def _ffn_mid_bwd(up, dact, conv_w, conv_b, name):
    T, F2 = up.shape
    tc = _pick(F2 // 2, (_TC_FFN, 256, 128))
    tb = _pick(T, (_TB_FFN, 256, 128, 64))
    nj, nt = F2 // (2 * tc), T // tb
    r = tb // _HALO

    def body(cur_ref, prev_ref, da_ref, w_ref, b_ref, dup_ref, dw_ref, db_ref, carry_ref):
        i = pl.program_id(1)
        ti = nt - 1 - i
        cur = cur_ref[...]
        prev = jnp.where(ti > 0, prev_ref[...], 0.0)
        w = _rows(w_ref, FFN_CONV)
        u = _conv_fwd(cur, prev, w, b_ref[...], FFN_CONV)
        gate, val = u[:, :tc], u[:, tc:]
        da = da_ref[...]
        sil, dsil = _silu_and_grad(gate)
        du = jnp.concatenate([da * val * dsil, da * sil], axis=1)

        @pl.when(i == 0)
        def _():
            carry_ref[...] = jnp.zeros_like(carry_ref)
            dw_ref[...] = jnp.zeros_like(dw_ref)
            db_ref[...] = jnp.zeros_like(db_ref)

        dx, dw, db = _conv_bwd(du, carry_ref[...], cur, prev, w, FFN_CONV)
        dup_ref[...] = dx.astype(dup_ref.dtype)
        _add_rows(dw_ref, dw)
        db_ref[...] += db
        carry_ref[...] = du[0:_HALO]

    return _pcall(body, name=name, grid=(nj, nt),
                  in_specs=[pl.BlockSpec((tb, 2 * tc), lambda j, i: (nt - 1 - i, j)),
                            pl.BlockSpec((_HALO, 2 * tc), lambda j, i: (jnp.maximum((nt - 1 - i) * r - 1, 0), j)),
                            pl.BlockSpec((tb, tc), lambda j, i: (nt - 1 - i, j)),
                            pl.BlockSpec((FFN_CONV, 2 * tc), lambda j, i: (0, j)),
                            pl.BlockSpec((1, 2 * tc), lambda j, i: (0, j))],
                  out_specs=[pl.BlockSpec((tb, 2 * tc), lambda j, i: (nt - 1 - i, j)),
                             pl.BlockSpec((FFN_CONV, 2 * tc), lambda j, i: (0, j)),
                             pl.BlockSpec((1, 2 * tc), lambda j, i: (0, j))],
                  out_shape=[jax.ShapeDtypeStruct((T, F2), _MXU), jax.ShapeDtypeStruct((FFN_CONV, F2), f32),
                             jax.ShapeDtypeStruct((1, F2), f32)],
                  scratch=[pltpu.VMEM((_HALO, 2 * tc), f32)], sem=("parallel", "arbitrary"))(
        up, up, dact, conv_w, conv_b)


def _scan_fwd(a, u):
    n = a.shape[0]
    row = lax.broadcasted_iota(jnp.int32, a.shape, 0)
    d = 1
    while d < n:
        a_s = jnp.where(row >= d, pltpu.roll(a, d, axis=0), 1.0)
        u_s = jnp.where(row >= d, pltpu.roll(u, d, axis=0), 0.0)
        u = u + a * u_s
        a = a * a_s
        d *= 2
    return a, u


def _scan_bwd(m, v):
    n = m.shape[0]
    row = lax.broadcasted_iota(jnp.int32, m.shape, 0)
    d = 1
    while d < n:
        ok = row + d < n
        m_s = jnp.where(ok, pltpu.roll(m, n - d, axis=0), 1.0)
        v_s = jnp.where(ok, pltpu.roll(v, n - d, axis=0), 0.0)
        v = v + m * v_s
        m = m * m_s
        d *= 2
    return v


def _lru_gates(xc, wr, br, wi, bi, lam):
    r = _sig(_dot(xc, wr) + br)
    gi = _sig(_dot(xc, wi) + bi)
    sp = _softplus(-lam)
    log_a = -LRU_C * r * sp
    a = jnp.exp(log_a)
    mult = jnp.sqrt(_neg_expm1(2.0 * log_a))
    return r, gi, sp, a, mult


def _lru_fwd(proj, conv_w, conv_b, wr, br, wi, bi, lam, lw, name):
    T = proj.shape[0]
    tc = _pick(lw, (_TC_LRU, 128))
    tb = _pick(T, (_TB_LRU, 128, 64))
    nj = lw // tc

    def body(cur_ref, prev_ref, cw_ref, cb_ref, wr_ref, br_ref, wi_ref, bi_ref, lam_ref, ya_ref, h_ref, carry_ref):
        i = pl.program_id(1)

        @pl.when(i == 0)
        def _():
            carry_ref[...] = jnp.zeros_like(carry_ref)

        xa = cur_ref[:, :tc]
        ga = cur_ref[:, tc:]
        prev = jnp.where(i > 0, prev_ref[:, :tc], 0.0)
        xc = _conv_fwd(xa, prev, _rows(cw_ref, LRU_CONV), cb_ref[...], LRU_CONV)
        r, gi, sp, a, mult = _lru_gates(xc, wr_ref[0], br_ref[...], wi_ref[0], bi_ref[...], lam_ref[...])
        acum, h = _scan_fwd(a, mult * gi * xc)
        h = h + acum * carry_ref[0:1, :]
        h_ref[...] = h
        gel, _ = _gelu_and_grad(ga)
        ya_ref[...] = (h * gel).astype(ya_ref.dtype)
        carry_ref[...] = pltpu.roll(h, 1, axis=0)[0:_HALO]

    vec = pl.BlockSpec((1, tc), lambda j, i: (0, j))
    mat = pl.BlockSpec((1, tc, tc), lambda j, i: (j, 0, 0))
    out = pl.BlockSpec((tb, tc), lambda j, i: (i, j))
    return _pcall(body, name=name, grid=(nj, T // tb),
                  in_specs=[pl.BlockSpec((tb, 2 * tc), lambda j, i: (i, j)), _halo_spec(tb, 2 * tc, lambda j: j),
                            pl.BlockSpec((LRU_CONV, tc), lambda j, i: (0, j)), vec, mat, vec, mat, vec, vec],
                  out_specs=[out, out],
                  out_shape=[jax.ShapeDtypeStruct((T, lw), _MXU), jax.ShapeDtypeStruct((T, lw), f32)],
                  scratch=[pltpu.VMEM((_HALO, tc), f32)], sem=("parallel", "arbitrary"))(
        proj, proj, conv_w, conv_b, wr, br, wi, bi, lam)


def _lru_bwd(proj, h, dy, conv_w, conv_b, wr, br, wi, bi, lam, lw, name):
    T, width = proj.shape
    tc = _pick(lw, (_TC_LRU, 128))
    tb = _pick(T, (_TB_LRU, 128, 64))
    nj, nt = lw // tc, T // tb
    rr = tb // _HALO

    def body(cur_ref, prev_ref, h_ref, hp_ref, dy_ref, cw_ref, cb_ref, wr_ref, br_ref, wi_ref, bi_ref, lam_ref,
             dp_ref, dcw_ref, dcb_ref, dwr_ref, dbr_ref, dwi_ref, dbi_ref, dlam_ref, cl_ref, cx_ref):
        i = pl.program_id(1)
        ti = nt - 1 - i

        @pl.when(i == 0)
        def _():
            for ref in (cl_ref, cx_ref, dcw_ref, dcb_ref, dwr_ref, dbr_ref, dwi_ref, dbi_ref, dlam_ref):
                ref[...] = jnp.zeros_like(ref)

        xa = cur_ref[:, :tc]
        ga = cur_ref[:, tc:]
        prev = jnp.where(ti > 0, prev_ref[:, :tc], 0.0)
        cw = _rows(cw_ref, LRU_CONV)
        xc = _conv_fwd(xa, prev, cw, cb_ref[...], LRU_CONV)
        wrv, wiv, lam_v = wr_ref[0], wi_ref[0], lam_ref[...]
        r, gi, sp, a, mult = _lru_gates(xc, wrv, br_ref[...], wiv, bi_ref[...], lam_v)
        hcur = h_ref[...]
        hprev = _shift_down(hcur, jnp.where(ti > 0, hp_ref[...], 0.0), 1)
        gel, dgel = _gelu_and_grad(ga)
        dya = dy_ref[...]
        dga = dya * hcur * dgel
        row = lax.broadcasted_iota(jnp.int32, a.shape, 0)
        v = dya * gel + jnp.where(row == tb - 1, cl_ref[0:1, :], 0.0)
        m = jnp.where(row < tb - 1, pltpu.roll(a, tb - 1, axis=0), 0.0)
        lamb = _scan_bwd(m, v)
        cl_ref[...] = (a * lamb)[0:_HALO]
        da = lamb * hprev
        dmult = lamb * gi * xc
        dgi = lamb * mult * xc
        dxc = lamb * mult * gi
        dlog_a = da * a - dmult * (a * a) / mult
        dr = dlog_a * (-LRU_C * sp)
        dlam_ref[...] += jnp.sum(dlog_a * (-LRU_C) * r, axis=0, keepdims=True) * (-_sig(-lam_v))
        dpr = dr * r * (1.0 - r)
        dpi = dgi * gi * (1.0 - gi)
        dbr_ref[...] += jnp.sum(dpr, axis=0, keepdims=True)
        dbi_ref[...] += jnp.sum(dpi, axis=0, keepdims=True)
        dwr_ref[0] += _dot_tn(xc, dpr)
        dwi_ref[0] += _dot_tn(xc, dpi)
        dxc = dxc + _dot_nt(dpr, wrv) + _dot_nt(dpi, wiv)
        dxa, dws, db = _conv_bwd(dxc, cx_ref[...], xa, prev, cw, LRU_CONV)
        cx_ref[...] = dxc[0:_HALO]
        _add_rows(dcw_ref, dws)
        dcb_ref[...] += db
        dp_ref[...] = jnp.concatenate([dxa, dga], axis=1).astype(dp_ref.dtype)

    vec = pl.BlockSpec((1, tc), lambda j, i: (0, j))
    mat = pl.BlockSpec((1, tc, tc), lambda j, i: (j, 0, 0))
    cwspec = pl.BlockSpec((LRU_CONV, tc), lambda j, i: (0, j))
    blk = pl.BlockSpec((tb, tc), lambda j, i: (nt - 1 - i, j))
    vshape = jax.ShapeDtypeStruct((1, lw), f32)
    mshape = jax.ShapeDtypeStruct((nj, tc, tc), f32)
    return _pcall(body, name=name, grid=(nj, nt),
                  in_specs=[pl.BlockSpec((tb, 2 * tc), lambda j, i: (nt - 1 - i, j)),
                            pl.BlockSpec((_HALO, 2 * tc), lambda j, i: (jnp.maximum((nt - 1 - i) * rr - 1, 0), j)),
                            blk,
                            pl.BlockSpec((_HALO, tc), lambda j, i: (jnp.maximum((nt - 1 - i) * rr - 1, 0), j)),
                            blk, cwspec, vec, mat, vec, mat, vec, vec],
                  out_specs=[pl.BlockSpec((tb, 2 * tc), lambda j, i: (nt - 1 - i, j)), cwspec, vec, mat, vec, mat, vec,
                             vec],
                  out_shape=[jax.ShapeDtypeStruct((T, width), _MXU), jax.ShapeDtypeStruct((LRU_CONV, lw), f32),
                             vshape, mshape, vshape, mshape, vshape, vshape],
                  scratch=[pltpu.VMEM((_HALO, tc), f32), pltpu.VMEM((_HALO, tc), f32)],
                  sem=("parallel", "arbitrary"))(
        proj, proj, h, h, dy, conv_w, conv_b, wr, br, wi, bi, lam)


def _hg_chunk_fwd(blk, lb, hd):
    q, fr, v, gt = blk[:, 0:hd], blk[:, hd:2 * hd], blk[:, 2 * hd:3 * hd], blk[:, 3 * hd:4 * hd]
    sg = _sig(fr)
    f = lb + (1.0 - lb) * sg
    k = (1.0 - lb) * (1.0 - sg)
    g = jnp.log(f)
    n = HG_CHUNK
    cum = _dot_hi(_tri(n).astype(f32), g)
    row = lax.broadcasted_iota(jnp.int32, g.shape, 0)
    tot = jnp.sum(g, axis=0, keepdims=True)
    mid = jnp.sum(jnp.where(row < n // 2, g, 0.0), axis=0, keepdims=True)
    sq = _sig(q)
    qs = q * sq * (hd ** -0.5)
    e_qd, e_kd, e_qc, e_kt = jnp.exp(cum - mid), jnp.exp(mid - cum), jnp.exp(cum), jnp.exp(tot - cum)
    qd, kd, qc, kt = qs * e_qd, k * e_kd, qs * e_qc, k * e_kt
    p = jnp.where(_tri(n), _dot_nt(qd, kd), 0.0)
    return dict(q=q, v=v, gt=gt, sg=sg, f=f, k=k, sq=sq, qs=qs, tot=tot, e_qd=e_qd, e_kd=e_kd, e_qc=e_qc,
                e_kt=e_kt, qd=qd, kd=kd, qc=qc, kt=kt, p=p)


def _hgrn_fwd(proj, lb, nw, col0, nh, hd, name):
    T = proj.shape[0]
    tb = _pick(T, (_TB_HG, 256, 128, 64))
    ncb = tb // HG_CHUNK
    cb0 = col0 // (4 * hd)

    def body(p_ref, lb_ref, nw_ref, y_ref, sc_ref, st_ref):
        @pl.when(pl.program_id(1) == 0)
        def _():
            st_ref[...] = jnp.zeros_like(st_ref)

        lbv, nwv = lb_ref[...], nw_ref[...]

        def chunk(c, carry):
            r0 = pl.multiple_of(c * HG_CHUNK, HG_CHUNK)
            d = _hg_chunk_fwd(p_ref[pl.ds(r0, HG_CHUNK), :], lbv, hd)
            st = st_ref[...]
            sc_ref[c, 0] = st
            o = _dot(d["p"], d["v"]) + _dot_nt(d["qc"], st)
            st_ref[...] = st * jnp.exp(d["tot"]) + _dot_tn(d["v"], d["kt"])
            rstd = lax.rsqrt(jnp.mean(o * o, axis=-1, keepdims=True) + EPS)
            sil, _ = _silu_and_grad(d["gt"])
            y_ref[pl.ds(r0, HG_CHUNK), :] = (o * rstd * nwv * sil).astype(y_ref.dtype)
            return carry

        lax.fori_loop(0, ncb, chunk, 0, unroll=_HG_UNROLL if ncb % _HG_UNROLL == 0 else 1)

    return _pcall(body, name=name, grid=(nh, T // tb),
                  in_specs=[pl.BlockSpec((tb, 4 * hd), lambda h, i: (i, cb0 + h)),
                            pl.BlockSpec((1, hd), lambda h, i: (0, h)), pl.BlockSpec((1, hd), lambda h, i: (0, 0))],
                  out_specs=[pl.BlockSpec((tb, hd), lambda h, i: (i, h)),
                             pl.BlockSpec((ncb, 1, hd, hd), lambda h, i: (i, h, 0, 0))],
                  out_shape=[jax.ShapeDtypeStruct((T, nh * hd), _MXU),
                             jax.ShapeDtypeStruct((T // HG_CHUNK, nh, hd, hd), f32)],
                  scratch=[pltpu.VMEM((hd, hd), f32)], sem=("parallel", "arbitrary"))(proj, lb, nw)


def _hgrn_bwd(proj, st_chunks, dy, dproj, lb, nw, col0, dycol0, nh, hd, name):
    T = proj.shape[0]
    tb = _pick(T, (_TB_HG, 256, 128, 64))
    ncb = tb // HG_CHUNK
    nt = T // tb
    cb0 = col0 // (4 * hd)
    dyb0 = dycol0 // hd
    n = HG_CHUNK

    def body(p_ref, sc_ref, dy_ref, dpin_ref, lb_ref, nw_ref, dp_ref, dlb_ref, dnw_ref, dst_ref):
        del dpin_ref

        @pl.when(pl.program_id(1) == 0)
        def _():
            dst_ref[...] = jnp.zeros_like(dst_ref)
            dlb_ref[...] = jnp.zeros_like(dlb_ref)
            dnw_ref[...] = jnp.zeros_like(dnw_ref)

        lbv, nwv = lb_ref[...], nw_ref[...]
        tril = _tri(n)

        def chunk(cc, carry):
            c = ncb - 1 - cc
            r0 = pl.multiple_of(c * HG_CHUNK, HG_CHUNK)
            d = _hg_chunk_fwd(p_ref[pl.ds(r0, HG_CHUNK), :], lbv, hd)
            st = sc_ref[c, 0]
            v, gt = d["v"], d["gt"]
            o = _dot(d["p"], v) + _dot_nt(d["qc"], st)
            rstd = lax.rsqrt(jnp.mean(o * o, axis=-1, keepdims=True) + EPS)
            sil, dsil = _silu_and_grad(gt)
            dyv = dy_ref[pl.ds(r0, HG_CHUNK), :]
            xhat = o * rstd
            don = dyv * sil
            dgt = dyv * xhat * nwv * dsil
            dnw_ref[0] += jnp.sum(don * xhat, axis=0, keepdims=True)
            g2 = don * nwv
            do = rstd * (g2 - xhat * jnp.mean(g2 * xhat, axis=-1, keepdims=True))
            dstn = dst_ref[...]
            dp = jnp.where(tril, _dot_nt(do, v), 0.0)
            dv = _dot_tn(d["p"], do) + _dot_nt(d["kt"], dstn)
            dqd = _dot(dp, d["kd"])
            dkd = _dot_tn(dp, d["qd"])
            dqc = _dot(do, st)
            dkt = _dot(v, dstn)
            e_tot = jnp.exp(d["tot"])
            dst_ref[...] = dstn * e_tot + _dot_tn(do, d["qc"])
            dtot = jnp.sum(e_tot * st * dstn, axis=0, keepdims=True) + jnp.sum(dkt * d["kt"], axis=0, keepdims=True)
            dqs = dqd * d["e_qd"] + dqc * d["e_qc"]
            dk = dkd * d["e_kd"] + dkt * d["e_kt"]
            dcum = dqd * d["qd"] - dkd * d["kd"] + dqc * d["qc"] - dkt * d["kt"]
            dg = _dot_hi(_tri(n, lower=False).astype(f32), dcum) + dtot
            df = dg / d["f"] - dk
            sg = d["sg"]
            dfr = df * (1.0 - lbv) * sg * (1.0 - sg)
            dlb_ref[...] += jnp.sum(df * (1.0 - sg), axis=0, keepdims=True)
            sq, q = d["sq"], d["q"]
            dq = dqs * (hd ** -0.5) * sq * (1.0 + q * (1.0 - sq))
            dp_ref[pl.ds(r0, HG_CHUNK), :] = jnp.concatenate([dq, dfr, dv, dgt], axis=1).astype(dp_ref.dtype)
            return carry

        lax.fori_loop(0, ncb, chunk, 0, unroll=_HG_UNROLL if ncb % _HG_UNROLL == 0 else 1)

    width = dproj.shape[1]
    return _pcall(body, name=name, grid=(nh, nt),
                  in_specs=[pl.BlockSpec((tb, 4 * hd), lambda h, i: (nt - 1 - i, cb0 + h)),
                            pl.BlockSpec((ncb, 1, hd, hd), lambda h, i: (nt - 1 - i, h, 0, 0)),
                            pl.BlockSpec((tb, hd), lambda h, i: (nt - 1 - i, dyb0 + h)),
                            pl.BlockSpec(memory_space=pl.ANY),
                            pl.BlockSpec((1, hd), lambda h, i: (0, h)), pl.BlockSpec((1, hd), lambda h, i: (0, 0))],
                  out_specs=[pl.BlockSpec((tb, 4 * hd), lambda h, i: (nt - 1 - i, cb0 + h)),
                             pl.BlockSpec((1, hd), lambda h, i: (0, h)),
                             pl.BlockSpec((1, 1, hd), lambda h, i: (h, 0, 0))],
                  out_shape=[jax.ShapeDtypeStruct((T, width), _MXU), jax.ShapeDtypeStruct((1, nh * hd), f32),
                             jax.ShapeDtypeStruct((nh, 1, hd), f32)],
                  scratch=[pltpu.VMEM((hd, hd), f32)], sem=("parallel", "arbitrary"), aliases={3: 0})(
        proj, st_chunks, dy, dproj, lb, nw)


def _ssd_dt(dtraw_c, g, hpg, dtb_row, dtb_col, a_row, a_col):
    hp = dtraw_c.shape[1]
    sel = (lax.broadcasted_iota(jnp.int32, (hp, hpg), 0) == g * hpg + lax.broadcasted_iota(jnp.int32, (hp, hpg), 1))
    selt = (lax.broadcasted_iota(jnp.int32, (hpg, hp), 1) == g * hpg + lax.broadcasted_iota(jnp.int32, (hpg, hp), 0))
    pre = _dot_hi(dtraw_c, sel.astype(f32)) + dtb_row
    pre_t = lax.dot_general(selt.astype(f32), dtraw_c, (((1,), (1,)), ((), ())), preferred_element_type=f32,
                            precision=_HI) + dtb_col
    dt = _softplus(pre)
    dt_t = _softplus(pre_t)
    return pre, dt, dt * a_row, dt_t * a_col


def _ssd_chunk_fwd(xbc, z, dtraw_c, g, prm, s_ref, hpg, gw):
    n, ns, pp = SSD_CHUNK, SSD_STATE, 2 * (gw // hpg)
    dtb_row, dtb_col, a_row, a_col, d_row = prm
    bm, cm = xbc[:, gw:gw + ns], xbc[:, gw + ns:gw + 2 * ns]
    pre, dt, a, a_t = _ssd_dt(dtraw_c, g, hpg, dtb_row, dtb_col, a_row, a_col)
    cum = _dot_hi(_tri(n).astype(f32), a)
    cum_t = _dot_hi(a_t, _tri(n, lower=False).astype(f32))
    tot = jnp.sum(a, axis=0, keepdims=True)
    cb = _dot_nt(cm, bm)
    tril = _tri(n)
    lo = lax.broadcasted_iota(jnp.int32, (n, pp), 1) < pp // 2
    sub_lo = lax.broadcasted_iota(jnp.int32, (pp, ns), 0) < pp // 2

    def per_lane(m, h0):
        return jnp.where(lo[0:m.shape[0]], m[:, h0:h0 + 1], m[:, h0 + 1:h0 + 2])

    pairs = []
    for p in range(hpg // 2):
        h0 = 2 * p
        xp = xbc[:, pp * p:pp * (p + 1)]
        dtl = per_lane(dt, h0)
        xdt = xp * dtl
        lm, mm = [], []
        for h in (h0, h0 + 1):
            l_h = jnp.where(tril, jnp.exp(jnp.minimum(cum[:, h:h + 1] - cum_t[h:h + 1, :], 0.0)), 0.0)
            lm.append(l_h)
            mm.append(cb * l_h)
        y_diag = _dot(mm[0], jnp.where(lo, xdt, 0.0)) + _dot(mm[1], jnp.where(lo, 0.0, xdt))
        sp = s_ref[p]
        e = jnp.exp(per_lane(cum, h0))
        y_off = _dot_nt(cm, sp) * e
        fd = jnp.exp(per_lane(tot - cum, h0))
        u = xdt * fd
        etot = jnp.exp(tot)
        etot_col = jnp.where(sub_lo, etot[:, h0:h0 + 1], etot[:, h0 + 1:h0 + 2])
        dl = per_lane(d_row, h0)
        y = y_diag + y_off + xp * dl
        zp = z[:, pp * p:pp * (p + 1)]
        sil, dsil = _silu_and_grad(zp)
        pairs.append(dict(xp=xp, dtl=dtl, xdt=xdt, lm=lm, mm=mm, y_diag=y_diag, sp=sp, e=e, y_off=y_off, fd=fd, u=u,
                          etot_col=etot_col, dl=dl, y=y, sil=sil, dsil=dsil, y2=y * sil))
    ms = sum(jnp.sum(q["y2"] * q["y2"], axis=-1, keepdims=True) for q in pairs) * (1.0 / gw)
    rstd = lax.rsqrt(ms + EPS)
    return dict(bm=bm, cm=cm, pre=pre, dt=dt, a=a, cum=cum, cum_t=cum_t, tot=tot, cb=cb, lo=lo, sub_lo=sub_lo,
                pairs=pairs, rstd=rstd)


def _ssd_specs(tb, gwx, hpg, hp, cw, gw, tmap):
    r = tb // _HALO
    row = pl.BlockSpec((1, 1, hpg), lambda g, i: (g, 0, 0))
    col = pl.BlockSpec((1, hpg, 1), lambda g, i: (g, 0, 0))
    return [pl.BlockSpec((tb, gwx), lambda g, i: (tmap(i), g)),
            pl.BlockSpec((_HALO, gwx), lambda g, i: (jnp.maximum(tmap(i) * r - 1, 0), g)),
            pl.BlockSpec((tb, hp), lambda g, i: (tmap(i), 0)),
            pl.BlockSpec((1, SSD_CONV, cw), lambda g, i: (g, 0, 0)),
            pl.BlockSpec((1, 1, cw), lambda g, i: (g, 0, 0)),
            row, col, row, col, row,
            pl.BlockSpec((1, gw), lambda g, i: (0, g))]


def _ssd_fwd(proj, dt_raw, conv_w, conv_b, dtb_row, dtb_col, a_row, a_col, d_row, norm_w, hpg, name):
    T = proj.shape[0]
    G = SSD_GROUPS
    gwx = proj.shape[1] // G
    gw = (gwx - 2 * SSD_STATE) // 2
    cw = gw + 2 * SSD_STATE
    hp = dt_raw.shape[1]
    tb = _pick(T, (_TB_SSD, 128, 64))
    ncb = tb // SSD_CHUNK
    npair = hpg // 2
    pp = 2 * (gw // hpg)
    assert pp == _LANES and hpg % 2 == 0

    def body(p_ref, prev_ref, dt_ref, cw_ref, cb_ref, dtbr_ref, dtbc_ref, ar_ref, ac_ref, dr_ref, nw_ref,
             y_ref, sc_ref, s_ref, xbc_ref):
        g = pl.program_id(0)
        i = pl.program_id(1)

        @pl.when(i == 0)
        def _():
            s_ref[...] = jnp.zeros_like(s_ref)

        raw = p_ref[:, gw:]
        prev = jnp.where(i > 0, prev_ref[:, gw:], 0.0)
        pre = _conv_fwd(raw, prev, [cw_ref[0, j:j + 1, :] for j in range(SSD_CONV)], cb_ref[0], SSD_CONV)
        xbc_ref[...] = pre * _sig(pre)
        prm = (dtbr_ref[0], dtbc_ref[0], ar_ref[0], ac_ref[0], dr_ref[0])
        nwv = nw_ref[...]

        def chunk(c, carry):
            r0 = pl.multiple_of(c * SSD_CHUNK, SSD_CHUNK)
            rows = pl.ds(r0, SSD_CHUNK)
            sc_ref[c, 0] = s_ref[...]
            d = _ssd_chunk_fwd(xbc_ref[rows, :], p_ref[rows, 0:gw], dt_ref[rows, :], g, prm, s_ref, hpg, gw)
            for p, q in enumerate(d["pairs"]):
                s_ref[p] = q["sp"] * q["etot_col"] + _dot_tn(q["u"], d["bm"])
                y_ref[rows, pp * p:pp * (p + 1)] = (q["y2"] * d["rstd"] * nwv[:, pp * p:pp * (p + 1)]).astype(
                    y_ref.dtype)
            return carry

        lax.fori_loop(0, ncb, chunk, 0)

    return _pcall(body, name=name, grid=(G, T // tb),
                  in_specs=_ssd_specs(tb, gwx, hpg, hp, cw, gw, lambda i: i),
                  out_specs=[pl.BlockSpec((tb, gw), lambda g, i: (i, g)),
                             pl.BlockSpec((ncb, 1, npair, pp, SSD_STATE), lambda g, i: (i, g, 0, 0, 0))],
                  out_shape=[jax.ShapeDtypeStruct((T, G * gw), _MXU),
                             jax.ShapeDtypeStruct((T // SSD_CHUNK, G, npair, pp, SSD_STATE), f32)],
                  scratch=[pltpu.VMEM((npair, pp, SSD_STATE), f32), pltpu.VMEM((tb, cw), f32)],
                  sem=("parallel", "arbitrary"))(
        proj, proj, dt_raw, conv_w, conv_b, dtb_row, dtb_col, a_row, a_col, d_row, norm_w)


def _ssd_bwd(proj, dt_raw, s_chunks, dy, conv_w, conv_b, dtb_row, dtb_col, a_row, a_col, d_row, norm_w, hpg, name):
    T = proj.shape[0]
    G = SSD_GROUPS
    gwx = proj.shape[1] // G
    gw = (gwx - 2 * SSD_STATE) // 2
    ns = SSD_STATE
    cw = gw + 2 * ns
    hp = dt_raw.shape[1]
    tb = _pick(T, (_TB_SSD, 128, 64))
    ncb, nt = tb // SSD_CHUNK, T // tb
    npair = hpg // 2
    pp = 2 * (gw // hpg)
    n = SSD_CHUNK

    def body(p_ref, prev_ref, dt_ref, cw_ref, cb_ref, dtbr_ref, dtbc_ref, ar_ref, ac_ref, dr_ref, nw_ref,
             sc_ref, dy_ref,
             dp_ref, ddt_ref, dcw_ref, dcb_ref, ddtb_ref, da_ref, dd_ref, dnw_ref,
             ds_ref, s_ref, xbc_ref, dxbc_ref, carry_ref):
        g = pl.program_id(0)
        i = pl.program_id(1)
        ti = nt - 1 - i

        @pl.when(i == 0)
        def _():
            for ref in (ds_ref, carry_ref, dcw_ref, dcb_ref, ddtb_ref, da_ref, dd_ref, dnw_ref):
                ref[...] = jnp.zeros_like(ref)

        raw = p_ref[:, gw:]
        prev = jnp.where(ti > 0, prev_ref[:, gw:], 0.0)
        cwr = [cw_ref[0, j:j + 1, :] for j in range(SSD_CONV)]
        pre = _conv_fwd(raw, prev, cwr, cb_ref[0], SSD_CONV)
        xbc_ref[...] = pre * _sig(pre)
        prm = (dtbr_ref[0], dtbc_ref[0], ar_ref[0], ac_ref[0], dr_ref[0])
        a_row = ar_ref[0]
        nwv = nw_ref[...]
        tril = _tri(n)
        lane8 = lax.broadcasted_iota(jnp.int32, (n, hpg), 1)

        def chunk(cc, carry):
            c = ncb - 1 - cc
            r0 = pl.multiple_of(c * SSD_CHUNK, SSD_CHUNK)
            rows = pl.ds(r0, SSD_CHUNK)
            s_ref[...] = sc_ref[c, 0]
            xbc = xbc_ref[rows, :]
            d = _ssd_chunk_fwd(xbc, p_ref[rows, 0:gw], dt_ref[rows, :], g, prm, s_ref, hpg, gw)
            bm, cm, lo, sub_lo, rstd = d["bm"], d["cm"], d["lo"], d["sub_lo"], d["rstd"]
            dyo = dy_ref[rows, :]
            s1 = sum(jnp.sum(dyo[:, pp * p:pp * (p + 1)] * nwv[:, pp * p:pp * (p + 1)] * q["y2"], axis=-1,
                             keepdims=True) for p, q in enumerate(d["pairs"]))
            dcb = jnp.zeros((n, n), f32)
            cb_t = _dot_nt(bm, cm)
            triu = _tri(n, lower=False)
            dbm = jnp.zeros((n, ns), f32)
            dcm = jnp.zeros((n, ns), f32)
            dcum = jnp.zeros((n, hpg), f32)
            ddt_dir = jnp.zeros((n, hpg), f32)
            dtot = jnp.zeros((1, hpg), f32)
            ddv = jnp.zeros((1, hpg), f32)
            dxs = []
            for p, q in enumerate(d["pairs"]):
                h0 = 2 * p
                sl = slice(pp * p, pp * (p + 1))
                dyp = dyo[:, sl]
                nwp = nwv[:, sl]
                dnw_ref[:, sl] += jnp.sum(dyp * q["y2"] * rstd, axis=0, keepdims=True)
                dy2 = rstd * (dyp * nwp - q["y2"] * (rstd * rstd) * s1 * (1.0 / gw))
                dyv = dy2 * q["sil"]
                dz = dy2 * q["y"] * q["dsil"]
                dp_ref[rows, sl] = dz.astype(dp_ref.dtype)
                xp, xdt, mm, lm = q["xp"], q["xdt"], q["mm"], q["lm"]
                dyx = dyv * xp
                dx = dyv * q["dl"]
                dxdt_diag = jnp.where(lo, _dot_tn(mm[0], dyv), _dot_tn(mm[1], dyv))
                dy_lo, dy_hi = jnp.where(lo, dyv, 0.0), jnp.where(lo, 0.0, dyv)
                dm0 = jnp.where(tril, _dot_nt(dy_lo, xdt), 0.0)
                dm1 = jnp.where(tril, _dot_nt(dy_hi, xdt), 0.0)
                dcb = dcb + dm0 * lm[0] + dm1 * lm[1]
                seg = []
                for hh, (dm, dyh) in enumerate(((dm0, dy_lo), (dm1, dy_hi))):
                    h = h0 + hh
                    l_t = jnp.where(triu, jnp.exp(jnp.minimum(d["cum_t"][h:h + 1, :] - d["cum"][:, h:h + 1], 0.0)), 0.0)
                    dm_t = _dot_nt(xdt, dyh)
                    seg.append(jnp.sum(dm * mm[hh], axis=1, keepdims=True)
                               - jnp.sum(dm_t * (cb_t * l_t), axis=1, keepdims=True))
                dsn = ds_ref[p]
                dzz = dyv * q["e"]
                dcm = dcm + _dot(dzz, q["sp"])
                ds_ref[p] = dsn * q["etot_col"] + _dot_tn(dzz, cm)
                du = _dot_nt(bm, dsn)
                dbm = dbm + _dot(q["u"], dsn)
                dxdt = dxdt_diag + du * q["fd"]
                wv = du * q["u"]
                cterm = dyv * q["y_off"] - wv
                ddir = dxdt * xp
                sdec = q["etot_col"] * q["sp"] * dsn

                def halves(x):
                    lo_s = jnp.sum(jnp.where(lo, x, 0.0), axis=1, keepdims=True)
                    return lo_s, jnp.sum(x, axis=1, keepdims=True) - lo_s

                c_lo, c_hi = halves(cterm)
                d_lo, d_hi = halves(ddir)
                w_lo, w_hi = halves(wv)
                x_lo, x_hi = halves(dyx)
                s_all = jnp.sum(sdec, axis=1, keepdims=True)
                s_lo = jnp.sum(jnp.where(sub_lo[:, 0:1], s_all, 0.0), axis=0, keepdims=True)
                s_hi = jnp.sum(s_all, axis=0, keepdims=True) - s_lo
                for h, cv, dv_, wv_, xv_, sv_ in ((h0, c_lo + seg[0], d_lo, w_lo, x_lo, s_lo),
                                                  (h0 + 1, c_hi + seg[1], d_hi, w_hi, x_hi, s_hi)):
                    sel = lane8 == h
                    dcum = dcum + jnp.where(sel, cv, 0.0)
                    ddt_dir = ddt_dir + jnp.where(sel, dv_, 0.0)
                    dtot = dtot + jnp.where(sel[0:1], jnp.sum(wv_, axis=0, keepdims=True) + sv_, 0.0)
                    ddv = ddv + jnp.where(sel[0:1], jnp.sum(xv_, axis=0, keepdims=True), 0.0)
                dxs.append(dx + dxdt * q["dtl"])
            da_c = _dot_hi(_tri(n, lower=False).astype(f32), dcum) + dtot
            ddt = da_c * a_row + ddt_dir
            da_ref[0] += jnp.sum(da_c * d["dt"], axis=0, keepdims=True)
            dd_ref[0] += ddv
            ddt_raw = ddt * _sig(d["pre"])
            ddtb_ref[0] += jnp.sum(ddt_raw, axis=0, keepdims=True)
            ddt_ref[0, rows, :] = ddt_raw
            dcm = dcm + _dot(dcb, bm)
            dbm = dbm + _dot_tn(dcb, cm)
            dxbc_ref[rows, :] = jnp.concatenate(dxs + [dbm, dcm], axis=1)
            return carry

        lax.fori_loop(0, ncb, chunk, 0)
        _, dsil = _silu_and_grad(pre)
        dpre = dxbc_ref[...] * dsil
        draw, dws, db = _conv_bwd(dpre, carry_ref[...], raw, prev, cwr, SSD_CONV)
        carry_ref[...] = dpre[0:_HALO]
        for j, r_ in enumerate(dws):
            dcw_ref[0, j:j + 1, :] += r_
        dcb_ref[0] += db
        dp_ref[:, gw:] = draw.astype(dp_ref.dtype)

    tmap = lambda i: nt - 1 - i
    row = pl.BlockSpec((1, 1, hpg), lambda g, i: (g, 0, 0))
    rshape = jax.ShapeDtypeStruct((G, 1, hpg), f32)
    return _pcall(body, name=name, grid=(G, nt),
                  in_specs=_ssd_specs(tb, gwx, hpg, hp, cw, gw, tmap) + [
                      pl.BlockSpec((ncb, 1, npair, pp, ns), lambda g, i: (tmap(i), g, 0, 0, 0)),
                      pl.BlockSpec((tb, gw), lambda g, i: (tmap(i), g))],
                  out_specs=[pl.BlockSpec((tb, gwx), lambda g, i: (tmap(i), g)),
                             pl.BlockSpec((1, tb, hpg), lambda g, i: (g, tmap(i), 0)),
                             pl.BlockSpec((1, SSD_CONV, cw), lambda g, i: (g, 0, 0)),
                             pl.BlockSpec((1, 1, cw), lambda g, i: (g, 0, 0)),
                             row, row, row,
                             pl.BlockSpec((1, gw), lambda g, i: (0, g))],
                  out_shape=[jax.ShapeDtypeStruct((T, G * gwx), _MXU), jax.ShapeDtypeStruct((G, T, hpg), f32),
                             jax.ShapeDtypeStruct((G, SSD_CONV, cw), f32), jax.ShapeDtypeStruct((G, 1, cw), f32),
                             rshape, rshape, rshape, jax.ShapeDtypeStruct((1, G * gw), f32)],
                  scratch=[pltpu.VMEM((npair, pp, ns), f32), pltpu.VMEM((npair, pp, ns), f32),
                           pltpu.VMEM((tb, cw), f32), pltpu.VMEM((tb, cw), f32), pltpu.VMEM((_HALO, cw), f32)],
                  sem=("parallel", "arbitrary"))(
        proj, proj, dt_raw, conv_w, conv_b, dtb_row, dtb_col, a_row, a_col, d_row, norm_w, s_chunks, dy)


def _interleave(w, parts, blk):
    lead = w.shape[:-1]
    n = w.shape[-1] // (parts * blk)
    nd = len(lead)
    w = w.reshape(*lead, parts, n, blk)
    w = jnp.swapaxes(w, nd, nd + 1)
    return w.reshape(*lead, parts * n * blk)


def _deinterleave(w, parts, blk):
    lead = w.shape[:-1]
    n = w.shape[-1] // (parts * blk)
    nd = len(lead)
    w = w.reshape(*lead, n, parts, blk)
    w = jnp.swapaxes(w, nd, nd + 1)
    return w.reshape(*lead, parts * n * blk)


def _even_perm(w, lw, hd, tc):
    return jnp.concatenate([_interleave(w[..., :2 * lw], 2, tc), _interleave(w[..., 2 * lw:], 4, hd)], axis=-1)


def _even_unperm(w, lw, hd, tc):
    return jnp.concatenate([_deinterleave(w[..., :2 * lw], 2, tc), _deinterleave(w[..., 2 * lw:], 4, hd)], axis=-1)


def _ssd_split(w, inner, offset):
    G, N = SSD_GROUPS, SSD_STATE
    lead = w.shape[:-1]
    xs = w[..., offset:offset + inner].reshape(*lead, G, inner // G)
    bm = w[..., offset + inner:offset + inner + G * N].reshape(*lead, G, N)
    cm = w[..., offset + inner + G * N:offset + inner + 2 * G * N].reshape(*lead, G, N)
    return xs, bm, cm


def _ssd_in_perm(w, inner, heads, hp):
    G = SSD_GROUPS
    lead = w.shape[:-1]
    z = w[..., :inner].reshape(*lead, G, inner // G)
    xs, bm, cm = _ssd_split(w, inner, inner)
    main = jnp.concatenate([z, xs, bm, cm], axis=-1).reshape(*lead, -1)
    dt = w[..., w.shape[-1] - heads:]
    dt = jnp.pad(dt, [(0, 0)] * len(lead) + [(0, hp - heads)])
    return main, dt


def _ssd_in_unperm(main, dt, inner, heads):
    G, N = SSD_GROUPS, SSD_STATE
    gw = inner // G
    lead = main.shape[:-1]
    m = main.reshape(*lead, G, 2 * gw + 2 * N)
    parts = [m[..., :gw], m[..., gw:2 * gw], m[..., 2 * gw:2 * gw + N], m[..., 2 * gw + N:]]
    return jnp.concatenate([p.reshape(*lead, -1) for p in parts] + [dt[..., :heads]], axis=-1)


def _ssd_conv_perm(w, inner):
    return jnp.concatenate(_ssd_split(w, inner, 0), axis=-1)


def _ssd_conv_unperm(w, inner):
    G, N = SSD_GROUPS, SSD_STATE
    gw = inner // G
    lead = w.shape[:-2]
    parts = [w[..., :gw], w[..., gw:gw + N], w[..., gw + N:]]
    return jnp.concatenate([p.reshape(*lead, -1) for p in parts], axis=-1)


def _blockdiag(w, tc):
    nb, b, _ = w.shape
    per = tc // b
    eye = jnp.eye(per, dtype=w.dtype)
    w = w.reshape(nb // per, per, b, b)
    return jnp.einsum("jbik,bc->jbick", w, eye).reshape(nb // per, tc, tc)


def _blockdiag_extract(g, b):
    nj, tc, _ = g.shape
    per = tc // b
    g = g.reshape(nj, per, b, per, b)
    return jnp.einsum("jbick,bc->jbik", g, jnp.eye(per, dtype=g.dtype)).reshape(nj * per, b, b)


def _lower_bounds(hg_lower_bounds):
    lbp = jax.nn.softmax(hg_lower_bounds.astype(f32), axis=0)
    return jnp.cumsum(lbp, axis=0) - lbp[0]


def _local_step(x, target, w):
    T, D = x.shape
    depth = w["norm_mix_w"].shape[0]
    lw = w["lru_conv_b"].shape[1]
    hw = w["hg_lower_bounds"].shape[1]
    hd = w["hg_norm_w"].shape[1]
    nh = hw // hd
    heads = w["ssd_dt_bias"].shape[1]
    inner = w["ssd_norm_w"].shape[1]
    G, N = SSD_GROUPS, SSD_STATE
    hpg = heads // G
    gw = inner // G
    hp = -(-heads // _LANES) * _LANES
    F = w["ffn_w_down"].shape[1]
    tcl = _pick(lw, (_TC_LRU, 128))
    tcf = _pick(F, (_TC_FFN, 256, 128))
    even_in = 2 * lw + 4 * hw
    n_even, n_odd = w["ev_w_in"].shape[0], w["ssd_w_in"].shape[0]

    ev_w_in = _even_perm(w["ev_w_in"], lw, hd, tcl)
    ssd_main, ssd_dtw = _ssd_in_perm(w["ssd_w_in"], inner, heads, hp)
    ssd_cw = jnp.swapaxes(_ssd_conv_perm(w["ssd_conv_w"], inner), 1, 2)
    ssd_cb = _ssd_conv_perm(w["ssd_conv_b"], inner)[:, :, None, :]
    ffn_up = _interleave(w["ffn_w_up"], 2, tcf)
    ffn_cw = _interleave(w["ffn_conv_w"], 2, tcf)
    ffn_cb = _interleave(w["ffn_conv_b"], 2, tcf)
    lbs_fn = jax.vjp(_lower_bounds, w["hg_lower_bounds"])
    lbs = lbs_fn[0]
    a_neg = -jnp.exp(w["ssd_a_log"].astype(f32))

    def rowcol(v):
        return v.reshape(G, 1, hpg), v.reshape(G, hpg, 1)

    saved = []
    for l in range(depth):
        s = {"x": x}
        h = _rmsnorm_fwd(x, w["norm_mix_w"][l], f"norm_mix_fwd_{l}")
        s["h"] = h
        if l % 2 == 0:
            e = l // 2
            proj = _mm(h, ev_w_in, b_lead=e, M=T, N=even_in, K=D, name=f"even_in_{l}")
            s["wr"] = _blockdiag(w["lru_w_r"][e], tcl).astype(_MXU)
            s["wi"] = _blockdiag(w["lru_w_i"][e], tcl).astype(_MXU)
            vecs = [w[k][e].reshape(1, lw) for k in ("lru_conv_b", "lru_b_r", "lru_b_i", "lru_lambda")]
            s["lru_args"] = (w["lru_conv_w"][e], vecs[0], s["wr"], vecs[1], s["wi"], vecs[2], vecs[3])
            ya, hst = _lru_fwd(proj, *s["lru_args"], lw, f"lru_fwd_{l}")
            s["hg_args"] = (lbs[e].reshape(1, hw), w["hg_norm_w"][e].reshape(1, hd))
            yb, stc = _hgrn_fwd(proj, *s["hg_args"], 2 * lw, nh, hd, f"hgrn_fwd_{l}")
            xm = _mm(ya, w["ev_w_out"], b_lead=e, M=T, N=D, K=lw, resid=x, name=f"even_out_a_{l}")
            xm = _mm(yb, w["ev_w_out"], b_lead=e, M=T, N=D, K=hw, b_off=(lw, 0), resid=xm, name=f"even_out_b_{l}")
            s.update(proj=proj, ya=ya, yb=yb, hst=hst, stc=stc)
        else:
            o = l // 2
            proj = _mm(h, ssd_main, b_lead=o, M=T, N=ssd_main.shape[-1], K=D, name=f"ssd_in_{l}")
            dtr = _mm(h, ssd_dtw, b_lead=o, M=T, N=hp, K=D, name=f"ssd_dt_{l}")
            dtb_r, dtb_c = rowcol(w["ssd_dt_bias"][o].astype(f32))
            a_r, a_c = rowcol(a_neg[o])
            d_r, _ = rowcol(w["ssd_d"][o].astype(f32))
            s["ssd_args"] = (ssd_cw[o], ssd_cb[o], dtb_r, dtb_c, a_r, a_c, d_r, w["ssd_norm_w"][o].reshape(1, inner))
            y, sc = _ssd_fwd(proj, dtr, *s["ssd_args"], hpg, f"ssd_fwd_{l}")
            xm = _mm(y, w["ssd_w_out"], b_lead=o, M=T, N=D, K=inner, resid=x, name=f"ssd_out_{l}")
            s.update(proj=proj, dtr=dtr, y=y, sc=sc)
        hn = _rmsnorm_fwd(xm, w["norm_ffn_w"][l], f"norm_ffn_fwd_{l}")
        up = _mm(hn, ffn_up, b_lead=l, M=T, N=2 * F, K=D, name=f"ffn_up_{l}")
        act = _ffn_mid_fwd(up, ffn_cw[l], ffn_cb[l].reshape(1, 2 * F), f"ffn_mid_fwd_{l}")
        x = _mm(act, w["ffn_w_down"], b_lead=l, M=T, N=D, K=F, resid=xm, name=f"ffn_down_{l}")
        s.update(xm=xm, hn=hn, up=up, act=act)
        saved.append(s)

    loss, dx, dxb, dwf = _loss_head(x, w["norm_final_w"], target)
    g = {k: [None] * v.shape[0] for k, v in w.items() if k != "norm_final_w"}
    g["norm_final_w"] = dwf.reshape(D)

    for l in reversed(range(depth)):
        s = saved[l]
        g["ffn_w_down"][l] = _mm(s["act"], dxb, ta=True, M=F, N=D, K=T, name=f"ffn_down_dw_{l}")
        dact = _mm(dxb, w["ffn_w_down"], b_lead=l, tb=True, M=T, N=F, K=D, name=f"ffn_down_dx_{l}")
        dup, dcw, dcb = _ffn_mid_bwd(s["up"], dact, ffn_cw[l], ffn_cb[l].reshape(1, 2 * F), f"ffn_mid_bwd_{l}")
        g["ffn_conv_w"][l] = _deinterleave(dcw, 2, tcf)
        g["ffn_conv_b"][l] = _deinterleave(dcb, 2, tcf).reshape(2 * F)
        g["ffn_w_up"][l] = _mm(s["hn"], dup, ta=True, M=D, N=2 * F, K=T, name=f"ffn_up_dw_{l}", tn=tcf,
                               out_perm=lambda j: lax.rem(j, 2) * (F // tcf) + lax.div(j, 2))
        dhn = _mm(dup, ffn_up, b_lead=l, tb=True, M=T, N=D, K=2 * F, name=f"ffn_up_dx_{l}")
        dxm, dxmb, dnf = _rmsnorm_bwd(dhn, s["xm"], w["norm_ffn_w"][l], dx, f"norm_ffn_bwd_{l}")
        g["norm_ffn_w"][l] = dnf.reshape(D)
        if l % 2 == 0:
            e = l // 2
            dy = _mm(dxmb, w["ev_w_out"], b_lead=e, tb=True, M=T, N=lw + hw, K=D, name=f"even_out_dx_{l}")
            dwo_a = _mm(s["ya"], dxmb, ta=True, M=lw, N=D, K=T, name=f"even_out_a_dw_{l}")
            dwo_b = _mm(s["yb"], dxmb, ta=True, M=hw, N=D, K=T, name=f"even_out_b_dw_{l}")
            g["ev_w_out"][e] = jnp.concatenate([dwo_a, dwo_b], axis=0)
            dproj, dcw, dcb, dwr, dbr, dwi, dbi, dlam = _lru_bwd(s["proj"], s["hst"], dy, *s["lru_args"], lw,
                                                                 f"lru_bwd_{l}")
            dproj, dlb, dnw = _hgrn_bwd(s["proj"], s["stc"], dy, dproj, *s["hg_args"], 2 * lw, lw, nh, hd,
                                        f"hgrn_bwd_{l}")
            g["lru_conv_w"][e] = dcw
            g["lru_conv_b"][e] = dcb.reshape(lw)
            g["lru_w_r"][e] = _blockdiag_extract(dwr, LRU_BLOCK)
            g["lru_w_i"][e] = _blockdiag_extract(dwi, LRU_BLOCK)
            g["lru_b_r"][e] = dbr.reshape(lw)
            g["lru_b_i"][e] = dbi.reshape(lw)
            g["lru_lambda"][e] = dlam.reshape(lw)
            g["hg_lower_bounds"][e] = dlb.reshape(hw)
            g["hg_norm_w"][e] = jnp.sum(dnw, axis=(0, 1))
            g["ev_w_in"][e] = _even_unperm(_mm(s["h"], dproj, ta=True, M=D, N=even_in, K=T, name=f"even_in_dw_{l}"),
                                           lw, hd, tcl)
            dh = _mm(dproj, ev_w_in, b_lead=e, tb=True, M=T, N=D, K=even_in, name=f"even_in_dx_{l}")
        else:
            o = l // 2
            dy = _mm(dxmb, w["ssd_w_out"], b_lead=o, tb=True, M=T, N=inner, K=D, name=f"ssd_out_dx_{l}")
            g["ssd_w_out"][o] = _mm(s["y"], dxmb, ta=True, M=inner, N=D, K=T, name=f"ssd_out_dw_{l}")
            dproj, ddt, dcw, dcb, ddtb, da, dd, dnw = _ssd_bwd(s["proj"], s["dtr"], s["sc"], dy, *s["ssd_args"], hpg,
                                                               f"ssd_bwd_{l}")
            ddt = jnp.pad(jnp.swapaxes(ddt, 0, 1).reshape(T, heads), ((0, 0), (0, hp - heads)))
            g["ssd_conv_w"][o] = _ssd_conv_unperm(jnp.swapaxes(dcw, 0, 1), inner)
            g["ssd_conv_b"][o] = _ssd_conv_unperm(dcb[:, 0, :], inner)
            g["ssd_dt_bias"][o] = ddtb.reshape(heads)
            g["ssd_a_log"][o] = da.reshape(heads) * a_neg[o]
            g["ssd_d"][o] = dd.reshape(heads)
            g["ssd_norm_w"][o] = dnw.reshape(inner)
            dmain = _mm(s["h"], dproj, ta=True, M=D, N=dproj.shape[1], K=T, name=f"ssd_in_dw_{l}")
            ddtw = _mm(s["h"], ddt, ta=True, M=D, N=hp, K=T, name=f"ssd_dt_dw_{l}")
            g["ssd_w_in"][o] = _ssd_in_unperm(dmain, ddtw, inner, heads)
            dh = _mm(dproj, ssd_main, b_lead=o, tb=True, M=T, N=D, K=dproj.shape[1], name=f"ssd_in_dx_{l}")
            dh = _mm(ddt, ssd_dtw, b_lead=o, tb=True, M=T, N=D, K=hp, resid=dh, name=f"ssd_dt_dx_{l}")
        dx, dxb, dnm = _rmsnorm_bwd(dh, s["x"], w["norm_mix_w"][l], dxm, f"norm_mix_bwd_{l}")
        g["norm_mix_w"][l] = dnm.reshape(D)

    grads = {k: (jnp.stack(v) if isinstance(v, list) else v) for k, v in g.items()}
    grads["hg_lower_bounds"] = lbs_fn[1](grads["hg_lower_bounds"])[0]
    return loss[0, 0], dx, grads


def _best_div(n, cap, mult=8):
    best = mult
    for d in range(mult, min(n, cap) + 1, mult):
        if n % d == 0:
            best = d
    assert n % best == 0, (n, best)
    return best


def _adamw(wv, gv, mv, vv, name):
    R, C = wv.shape
    tr = _best_div(R, max(8, _SLAB_BLOCK_ELEMS // C))

    def body(w_ref, g_ref, m_ref, v_ref, d_ref, nm_ref, nv_ref):
        g = g_ref[...]
        m = ADAM_B1 * m_ref[...] + (1.0 - ADAM_B1) * g
        v = ADAM_B2 * v_ref[...] + (1.0 - ADAM_B2) * (g * g)
        m_hat = m / (1.0 - ADAM_B1 ** ADAM_STEP)
        v_hat = v / (1.0 - ADAM_B2 ** ADAM_STEP)
        d_ref[...] = -ADAM_LR * (m_hat / (jnp.sqrt(v_hat) + ADAM_EPS) + ADAM_WD * w_ref[...])
        nm_ref[...] = m
        nv_ref[...] = v

    blk = pl.BlockSpec((tr, C), lambda i: (i, 0))
    shp = jax.ShapeDtypeStruct((R, C), f32)
    return _pcall(body, name=name, grid=(R // tr,), in_specs=[blk] * 4, out_specs=[blk] * 3, out_shape=[shp] * 3,
                  sem=("parallel",))(wv, gv, mv, vv)


_ANY = pl.BlockSpec(memory_space=pl.ANY)
_MESH = pl.DeviceIdType.MESH


def _where_am_i():
    x, y, c = lax.axis_index("x"), lax.axis_index("y"), lax.axis_index("c")
    chips = [(1 - x, y), (x, 1 - y), (1 - x, 1 - y)]
    return x, y, c, chips


def _rcopy(src, dst, send_sems, recv_sems, j, dev):
    return pltpu.make_async_remote_copy(src_ref=src, dst_ref=dst, send_sem=send_sems.at[j], recv_sem=recv_sems.at[j],
                                        device_id=dev, device_id_type=_MESH)


def _comm_call(body, name, out_shape, n_sems, n_in):
    scratch = [pltpu.SemaphoreType.DMA((n_sems,)), pltpu.SemaphoreType.DMA((n_sems,))]
    return pl.pallas_call(body, name=name, out_shape=out_shape, in_specs=[_ANY] * n_in, out_specs=_ANY,
                          scratch_shapes=scratch)


def _allgather_chips(shard, name):
    R, C = shard.shape
    rh = R // 2
    assert R % 2 == 0

    def body(x_ref, out_ref, send_sems, recv_sems):
        x, y, c, chips = _where_am_i()
        k = 2 * x + y
        sib = (x, y, 1 - c)

        def blk(kk, half):
            return out_ref.at[kk, pl.ds(half * rh, rh)]

        first =[_rcopy(x_ref.at[pl.ds(c * rh, rh)], blk(k, c), send_sems, recv_sems, j, (cx, cy, c))
                 for j, (cx, cy) in enumerate(chips)]
        for cp in first:
            cp.start()
        passed = []
        for j, (cx, cy) in enumerate(chips):
            kk = 2 * cx + cy
            _rcopy(blk(kk, c), blk(kk, c), send_sems, recv_sems, j, (cx, cy, c)).wait_recv()
            fwd = _rcopy(blk(kk, c), blk(kk, c), send_sems, recv_sems, 3 + j, sib)
            fwd.start()
            passed.append(fwd)
        for j, (cx, cy) in enumerate(chips):
            kk = 2 * cx + cy
            _rcopy(blk(kk, 1 - c), blk(kk, 1 - c), send_sems, recv_sems, 3 + j, sib).wait_recv()
        for cp in first + passed:
            cp.wait_send()

    return _comm_call(body, name, jax.ShapeDtypeStruct((4, R, C), shard.dtype), 6, 1)(shard)


def _d2d_exchange(g, name):
    _, R, C = g.shape
    rh = R // 2

    def body(g_ref, a_ref, send_sems, recv_sems):
        x, y, c, _ = _where_am_i()
        cp = _rcopy(g_ref.at[:, pl.ds((1 - c) * rh, rh)], a_ref, send_sems, recv_sems, 0, (x, y, 1 - c))
        cp.start()
        cp.wait()

    return _comm_call(body, name, jax.ShapeDtypeStruct((4, rh, C), g.dtype), 1, 1)(g)


def _ici_scatter(p, name):
    _, rh, C = p.shape

    def body(p_ref, b_ref, send_sems, recv_sems):
        x, y, c, chips = _where_am_i()
        cps = [_rcopy(p_ref.at[2 * cx + cy], b_ref.at[j], send_sems, recv_sems, j, (cx, cy, c))
               for j, (cx, cy) in enumerate(chips)]
        for cp in cps:
            cp.start()
        for cp in cps:
            cp.wait()

    return _comm_call(body, name, jax.ShapeDtypeStruct((3, rh, C), p.dtype), 3, 1)(p)


def _d2d_swap(q, name):
    R, C = q.shape
    rh = R // 2

    def body(q_ref, o_ref, send_sems, recv_sems):
        del q_ref
        x, y, c, _ = _where_am_i()
        mine = o_ref.at[pl.ds(c * rh, rh)]
        cp = _rcopy(mine, mine, send_sems, recv_sems, 0, (x, y, 1 - c))
        cp.start()
        cp.wait()

    return pl.pallas_call(body, name=name, out_shape=jax.ShapeDtypeStruct((R, C), q.dtype), in_specs=[_ANY],
                          out_specs=_ANY, input_output_aliases={0: 0},
                          scratch_shapes=[pltpu.SemaphoreType.DMA((1,)), pltpu.SemaphoreType.DMA((1,))])(q)


def _add_halves(g, a, c_idx, name):
    _, R, C = g.shape
    rh = R // 2
    tr = _best_div(rh, max(16, _SLAB_BLOCK_ELEMS // C), 16)
    nb = rh // tr

    def body(c_ref, g_ref, a_ref, o_ref):
        del c_ref
        o_ref[...] = (g_ref[...] + a_ref[...]).astype(o_ref.dtype)

    return _pcall(body, name=name, grid=(4, nb), prefetch=1,
                  in_specs=[pl.BlockSpec((1, tr, C), lambda j, i, c: (j, c[0] * nb + i, 0)),
                            pl.BlockSpec((1, tr, C), lambda j, i, c: (j, i, 0))],
                  out_specs=pl.BlockSpec((1, tr, C), lambda j, i, c: (j, i, 0)),
                  out_shape=jax.ShapeDtypeStruct((4, rh, C), _WIRE), sem=("parallel", "parallel"))(c_idx, g, a)


def _add_chips(p, b, ck_idx, name):
    _, rh, C = p.shape
    tr = _best_div(rh, max(16, _SLAB_BLOCK_ELEMS // C), 16)
    nb = rh // tr

    def body(ck_ref, p_ref, b_ref, o_ref):
        del ck_ref
        o_ref[...] = ((p_ref[0].astype(f32) + b_ref[0].astype(f32)) + b_ref[1].astype(f32)) + b_ref[2].astype(f32)

    return _pcall(body, name=name, grid=(nb,), prefetch=1,
                  in_specs=[pl.BlockSpec((1, tr, C), lambda i, ck: (ck[1], i, 0)),
                            pl.BlockSpec((3, tr, C), lambda i, ck: (0, i, 0))],
                  out_specs=pl.BlockSpec((tr, C), lambda i, ck: (ck[0] * nb + i, 0)),
                  out_shape=jax.ShapeDtypeStruct((2 * rh, C), f32), sem=("parallel",))(ck_idx, p, b)


def _reduce_scatter(g, c_idx, ck_idx, name):
    a = _d2d_exchange(g, name + "_d2d")
    p = _add_halves(g, a, c_idx, name + "_add2")
    b = _ici_scatter(p, name + "_ici")
    q = _add_chips(p, b, ck_idx, name + "_add4")
    return _d2d_swap(q, name + "_swap")


def _allreduce_small(buf, name):
    R, C = buf.shape

    def body(x_ref, o_ref, gat_ref, send_sems, recv_sems):
        x, y, c, _ = _where_am_i()
        me = 4 * x + 2 * y + c
        gat_ref[me] = x_ref[...]
        cps = []
        for f in range(1, 8):
            fx, fy, fc = (f >> 2) & 1, (f >> 1) & 1, f & 1
            peer = ((1 - x) if fx else x, (1 - y) if fy else y, (1 - c) if fc else c)
            cps.append(_rcopy(x_ref, gat_ref.at[me], send_sems, recv_sems, f - 1, peer))
        for cp in cps:
            cp.start()
        for cp in cps:
            cp.wait()
        acc = gat_ref[0]
        for d in range(1, 8):
            acc = acc + gat_ref[d]
        o_ref[...] = acc

    vm = pl.BlockSpec(memory_space=pltpu.VMEM)
    return pl.pallas_call(body, name=name, out_shape=jax.ShapeDtypeStruct((R, C), f32), in_specs=[vm], out_specs=vm,
                          scratch_shapes=[pltpu.VMEM((8, R, C), f32), pltpu.SemaphoreType.DMA((7,)),
                                          pltpu.SemaphoreType.DMA((7,))],
                          compiler_params=pltpu.CompilerParams(vmem_limit_bytes=_VMEM_LIMIT))(buf)


def _pack(arrs, cols=_COMM_COLS):
    flat = jnp.concatenate([a.reshape(-1).astype(f32) for a in arrs])
    per = 8 * cols
    n = -(-flat.shape[0] // per) * per
    return jnp.pad(flat, (0, n - flat.shape[0])).reshape(n // cols, cols)


def _unpack(buf, shapes):
    flat = buf.reshape(-1)
    out, off = [], 0
    for s in shapes:
        n = math.prod(s)
        out.append(flat[off:off + n].reshape(s))
        off += n
    return out


_WEIGHTS = ['norm_mix_w', 'norm_ffn_w', 'norm_final_w', 'ev_w_in', 'lru_conv_w', 'lru_conv_b', 'lru_w_r', 'lru_b_r',
            'lru_w_i', 'lru_b_i', 'lru_lambda', 'hg_lower_bounds', 'hg_norm_w', 'ev_w_out', 'ssd_w_in', 'ssd_conv_w',
            'ssd_conv_b', 'ssd_dt_bias', 'ssd_a_log', 'ssd_d', 'ssd_norm_w', 'ssd_w_out', 'ffn_w_up', 'ffn_conv_w',
            'ffn_conv_b', 'ffn_w_down']
_BIG = {'ev_w_in': 2, 'ev_w_out': 1, 'ssd_w_in': 2, 'ssd_w_out': 1, 'ffn_w_up': 2, 'ffn_w_down': 1}
_SMALL_SHARDED = {'lru_conv_w': 2, 'ssd_conv_w': 2, 'ssd_conv_b': 1, 'ssd_norm_w': 1, 'ffn_conv_w': 2}


def _full_shape(shape, ax):
    return tuple(4 * d if i == ax else d for i, d in enumerate(shape))


def kernel(x, norm_mix_w, norm_ffn_w, norm_final_w, ev_w_in, lru_conv_w, lru_conv_b, lru_w_r, lru_b_r, lru_w_i, lru_b_i, lru_lambda, hg_lower_bounds, hg_norm_w, ev_w_out, ssd_w_in, ssd_conv_w, ssd_conv_b, ssd_dt_bias, ssd_a_log, ssd_d, ssd_norm_w, ssd_w_out, ffn_w_up, ffn_conv_w, ffn_conv_b, ffn_w_down, loss_target, m_norm_mix_w, m_norm_ffn_w, m_norm_final_w, m_ev_w_in, m_lru_conv_w, m_lru_conv_b, m_lru_w_r, m_lru_b_r, m_lru_w_i, m_lru_b_i, m_lru_lambda, m_hg_lower_bounds, m_hg_norm_w, m_ev_w_out, m_ssd_w_in, m_ssd_conv_w, m_ssd_conv_b, m_ssd_dt_bias, m_ssd_a_log, m_ssd_d, m_ssd_norm_w, m_ssd_w_out, m_ffn_w_up, m_ffn_conv_w, m_ffn_conv_b, m_ffn_w_down, v_norm_mix_w, v_norm_ffn_w, v_norm_final_w, v_ev_w_in, v_lru_conv_w, v_lru_conv_b, v_lru_w_r, v_lru_b_r, v_lru_w_i, v_lru_b_i, v_lru_lambda, v_hg_lower_bounds, v_hg_norm_w, v_ev_w_out, v_ssd_w_in, v_ssd_conv_w, v_ssd_conv_b, v_ssd_dt_bias, v_ssd_a_log, v_ssd_d, v_ssd_norm_w, v_ssd_w_out, v_ffn_w_up, v_ffn_conv_w, v_ffn_conv_b, v_ffn_w_down):
    given = dict(locals())
    w = {n: given[n] for n in _WEIGHTS}
    m = {n: given["m_" + n] for n in _WEIGHTS}
    v = {n: given["v_" + n] for n in _WEIGHTS}
    cx, cy, cc = lax.axis_index("x"), lax.axis_index("y"), lax.axis_index("c")
    chip = 2 * cx + cy
    c_idx = jnp.reshape(cc, (1,)).astype(jnp.int32)
    ck_idx = jnp.stack([cc, chip]).astype(jnp.int32)

    def rows2d(a):
        return a.reshape(-1, a.shape[-1])

    full = dict(w)
    for n, ax in _BIG.items():
        own = rows2d(w[n].astype(_MXU))
        gat = _allgather_chips(own, "gather_" + n)
        gat = lax.dynamic_update_index_in_dim(gat, own, chip, 0)
        blocks = gat.reshape((4,) + w[n].shape)
        full[n] = jnp.moveaxis(blocks, 0, ax).reshape(_full_shape(w[n].shape, ax))
    placed = []
    for n, ax in _SMALL_SHARDED.items():
        z = jnp.zeros(_full_shape(w[n].shape, ax), f32)
        z = lax.dynamic_update_slice_in_dim(z, w[n].astype(f32), chip * w[n].shape[ax], ax)
        placed.append(jnp.where(cc == 0, z, 0.0))
    small_full = _unpack(_allreduce_small(_pack(placed), "gather_small"), [p.shape for p in placed])
    for n, a in zip(_SMALL_SHARDED, small_full):
        full[n] = a

    loss_part, dx, grads = _local_step(x[0], loss_target[0], full)

    small = [n for n in _WEIGHTS if n not in _BIG]
    red = _allreduce_small(_pack([loss_part.reshape(1)] + [grads[n] for n in small]), "allreduce_small")
    red = _unpack(red, [(1,)] + [grads[n].shape for n in small])
    loss = red[0][0]
    gsum = {}
    for n, a in zip(small, red[1:]):
        if n in _SMALL_SHARDED:
            ax = _SMALL_SHARDED[n]
            a = lax.dynamic_slice_in_dim(a, chip * w[n].shape[ax], w[n].shape[ax], ax)
        gsum[n] = a
    outs = _adamw(_pack([w[n] for n in small]), _pack([gsum[n] for n in small]), _pack([m[n] for n in small]),
                  _pack([v[n] for n in small]), "adamw_small")
    shapes = [w[n].shape for n in small]
    delta, new_m, new_v = ({n: a for n, a in zip(small, _unpack(o, shapes))} for o in outs)

    for n, ax in _BIG.items():
        shp = w[n].shape
        g4 = jnp.moveaxis(grads[n].reshape(shp[:ax] + (4, shp[ax]) + shp[ax + 1:]), ax, 0)
        gs = _reduce_scatter(g4.reshape(4, -1, shp[-1]), c_idx, ck_idx, "rs_" + n)
        d_, m_, v_ = _adamw(rows2d(w[n]), gs, rows2d(m[n]), rows2d(v[n]), "adamw_" + n)
        gsum[n] = gs.reshape(shp)
        delta[n], new_m[n], new_v[n] = d_.reshape(shp), m_.reshape(shp), v_.reshape(shp)

    return (loss, dx[None], *[gsum[n] for n in _WEIGHTS], *[delta[n] for n in _WEIGHTS],
            *[new_m[n] for n in _WEIGHTS], *[new_v[n] for n in _WEIGHTS])
```

```python
import functools
import math

import jax
import jax.numpy as jnp
from jax import lax
from jax.experimental import pallas as pl
from jax.experimental.pallas import tpu as pltpu

f32 = jnp.float32
bf16 = jnp.bfloat16
_MXU = jnp.bfloat16
_WIRE = jnp.bfloat16
_HI = lax.Precision.HIGHEST

EPS = 1e-6
LRU_C = 8.0
LRU_BLOCK = 64
LRU_CONV = 4
HG_CHUNK = 64
SSD_GROUPS = 8
SSD_STATE = 128
SSD_CONV = 4
SSD_CHUNK = 64
FFN_CONV = 3
ADAM_LR, ADAM_B1, ADAM_B2, ADAM_EPS, ADAM_WD, ADAM_STEP = 0.001, 0.9, 0.999, 1e-08, 0.01, 10

_VMEM_LIMIT = 56 * 1024 * 1024
_LANES = 128
_HALO = 8

_TB_NORM = 512
_TB_FFN = 512
_TB_LRU = 256
_TB_HG = 512
_TB_SSD = 256
_TC_FFN = 512
_TC_LRU = 256
_FFN_STRIP, _FFN_CW = 16, 256
_HG_UNROLL = 4
_MM_TM, _MM_TN = 1024, 1024
_MM_TK_ONE = 2048
_MM_TK_MAX = 2816
_COMM_COLS = 1024
_SLAB_BLOCK_ELEMS = 384 * 1024


def _pick(n, prefs):
    for p in prefs:
        if p <= n and n % p == 0:
            return p
    return n


def _pcall(body, *, name, grid, in_specs, out_specs, out_shape, scratch=(), sem=None, aliases=None, prefetch=0):
    params = pltpu.CompilerParams(dimension_semantics=sem, vmem_limit_bytes=_VMEM_LIMIT)
    if prefetch:
        gs = pltpu.PrefetchScalarGridSpec(num_scalar_prefetch=prefetch, grid=grid, in_specs=in_specs,
                                          out_specs=out_specs, scratch_shapes=list(scratch))
        return pl.pallas_call(body, name=name, grid_spec=gs, out_shape=out_shape,
                              input_output_aliases=aliases or {}, compiler_params=params)
    return pl.pallas_call(body, name=name, grid=grid, in_specs=in_specs, out_specs=out_specs, out_shape=out_shape,
                          scratch_shapes=list(scratch), input_output_aliases=aliases or {}, compiler_params=params)


def _sig(x):
    return jax.nn.sigmoid(x)


def _softplus(x):
    return jnp.maximum(x, 0.0) + jnp.log(1.0 + jnp.exp(-jnp.abs(x)))


def _neg_expm1(x):
    series = -x * (1.0 + x * (0.5 + x * (1.0 / 6.0 + x * (1.0 / 24.0 + x * (1.0 / 120.0)))))
    return jnp.where(x > -0.05, series, 1.0 - jnp.exp(x))


_GELU_C = math.sqrt(2.0 / math.pi)


def _gelu_and_grad(x):
    inner = _GELU_C * (x + 0.044715 * x * x * x)
    t = jnp.tanh(inner)
    g = 0.5 * x * (1.0 + t)
    dg = 0.5 * (1.0 + t) + 0.5 * x * (1.0 - t * t) * _GELU_C * (1.0 + 3.0 * 0.044715 * x * x)
    return g, dg


def _silu_and_grad(x):
    s = _sig(x)
    return x * s, s * (1.0 + x * (1.0 - s))


def _dot(a, b, dims=(((1,), (0,)), ((), ()))):
    return lax.dot_general(a.astype(_MXU), b.astype(_MXU), dims, preferred_element_type=f32)


def _dot_nt(a, b):
    return _dot(a, b, (((1,), (1,)), ((), ())))


def _dot_tn(a, b):
    return _dot(a, b, (((0,), (0,)), ((), ())))


def _dot_hi(a, b):
    return jnp.dot(a, b, preferred_element_type=f32, precision=_HI)


def _tri(n, lower=True):
    r = lax.broadcasted_iota(jnp.int32, (n, n), 0)
    c = lax.broadcasted_iota(jnp.int32, (n, n), 1)
    return (r >= c) if lower else (r <= c)


def _shift_down(cur, prev, s):
    if s == 0:
        return cur
    n = cur.shape[0]
    ext = jnp.concatenate([prev, cur], axis=0)
    return pltpu.roll(ext, s, axis=0)[_HALO:_HALO + n]


def _shift_up(cur, nxt, s):
    if s == 0:
        return cur
    n = cur.shape[0]
    ext = jnp.concatenate([cur, nxt], axis=0)
    return pltpu.roll(ext, n + _HALO - s, axis=0)[0:n]


def _conv_fwd(cur, prev, w, b, k):
    y = b + w[k - 1] * cur
    for j in range(k - 1):
        y = y + w[j] * _shift_down(cur, prev, k - 1 - j)
    return y


def _conv_bwd(dy, nxt_dy, cur, prev, w, k):
    del prev
    dx = w[k - 1] * dy
    dws = []
    for j in range(k - 1):
        up = _shift_up(dy, nxt_dy, k - 1 - j)
        dx = dx + w[j] * up
        dws.append(jnp.sum(up * cur, axis=0, keepdims=True))
    dws.append(jnp.sum(dy * cur, axis=0, keepdims=True))
    return dx, dws, jnp.sum(dy, axis=0, keepdims=True)


def _rows(ref, k, cols=None):
    if cols is None:
        return [ref[j:j + 1, :] for j in range(k)]
    return [ref[j:j + 1, cols[0]:cols[1]] for j in range(k)]


def _add_rows(ref, rows, cols=None):
    for j, r in enumerate(rows):
        if cols is None:
            ref[j:j + 1, :] += r
        else:
            ref[j:j + 1, cols[0]:cols[1]] += r


def _mm(a, b, *, M, N, K, ta=False, tb=False, a_off=(0, 0), b_off=(0, 0), resid=None, out_dtype=f32, name,
        b_lead=None, tm=None, tn=None, out_perm=None):
    tm = tm or _pick(M, (_MM_TM, 512, 256, 128))
    tn = tn or _pick(N, (_MM_TN, 512, 256, 128))
    narrow = a.dtype.itemsize <= 2 and b.dtype.itemsize <= 2
    if not narrow:
        tk = _pick(K, (512, 256, 128))
    elif K <= _MM_TK_ONE:
        tk = K
    else:
        tk = max(d for d in range(_LANES, _MM_TK_MAX + 1, _LANES) if K % d == 0)
    nk = K // tk
    if ta:
        assert a_off[0] % tk == 0 and a_off[1] % tm == 0
        a_spec = pl.BlockSpec((tk, tm), lambda i, j, k: (k + a_off[0] // tk, i + a_off[1] // tm))
        a_dim = 0
    else:
        assert a_off[0] % tm == 0 and a_off[1] % tk == 0
        a_spec = pl.BlockSpec((tm, tk), lambda i, j, k: (i + a_off[0] // tm, k + a_off[1] // tk))
        a_dim = 1
    lead = () if b_lead is None else (None,)
    pre = (lambda t: t) if b_lead is None else (lambda t: (b_lead,) + t)
    if tb:
        assert b_off[0] % tn == 0 and b_off[1] % tk == 0
        b_spec = pl.BlockSpec(lead + (tn, tk), lambda i, j, k: pre((j + b_off[0] // tn, k + b_off[1] // tk)))
        b_dim = 1
    else:
        assert b_off[0] % tk == 0 and b_off[1] % tn == 0
        b_spec = pl.BlockSpec(lead + (tk, tn), lambda i, j, k: pre((k + b_off[0] // tk, j + b_off[1] // tn)))
        b_dim = 0
    dims = (((a_dim,), (b_dim,)), ((), ()))
    out_col = (lambda j: j) if out_perm is None else out_perm
    has_resid = resid is not None

    def body_one(*refs):
        a_ref, b_ref = refs[0], refs[1]
        o_ref = refs[-1]
        r = _dot(a_ref[...], b_ref[...], dims)
        if has_resid:
            r = r + refs[2][...].astype(f32)
        o_ref[...] = r.astype(out_dtype)

    def body_acc(*refs):
        a_ref, b_ref = refs[0], refs[1]
        o_ref, acc_ref = refs[-2], refs[-1]
        kk = pl.program_id(2)

        @pl.when(kk == 0)
        def _():
            acc_ref[...] = _dot(a_ref[...], b_ref[...], dims)

        @pl.when(kk > 0)
        def _():
            acc_ref[...] += _dot(a_ref[...], b_ref[...], dims)

        @pl.when(kk == nk - 1)
        def _():
            r = acc_ref[...]
            if has_resid:
                r = r + refs[2][...].astype(f32)
            o_ref[...] = r.astype(out_dtype)

    in_specs = [a_spec, b_spec]
    args = [a, b]
    if has_resid:
        in_specs.append(pl.BlockSpec((tm, tn), lambda i, j, k: (i, j)))
        args.append(resid)
    return _pcall(body_one if nk == 1 else body_acc, name=name, grid=(M // tm, N // tn, nk), in_specs=in_specs,
                  out_specs=pl.BlockSpec((tm, tn), lambda i, j, k: (i, out_col(j))),
                  out_shape=jax.ShapeDtypeStruct((M, N), out_dtype),
                  scratch=[] if nk == 1 else [pltpu.VMEM((tm, tn), f32)],
                  sem=("parallel", "parallel", "arbitrary"))(*args)


def _rmsnorm_fwd(x, w, name):
    T, D = x.shape
    tb = _pick(T, (_TB_NORM, 256, 128, 64))

    def body(x_ref, w_ref, o_ref):
        xv = x_ref[...]
        rstd = lax.rsqrt(jnp.mean(xv * xv, axis=-1, keepdims=True) + EPS)
        o_ref[...] = (xv * rstd * w_ref[...]).astype(o_ref.dtype)

    return _pcall(body, name=name, grid=(T // tb,),
                  in_specs=[pl.BlockSpec((tb, D), lambda i: (i, 0)), pl.BlockSpec((1, D), lambda i: (0, 0))],
                  out_specs=pl.BlockSpec((tb, D), lambda i: (i, 0)),
                  out_shape=jax.ShapeDtypeStruct((T, D), _MXU), sem=("parallel",))(x, w.reshape(1, D))


def _rmsnorm_bwd(dh, x, w, dres, name):
    T, D = x.shape
    tb = _pick(T, (_TB_NORM, 256, 128, 64))

    def body(dh_ref, x_ref, w_ref, dres_ref, dx_ref, dxm_ref, dw_ref):
        xv = x_ref[...]
        rstd = lax.rsqrt(jnp.mean(xv * xv, axis=-1, keepdims=True) + EPS)
        xhat = xv * rstd
        dhv = dh_ref[...]
        g = dhv * w_ref[...]
        dx = dres_ref[...] + rstd * (g - xhat * jnp.mean(g * xhat, axis=-1, keepdims=True))
        dx_ref[...] = dx
        dxm_ref[...] = dx.astype(dxm_ref.dtype)

        @pl.when(pl.program_id(0) == 0)
        def _():
            dw_ref[...] = jnp.zeros_like(dw_ref)

        dw_ref[...] += jnp.sum(dhv * xhat, axis=0, keepdims=True)

    row = pl.BlockSpec((tb, D), lambda i: (i, 0))
    vec = pl.BlockSpec((1, D), lambda i: (0, 0))
    return _pcall(body, name=name, grid=(T // tb,), in_specs=[row, row, vec, row], out_specs=[row, row, vec],
                  out_shape=[jax.ShapeDtypeStruct((T, D), f32), jax.ShapeDtypeStruct((T, D), _MXU),
                             jax.ShapeDtypeStruct((1, D), f32)],
                  sem=("arbitrary",))(dh, x, w.reshape(1, D), dres)


def _loss_head(x, w, target):
    T, D = x.shape
    tb = _pick(T, (_TB_NORM, 256, 128, 64))

    def body(x_ref, w_ref, t_ref, loss_ref, dx_ref, dxm_ref, dw_ref):
        xv = x_ref[...]
        rstd = lax.rsqrt(jnp.mean(xv * xv, axis=-1, keepdims=True) + EPS)
        xhat = xv * rstd
        err = xhat * w_ref[...] - t_ref[...]
        dy = err * (1.0 / D)
        g = dy * w_ref[...]
        dx = rstd * (g - xhat * jnp.mean(g * xhat, axis=-1, keepdims=True))
        dx_ref[...] = dx
        dxm_ref[...] = dx.astype(dxm_ref.dtype)

        @pl.when(pl.program_id(0) == 0)
        def _():
            dw_ref[...] = jnp.zeros_like(dw_ref)
            loss_ref[...] = jnp.zeros_like(loss_ref)

        dw_ref[...] += jnp.sum(dy * xhat, axis=0, keepdims=True)
        part = 0.5 * jnp.sum(jnp.sum(err * err, axis=-1, keepdims=True) * (1.0 / D), axis=0, keepdims=True)
        loss_ref[...] += jnp.broadcast_to(part, loss_ref.shape)

    row = pl.BlockSpec((tb, D), lambda i: (i, 0))
    vec = pl.BlockSpec((1, D), lambda i: (0, 0))
    return _pcall(body, name="loss_head", grid=(T // tb,), in_specs=[row, vec, row],
                  out_specs=[pl.BlockSpec((1, _LANES), lambda i: (0, 0)), row, row, vec],
                  out_shape=[jax.ShapeDtypeStruct((1, _LANES), f32), jax.ShapeDtypeStruct((T, D), f32),
                             jax.ShapeDtypeStruct((T, D), _MXU), jax.ShapeDtypeStruct((1, D), f32)],
                  sem=("arbitrary",))(x, w.reshape(1, D), target)


def _halo_spec(tb, width, col_fn):
    r = tb // _HALO
    return pl.BlockSpec((_HALO, width), lambda j, i: (jnp.maximum(i * r - 1, 0), col_fn(j)))


def _ffn_mid_fwd(up, conv_w, conv_b, name):
    T, F2 = up.shape
    tc = _pick(F2 // 2, (_TC_FFN, 256, 128))
    tb = _pick(T, (_TB_FFN, 256, 128, 64))
    nj = F2 // (2 * tc)

    def body(cur_ref, prev_ref, w_ref, b_ref, o_ref):
        cur = cur_ref[...]
        prev = jnp.where(pl.program_id(1) > 0, prev_ref[...], 0.0)
        u = _conv_fwd(cur, prev, _rows(w_ref, FFN_CONV), b_ref[...], FFN_CONV)
        gate, val = u[:, :tc], u[:, tc:]
        o_ref[...] = (gate * _sig(gate) * val).astype(o_ref.dtype)

    return _pcall(body, name=name, grid=(nj, T // tb),
                  in_specs=[pl.BlockSpec((tb, 2 * tc), lambda j, i: (i, j)), _halo_spec(tb, 2 * tc, lambda j: j),
                            pl.BlockSpec((FFN_CONV, 2 * tc), lambda j, i: (0, j)),
                            pl.BlockSpec((1, 2 * tc), lambda j, i: (0, j))],
                  out_specs=pl.BlockSpec((tb, tc), lambda j, i: (i, j)),
                  out_shape=jax.ShapeDtypeStruct((T, F2 // 2), _MXU), sem=("parallel", "parallel"))(
        up, up, conv_w, conv_b)


def _ffn_mid_bwd(up, dact, conv_w, conv_b, name):
    T, F2 = up.shape
    tc = _pick(F2 // 2, (_TC_FFN, 256, 128))
    tb = _pick(T, (_TB_FFN, 256, 128, 64))
    nj, nt = F2 // (2 * tc), T // tb
    r = tb // _HALO

    cw = _pick(tc, (_FFN_CW, 128))
    rs = _pick(tb, (_FFN_STRIP, 8))
    k = FFN_CONV

    def fold(x):
        acc = x[0:_HALO]
        for r0 in range(_HALO, x.shape[0], _HALO):
            acc = acc + x[r0:r0 + _HALO]
        return acc

    def conv_t(dy, nxt, x, w):
        dx = w[k - 1] * dy
        sums = []
        for j in range(k - 1):
            up = _shift_up(dy, nxt, k - 1 - j)
            dx = dx + w[j] * up
            sums.append(fold(up * x))
        return dx, sums + [fold(dy * x), fold(dy)]

    def body(cur_ref, prev_ref, da_ref, w_ref, b_ref, dup_ref, dw_ref, db_ref, carry_ref):
        i = pl.program_id(1)
        not_first = nt - 1 - i > 0

        @pl.when(i == 0)
        def _():
            carry_ref[...] = jnp.zeros_like(carry_ref)
            dw_ref[...] = jnp.zeros_like(dw_ref)
            db_ref[...] = jnp.zeros_like(db_ref)

        ns = tb // rs
        for g in range(tc // cw):
            gc, vc = slice(g * cw, (g + 1) * cw), slice(tc + g * cw, tc + (g + 1) * cw)
            wg, wv = _rows(w_ref, k, (gc.start, gc.stop)), _rows(w_ref, k, (vc.start, vc.stop))
            bg, bv = b_ref[:, gc], b_ref[:, vc]
            pg = jnp.where(not_first, prev_ref[:, gc], 0.0)
            pv = jnp.where(not_first, prev_ref[:, vc], 0.0)

            def strip(ss, acc):
                s = ns - 1 - ss
                r0 = pl.multiple_of(s * rs, rs)
                rows = pl.ds(r0, rs)
                halo = pl.ds(pl.multiple_of(jnp.maximum(r0 - _HALO, 0), _HALO), _HALO)
                xg, xv = cur_ref[rows, gc], cur_ref[rows, vc]
                gate = _conv_fwd(xg, jnp.where(s > 0, cur_ref[halo, gc], pg), wg, bg, k)
                val = _conv_fwd(xv, jnp.where(s > 0, cur_ref[halo, vc], pv), wv, bv, k)
                da = da_ref[rows, gc]
                sil, dsil = _silu_and_grad(gate)
                dug, duv = da * val * dsil, da * sil
                dxg, sg = conv_t(dug, carry_ref[:, gc], xg, wg)
                dxv, sv = conv_t(duv, carry_ref[:, vc], xv, wv)
                dup_ref[rows, gc] = dxg.astype(dup_ref.dtype)
                dup_ref[rows, vc] = dxv.astype(dup_ref.dtype)
                carry_ref[:, gc] = dug[0:_HALO]
                carry_ref[:, vc] = duv[0:_HALO]
                return tuple(a + b for a, b in zip(acc, sg + sv))

            zero = jnp.zeros((_HALO, cw), f32)
            acc = lax.fori_loop(0, ns, strip, (zero,) * (2 * (k + 1)))
            tot = [jnp.sum(a, axis=0, keepdims=True) for a in acc]
            _add_rows(dw_ref, tot[0:k], (gc.start, gc.stop))
            _add_rows(dw_ref, tot[k + 1:2 * k + 1], (vc.start, vc.stop))
            db_ref[:, gc] += tot[k]
            db_ref[:, vc] += tot[2 * k + 1]

    return _pcall(body, name=name, grid=(nj, nt),
                  in_specs=[pl.BlockSpec((tb, 2 * tc), lambda j, i: (nt - 1 - i, j)),
                            pl.BlockSpec((_HALO, 2 * tc), lambda j, i: (jnp.maximum((nt - 1 - i) * r - 1, 0), j)),
                            pl.BlockSpec((tb, tc), lambda j, i: (nt - 1 - i, j)),
                            pl.BlockSpec((FFN_CONV, 2 * tc), lambda j, i: (0, j)),
                            pl.BlockSpec((1, 2 * tc), lambda j, i: (0, j))],
                  out_specs=[pl.BlockSpec((tb, 2 * tc), lambda j, i: (nt - 1 - i, j)),
                             pl.BlockSpec((FFN_CONV, 2 * tc), lambda j, i: (0, j)),
                             pl.BlockSpec((1, 2 * tc), lambda j, i: (0, j))],
                  out_shape=[jax.ShapeDtypeStruct((T, F2), _MXU), jax.ShapeDtypeStruct((FFN_CONV, F2), f32),
                             jax.ShapeDtypeStruct((1, F2), f32)],
                  scratch=[pltpu.VMEM((_HALO, 2 * tc), f32)], sem=("parallel", "arbitrary"))(
        up, up, dact, conv_w, conv_b)


def _scan_fwd(a, u):
    n = a.shape[0]
    row = lax.broadcasted_iota(jnp.int32, a.shape, 0)
    d = 1
    while d < n:
        a_s = jnp.where(row >= d, pltpu.roll(a, d, axis=0), 1.0)
        u_s = jnp.where(row >= d, pltpu.roll(u, d, axis=0), 0.0)
        u = u + a * u_s
        a = a * a_s
        d *= 2
    return a, u


def _scan_bwd(m, v):
    n = m.shape[0]
    row = lax.broadcasted_iota(jnp.int32, m.shape, 0)
    d = 1
    while d < n:
        ok = row + d < n
        m_s = jnp.where(ok, pltpu.roll(m, n - d, axis=0), 1.0)
        v_s = jnp.where(ok, pltpu.roll(v, n - d, axis=0), 0.0)
        v = v + m * v_s
        m = m * m_s
        d *= 2
    return v


def _lru_gates(xc, wr, br, wi, bi, lam):
    r = _sig(_dot(xc, wr) + br)
    gi = _sig(_dot(xc, wi) + bi)
    sp = _softplus(-lam)
    log_a = -LRU_C * r * sp
    a = jnp.exp(log_a)
    mult = jnp.sqrt(_neg_expm1(2.0 * log_a))
    return r, gi, sp, a, mult


def _lru_fwd(proj, conv_w, conv_b, wr, br, wi, bi, lam, lw, name):
    T = proj.shape[0]
    tc = _pick(lw, (_TC_LRU, 128))
    tb = _pick(T, (_TB_LRU, 128, 64))
    nj = lw // tc

    def body(cur_ref, prev_ref, cw_ref, cb_ref, wr_ref, br_ref, wi_ref, bi_ref, lam_ref, ya_ref, h_ref, carry_ref):
        i = pl.program_id(1)

        @pl.when(i == 0)
        def _():
            carry_ref[...] = jnp.zeros_like(carry_ref)

        xa = cur_ref[:, :tc]
        ga = cur_ref[:, tc:]
        prev = jnp.where(i > 0, prev_ref[:, :tc], 0.0)
        xc = _conv_fwd(xa, prev, _rows(cw_ref, LRU_CONV), cb_ref[...], LRU_CONV)
        r, gi, sp, a, mult = _lru_gates(xc, wr_ref[0], br_ref[...], wi_ref[0], bi_ref[...], lam_ref[...])
        acum, h = _scan_fwd(a, mult * gi * xc)
        h = h + acum * carry_ref[0:1, :]
        h_ref[...] = h
        gel, _ = _gelu_and_grad(ga)
        ya_ref[...] = (h * gel).astype(ya_ref.dtype)
        carry_ref[...] = pltpu.roll(h, 1, axis=0)[0:_HALO]

    vec = pl.BlockSpec((1, tc), lambda j, i: (0, j))
    mat = pl.BlockSpec((1, tc, tc), lambda j, i: (j, 0, 0))
    out = pl.BlockSpec((tb, tc), lambda j, i: (i, j))
    return _pcall(body, name=name, grid=(nj, T // tb),
                  in_specs=[pl.BlockSpec((tb, 2 * tc), lambda j, i: (i, j)), _halo_spec(tb, 2 * tc, lambda j: j),
                            pl.BlockSpec((LRU_CONV, tc), lambda j, i: (0, j)), vec, mat, vec, mat, vec, vec],
                  out_specs=[out, out],
                  out_shape=[jax.ShapeDtypeStruct((T, lw), _MXU), jax.ShapeDtypeStruct((T, lw), f32)],
                  scratch=[pltpu.VMEM((_HALO, tc), f32)], sem=("parallel", "arbitrary"))(
        proj, proj, conv_w, conv_b, wr, br, wi, bi, lam)


def _lru_bwd(proj, h, dy, conv_w, conv_b, wr, br, wi, bi, lam, lw, name):
    T, width = proj.shape
    tc = _pick(lw, (_TC_LRU, 128))
    tb = _pick(T, (_TB_LRU, 128, 64))
    nj, nt = lw // tc, T // tb
    rr = tb // _HALO

    def body(cur_ref, prev_ref, h_ref, hp_ref, dy_ref, cw_ref, cb_ref, wr_ref, br_ref, wi_ref, bi_ref, lam_ref,
             dp_ref, dcw_ref, dcb_ref, dwr_ref, dbr_ref, dwi_ref, dbi_ref, dlam_ref, cl_ref, cx_ref):
        i = pl.program_id(1)
        ti = nt - 1 - i

        @pl.when(i == 0)
        def _():
            for ref in (cl_ref, cx_ref, dcw_ref, dcb_ref, dwr_ref, dbr_ref, dwi_ref, dbi_ref, dlam_ref):
                ref[...] = jnp.zeros_like(ref)

        xa = cur_ref[:, :tc]
        ga = cur_ref[:, tc:]
        prev = jnp.where(ti > 0, prev_ref[:, :tc], 0.0)
        cw = _rows(cw_ref, LRU_CONV)
        xc = _conv_fwd(xa, prev, cw, cb_ref[...], LRU_CONV)
        wrv, wiv, lam_v = wr_ref[0], wi_ref[0], lam_ref[...]
        r, gi, sp, a, mult = _lru_gates(xc, wrv, br_ref[...], wiv, bi_ref[...], lam_v)
        hcur = h_ref[...]
        hprev = _shift_down(hcur, jnp.where(ti > 0, hp_ref[...], 0.0), 1)
        gel, dgel = _gelu_and_grad(ga)
        dya = dy_ref[...]
        dga = dya * hcur * dgel
        row = lax.broadcasted_iota(jnp.int32, a.shape, 0)
        v = dya * gel + jnp.where(row == tb - 1, cl_ref[0:1, :], 0.0)
        m = jnp.where(row < tb - 1, pltpu.roll(a, tb - 1, axis=0), 0.0)
        lamb = _scan_bwd(m, v)
        cl_ref[...] = (a * lamb)[0:_HALO]
        da = lamb * hprev
        dmult = lamb * gi * xc
        dgi = lamb * mult * xc
        dxc = lamb * mult * gi
        dlog_a = da * a - dmult * (a * a) / mult
        dr = dlog_a * (-LRU_C * sp)
        dlam_ref[...] += jnp.sum(dlog_a * (-LRU_C) * r, axis=0, keepdims=True) * (-_sig(-lam_v))
        dpr = dr * r * (1.0 - r)
        dpi = dgi * gi * (1.0 - gi)
        dbr_ref[...] += jnp.sum(dpr, axis=0, keepdims=True)
        dbi_ref[...] += jnp.sum(dpi, axis=0, keepdims=True)
        dwr_ref[0] += _dot_tn(xc, dpr)
        dwi_ref[0] += _dot_tn(xc, dpi)
        dxc = dxc + _dot_nt(dpr, wrv) + _dot_nt(dpi, wiv)
        dxa, dws, db = _conv_bwd(dxc, cx_ref[...], xa, prev, cw, LRU_CONV)
        cx_ref[...] = dxc[0:_HALO]
        _add_rows(dcw_ref, dws)
        dcb_ref[...] += db
        dp_ref[...] = jnp.concatenate([dxa, dga], axis=1).astype(dp_ref.dtype)

    vec = pl.BlockSpec((1, tc), lambda j, i: (0, j))
    mat = pl.BlockSpec((1, tc, tc), lambda j, i: (j, 0, 0))
    cwspec = pl.BlockSpec((LRU_CONV, tc), lambda j, i: (0, j))
    blk = pl.BlockSpec((tb, tc), lambda j, i: (nt - 1 - i, j))
    vshape = jax.ShapeDtypeStruct((1, lw), f32)
    mshape = jax.ShapeDtypeStruct((nj, tc, tc), f32)
    return _pcall(body, name=name, grid=(nj, nt),
                  in_specs=[pl.BlockSpec((tb, 2 * tc), lambda j, i: (nt - 1 - i, j)),
                            pl.BlockSpec((_HALO, 2 * tc), lambda j, i: (jnp.maximum((nt - 1 - i) * rr - 1, 0), j)),
                            blk,
                            pl.BlockSpec((_HALO, tc), lambda j, i: (jnp.maximum((nt - 1 - i) * rr - 1, 0), j)),
                            blk, cwspec, vec, mat, vec, mat, vec, vec],
                  out_specs=[pl.BlockSpec((tb, 2 * tc), lambda j, i: (nt - 1 - i, j)), cwspec, vec, mat, vec, mat, vec,
                             vec],
                  out_shape=[jax.ShapeDtypeStruct((T, width), _MXU), jax.ShapeDtypeStruct((LRU_CONV, lw), f32),
                             vshape, mshape, vshape, mshape, vshape, vshape],
                  scratch=[pltpu.VMEM((_HALO, tc), f32), pltpu.VMEM((_HALO, tc), f32)],
                  sem=("parallel", "arbitrary"))(
        proj, proj, h, h, dy, conv_w, conv_b, wr, br, wi, bi, lam)


def _hg_chunk_fwd(blk, lb, hd):
    q, fr, v, gt = blk[:, 0:hd], blk[:, hd:2 * hd], blk[:, 2 * hd:3 * hd], blk[:, 3 * hd:4 * hd]
    sg = _sig(fr)
    f = lb + (1.0 - lb) * sg
    k = (1.0 - lb) * (1.0 - sg)
    g = jnp.log(f)
    n = HG_CHUNK
    cum = _dot_hi(_tri(n).astype(f32), g)
    row = lax.broadcasted_iota(jnp.int32, g.shape, 0)
    tot = jnp.sum(g, axis=0, keepdims=True)
    mid = jnp.sum(jnp.where(row < n // 2, g, 0.0), axis=0, keepdims=True)
    sq = _sig(q)
    qs = q * sq * (hd ** -0.5)
    e_qd, e_kd, e_qc, e_kt = jnp.exp(cum - mid), jnp.exp(mid - cum), jnp.exp(cum), jnp.exp(tot - cum)
    qd, kd, qc, kt = qs * e_qd, k * e_kd, qs * e_qc, k * e_kt
    p = jnp.where(_tri(n), _dot_nt(qd, kd), 0.0)
    return dict(q=q, v=v, gt=gt, sg=sg, f=f, k=k, sq=sq, qs=qs, tot=tot, e_qd=e_qd, e_kd=e_kd, e_qc=e_qc,
                e_kt=e_kt, qd=qd, kd=kd, qc=qc, kt=kt, p=p)


def _hgrn_fwd(proj, lb, nw, col0, nh, hd, name):
    T = proj.shape[0]
    tb = _pick(T, (_TB_HG, 256, 128, 64))
    ncb = tb // HG_CHUNK
    cb0 = col0 // (4 * hd)

    def body(p_ref, lb_ref, nw_ref, y_ref, sc_ref, st_ref):
        @pl.when(pl.program_id(1) == 0)
        def _():
            st_ref[...] = jnp.zeros_like(st_ref)

        lbv, nwv = lb_ref[...], nw_ref[...]

        def chunk(c, carry):
            r0 = pl.multiple_of(c * HG_CHUNK, HG_CHUNK)
            d = _hg_chunk_fwd(p_ref[pl.ds(r0, HG_CHUNK), :], lbv, hd)
            st = st_ref[...]
            sc_ref[c, 0] = st
            o = _dot(d["p"], d["v"]) + _dot_nt(d["qc"], st)
            st_ref[...] = st * jnp.exp(d["tot"]) + _dot_tn(d["v"], d["kt"])
            rstd = lax.rsqrt(jnp.mean(o * o, axis=-1, keepdims=True) + EPS)
            sil, _ = _silu_and_grad(d["gt"])
            y_ref[pl.ds(r0, HG_CHUNK), :] = (o * rstd * nwv * sil).astype(y_ref.dtype)
            return carry

        lax.fori_loop(0, ncb, chunk, 0, unroll=_HG_UNROLL if ncb % _HG_UNROLL == 0 else 1)

    return _pcall(body, name=name, grid=(nh, T // tb),
                  in_specs=[pl.BlockSpec((tb, 4 * hd), lambda h, i: (i, cb0 + h)),
                            pl.BlockSpec((1, hd), lambda h, i: (0, h)), pl.BlockSpec((1, hd), lambda h, i: (0, 0))],
                  out_specs=[pl.BlockSpec((tb, hd), lambda h, i: (i, h)),
                             pl.BlockSpec((ncb, 1, hd, hd), lambda h, i: (i, h, 0, 0))],
                  out_shape=[jax.ShapeDtypeStruct((T, nh * hd), _MXU),
                             jax.ShapeDtypeStruct((T // HG_CHUNK, nh, hd, hd), f32)],
                  scratch=[pltpu.VMEM((hd, hd), f32)], sem=("parallel", "arbitrary"))(proj, lb, nw)


def _hgrn_bwd(proj, st_chunks, dy, dproj, lb, nw, col0, dycol0, nh, hd, name):
    T = proj.shape[0]
    tb = _pick(T, (_TB_HG, 256, 128, 64))
    ncb = tb // HG_CHUNK
    nt = T // tb
    cb0 = col0 // (4 * hd)
    dyb0 = dycol0 // hd
    n = HG_CHUNK

    def body(p_ref, sc_ref, dy_ref, dpin_ref, lb_ref, nw_ref, dp_ref, dlb_ref, dnw_ref, dst_ref):
        del dpin_ref

        @pl.when(pl.program_id(1) == 0)
        def _():
            dst_ref[...] = jnp.zeros_like(dst_ref)
            dlb_ref[...] = jnp.zeros_like(dlb_ref)
            dnw_ref[...] = jnp.zeros_like(dnw_ref)

        lbv, nwv = lb_ref[...], nw_ref[...]
        tril = _tri(n)

        def chunk(cc, carry):
            c = ncb - 1 - cc
            r0 = pl.multiple_of(c * HG_CHUNK, HG_CHUNK)
            d = _hg_chunk_fwd(p_ref[pl.ds(r0, HG_CHUNK), :], lbv, hd)
            st = sc_ref[c, 0]
            v, gt = d["v"], d["gt"]
            o = _dot(d["p"], v) + _dot_nt(d["qc"], st)
            rstd = lax.rsqrt(jnp.mean(o * o, axis=-1, keepdims=True) + EPS)
            sil, dsil = _silu_and_grad(gt)
            dyv = dy_ref[pl.ds(r0, HG_CHUNK), :]
            xhat = o * rstd
            don = dyv * sil
            dgt = dyv * xhat * nwv * dsil
            dnw_ref[0] += jnp.sum(don * xhat, axis=0, keepdims=True)
            g2 = don * nwv
            do = rstd * (g2 - xhat * jnp.mean(g2 * xhat, axis=-1, keepdims=True))
            dstn = dst_ref[...]
            dp = jnp.where(tril, _dot_nt(do, v), 0.0)
            dv = _dot_tn(d["p"], do) + _dot_nt(d["kt"], dstn)
            dqd = _dot(dp, d["kd"])
            dkd = _dot_tn(dp, d["qd"])
            dqc = _dot(do, st)
            dkt = _dot(v, dstn)
            e_tot = jnp.exp(d["tot"])
            dst_ref[...] = dstn * e_tot + _dot_tn(do, d["qc"])
            dtot = jnp.sum(e_tot * st * dstn, axis=0, keepdims=True) + jnp.sum(dkt * d["kt"], axis=0, keepdims=True)
            dqs = dqd * d["e_qd"] + dqc * d["e_qc"]
            dk = dkd * d["e_kd"] + dkt * d["e_kt"]
            dcum = dqd * d["qd"] - dkd * d["kd"] + dqc * d["qc"] - dkt * d["kt"]
            dg = _dot_hi(_tri(n, lower=False).astype(f32), dcum) + dtot
            df = dg / d["f"] - dk
            sg = d["sg"]
            dfr = df * (1.0 - lbv) * sg * (1.0 - sg)
            dlb_ref[...] += jnp.sum(df * (1.0 - sg), axis=0, keepdims=True)
            sq, q = d["sq"], d["q"]
            dq = dqs * (hd ** -0.5) * sq * (1.0 + q * (1.0 - sq))
            dp_ref[pl.ds(r0, HG_CHUNK), :] = jnp.concatenate([dq, dfr, dv, dgt], axis=1).astype(dp_ref.dtype)
            return carry

        lax.fori_loop(0, ncb, chunk, 0, unroll=_HG_UNROLL if ncb % _HG_UNROLL == 0 else 1)

    width = dproj.shape[1]
    return _pcall(body, name=name, grid=(nh, nt),
                  in_specs=[pl.BlockSpec((tb, 4 * hd), lambda h, i: (nt - 1 - i, cb0 + h)),
                            pl.BlockSpec((ncb, 1, hd, hd), lambda h, i: (nt - 1 - i, h, 0, 0)),
                            pl.BlockSpec((tb, hd), lambda h, i: (nt - 1 - i, dyb0 + h)),
                            pl.BlockSpec(memory_space=pl.ANY),
                            pl.BlockSpec((1, hd), lambda h, i: (0, h)), pl.BlockSpec((1, hd), lambda h, i: (0, 0))],
                  out_specs=[pl.BlockSpec((tb, 4 * hd), lambda h, i: (nt - 1 - i, cb0 + h)),
                             pl.BlockSpec((1, hd), lambda h, i: (0, h)),
                             pl.BlockSpec((1, 1, hd), lambda h, i: (h, 0, 0))],
                  out_shape=[jax.ShapeDtypeStruct((T, width), _MXU), jax.ShapeDtypeStruct((1, nh * hd), f32),
                             jax.ShapeDtypeStruct((nh, 1, hd), f32)],
                  scratch=[pltpu.VMEM((hd, hd), f32)], sem=("parallel", "arbitrary"), aliases={3: 0})(
        proj, st_chunks, dy, dproj, lb, nw)


def _ssd_dt(dtraw_c, g, hpg, dtb_row, dtb_col, a_row, a_col):
    hp = dtraw_c.shape[1]
    sel = (lax.broadcasted_iota(jnp.int32, (hp, hpg), 0) == g * hpg + lax.broadcasted_iota(jnp.int32, (hp, hpg), 1))
    selt = (lax.broadcasted_iota(jnp.int32, (hpg, hp), 1) == g * hpg + lax.broadcasted_iota(jnp.int32, (hpg, hp), 0))
    pre = _dot_hi(dtraw_c, sel.astype(f32)) + dtb_row
    pre_t = lax.dot_general(selt.astype(f32), dtraw_c, (((1,), (1,)), ((), ())), preferred_element_type=f32,
                            precision=_HI) + dtb_col
    dt = _softplus(pre)
    dt_t = _softplus(pre_t)
    return pre, dt, dt * a_row, dt_t * a_col


def _ssd_chunk_fwd(xbc, z, dtraw_c, g, prm, s_ref, hpg, gw):
    n, ns, pp = SSD_CHUNK, SSD_STATE, 2 * (gw // hpg)
    dtb_row, dtb_col, a_row, a_col, d_row = prm
    bm, cm = xbc[:, gw:gw + ns], xbc[:, gw + ns:gw + 2 * ns]
    pre, dt, a, a_t = _ssd_dt(dtraw_c, g, hpg, dtb_row, dtb_col, a_row, a_col)
    cum = _dot_hi(_tri(n).astype(f32), a)
    cum_t = _dot_hi(a_t, _tri(n, lower=False).astype(f32))
    tot = jnp.sum(a, axis=0, keepdims=True)
    cb = _dot_nt(cm, bm)
    tril = _tri(n)
    lo = lax.broadcasted_iota(jnp.int32, (n, pp), 1) < pp // 2
    sub_lo = lax.broadcasted_iota(jnp.int32, (pp, ns), 0) < pp // 2

    def per_lane(m, h0):
        return jnp.where(lo[0:m.shape[0]], m[:, h0:h0 + 1], m[:, h0 + 1:h0 + 2])

    pairs = []
    for p in range(hpg // 2):
        h0 = 2 * p
        xp = xbc[:, pp * p:pp * (p + 1)]
        dtl = per_lane(dt, h0)
        xdt = xp * dtl
        lm, mm = [], []
        for h in (h0, h0 + 1):
            l_h = jnp.where(tril, jnp.exp(jnp.minimum(cum[:, h:h + 1] - cum_t[h:h + 1, :], 0.0)), 0.0)
            lm.append(l_h)
            mm.append(cb * l_h)
        y_diag = _dot(mm[0], jnp.where(lo, xdt, 0.0)) + _dot(mm[1], jnp.where(lo, 0.0, xdt))
        sp = s_ref[p]
        e = jnp.exp(per_lane(cum, h0))
        y_off = _dot_nt(cm, sp) * e
        fd = jnp.exp(per_lane(tot - cum, h0))
        u = xdt * fd
        etot = jnp.exp(tot)
        etot_col = jnp.where(sub_lo, etot[:, h0:h0 + 1], etot[:, h0 + 1:h0 + 2])
        dl = per_lane(d_row, h0)
        y = y_diag + y_off + xp * dl
        zp = z[:, pp * p:pp * (p + 1)]
        sil, dsil = _silu_and_grad(zp)
        pairs.append(dict(xp=xp, dtl=dtl, xdt=xdt, lm=lm, mm=mm, y_diag=y_diag, sp=sp, e=e, y_off=y_off, fd=fd, u=u,
                          etot_col=etot_col, dl=dl, y=y, sil=sil, dsil=dsil, y2=y * sil))
    ms = sum(jnp.sum(q["y2"] * q["y2"], axis=-1, keepdims=True) for q in pairs) * (1.0 / gw)
    rstd = lax.rsqrt(ms + EPS)
    return dict(bm=bm, cm=cm, pre=pre, dt=dt, a=a, cum=cum, cum_t=cum_t, tot=tot, cb=cb, lo=lo, sub_lo=sub_lo,
                pairs=pairs, rstd=rstd)


def _ssd_specs(tb, gwx, hpg, hp, cw, gw, tmap):
    r = tb // _HALO
    row = pl.BlockSpec((1, 1, hpg), lambda g, i: (g, 0, 0))
    col = pl.BlockSpec((1, hpg, 1), lambda g, i: (g, 0, 0))
    return [pl.BlockSpec((tb, gwx), lambda g, i: (tmap(i), g)),
            pl.BlockSpec((_HALO, gwx), lambda g, i: (jnp.maximum(tmap(i) * r - 1, 0), g)),
            pl.BlockSpec((tb, hp), lambda g, i: (tmap(i), 0)),
            pl.BlockSpec((1, SSD_CONV, cw), lambda g, i: (g, 0, 0)),
            pl.BlockSpec((1, 1, cw), lambda g, i: (g, 0, 0)),
            row, col, row, col, row,
            pl.BlockSpec((1, gw), lambda g, i: (0, g))]


def _ssd_fwd(proj, dt_raw, conv_w, conv_b, dtb_row, dtb_col, a_row, a_col, d_row, norm_w, hpg, name):
    T = proj.shape[0]
    G = SSD_GROUPS
    gwx = proj.shape[1] // G
    gw = (gwx - 2 * SSD_STATE) // 2
    cw = gw + 2 * SSD_STATE
    hp = dt_raw.shape[1]
    tb = _pick(T, (_TB_SSD, 128, 64))
    ncb = tb // SSD_CHUNK
    npair = hpg // 2
    pp = 2 * (gw // hpg)
    assert pp == _LANES and hpg % 2 == 0

    def body(p_ref, prev_ref, dt_ref, cw_ref, cb_ref, dtbr_ref, dtbc_ref, ar_ref, ac_ref, dr_ref, nw_ref,
             y_ref, sc_ref, s_ref, xbc_ref):
        g = pl.program_id(0)
        i = pl.program_id(1)

        @pl.when(i == 0)
        def _():
            s_ref[...] = jnp.zeros_like(s_ref)

        raw = p_ref[:, gw:]
        prev = jnp.where(i > 0, prev_ref[:, gw:], 0.0)
        pre = _conv_fwd(raw, prev, [cw_ref[0, j:j + 1, :] for j in range(SSD_CONV)], cb_ref[0], SSD_CONV)
        xbc_ref[...] = pre * _sig(pre)
        prm = (dtbr_ref[0], dtbc_ref[0], ar_ref[0], ac_ref[0], dr_ref[0])
        nwv = nw_ref[...]

        def chunk(c, carry):
            r0 = pl.multiple_of(c * SSD_CHUNK, SSD_CHUNK)
            rows = pl.ds(r0, SSD_CHUNK)
            sc_ref[c, 0] = s_ref[...]
            d = _ssd_chunk_fwd(xbc_ref[rows, :], p_ref[rows, 0:gw], dt_ref[rows, :], g, prm, s_ref, hpg, gw)
            for p, q in enumerate(d["pairs"]):
                s_ref[p] = q["sp"] * q["etot_col"] + _dot_tn(q["u"], d["bm"])
                y_ref[rows, pp * p:pp * (p + 1)] = (q["y2"] * d["rstd"] * nwv[:, pp * p:pp * (p + 1)]).astype(
                    y_ref.dtype)
            return carry

        lax.fori_loop(0, ncb, chunk, 0, unroll=2 if ncb % 2 == 0 else 1)

    return _pcall(body, name=name, grid=(G, T // tb),
                  in_specs=_ssd_specs(tb, gwx, hpg, hp, cw, gw, lambda i: i),
                  out_specs=[pl.BlockSpec((tb, gw), lambda g, i: (i, g)),
                             pl.BlockSpec((ncb, 1, npair, pp, SSD_STATE), lambda g, i: (i, g, 0, 0, 0))],
                  out_shape=[jax.ShapeDtypeStruct((T, G * gw), _MXU),
                             jax.ShapeDtypeStruct((T // SSD_CHUNK, G, npair, pp, SSD_STATE), f32)],
                  scratch=[pltpu.VMEM((npair, pp, SSD_STATE), f32), pltpu.VMEM((tb, cw), f32)],
                  sem=("parallel", "arbitrary"))(
        proj, proj, dt_raw, conv_w, conv_b, dtb_row, dtb_col, a_row, a_col, d_row, norm_w)


def _ssd_bwd(proj, dt_raw, s_chunks, dy, conv_w, conv_b, dtb_row, dtb_col, a_row, a_col, d_row, norm_w, hpg, name):
    T = proj.shape[0]
    G = SSD_GROUPS
    gwx = proj.shape[1] // G
    gw = (gwx - 2 * SSD_STATE) // 2
    ns = SSD_STATE
    cw = gw + 2 * ns
    hp = dt_raw.shape[1]
    tb = _pick(T, (_TB_SSD, 128, 64))
    ncb, nt = tb // SSD_CHUNK, T // tb
    npair = hpg // 2
    pp = 2 * (gw // hpg)
    n = SSD_CHUNK

    def body(p_ref, prev_ref, dt_ref, cw_ref, cb_ref, dtbr_ref, dtbc_ref, ar_ref, ac_ref, dr_ref, nw_ref,
             sc_ref, dy_ref,
             dp_ref, ddt_ref, dcw_ref, dcb_ref, ddtb_ref, da_ref, dd_ref, dnw_ref,
             ds_ref, s_ref, xbc_ref, dxbc_ref, carry_ref):
        g = pl.program_id(0)
        i = pl.program_id(1)
        ti = nt - 1 - i

        @pl.when(i == 0)
        def _():
            for ref in (ds_ref, carry_ref, dcw_ref, dcb_ref, ddtb_ref, da_ref, dd_ref, dnw_ref):
                ref[...] = jnp.zeros_like(ref)

        raw = p_ref[:, gw:]
        prev = jnp.where(ti > 0, prev_ref[:, gw:], 0.0)
        cwr = [cw_ref[0, j:j + 1, :] for j in range(SSD_CONV)]
        pre = _conv_fwd(raw, prev, cwr, cb_ref[0], SSD_CONV)
        xbc_ref[...] = pre * _sig(pre)
        prm = (dtbr_ref[0], dtbc_ref[0], ar_ref[0], ac_ref[0], dr_ref[0])
        a_row = ar_ref[0]
        nwv = nw_ref[...]
        tril = _tri(n)
        lane8 = lax.broadcasted_iota(jnp.int32, (n, hpg), 1)

        def chunk(cc, carry):
            c = ncb - 1 - cc
            r0 = pl.multiple_of(c * SSD_CHUNK, SSD_CHUNK)
            rows = pl.ds(r0, SSD_CHUNK)
            s_ref[...] = sc_ref[c, 0]
            xbc = xbc_ref[rows, :]
            d = _ssd_chunk_fwd(xbc, p_ref[rows, 0:gw], dt_ref[rows, :], g, prm, s_ref, hpg, gw)
            bm, cm, lo, sub_lo, rstd = d["bm"], d["cm"], d["lo"], d["sub_lo"], d["rstd"]
            dyo = dy_ref[rows, :]
            s1 = sum(jnp.sum(dyo[:, pp * p:pp * (p + 1)] * nwv[:, pp * p:pp * (p + 1)] * q["y2"], axis=-1,
                             keepdims=True) for p, q in enumerate(d["pairs"]))
            dcb = jnp.zeros((n, n), f32)
            cb_t = _dot_nt(bm, cm)
            triu = _tri(n, lower=False)
            dbm = jnp.zeros((n, ns), f32)
            dcm = jnp.zeros((n, ns), f32)
            dcum = jnp.zeros((n, hpg), f32)
            ddt_dir = jnp.zeros((n, hpg), f32)
            dtot = jnp.zeros((1, hpg), f32)
            ddv = jnp.zeros((1, hpg), f32)
            dxs = []
            for p, q in enumerate(d["pairs"]):
                h0 = 2 * p
                sl = slice(pp * p, pp * (p + 1))
                dyp = dyo[:, sl]
                nwp = nwv[:, sl]
                dnw_ref[:, sl] += jnp.sum(dyp * q["y2"] * rstd, axis=0, keepdims=True)
                dy2 = rstd * (dyp * nwp - q["y2"] * (rstd * rstd) * s1 * (1.0 / gw))
                dyv = dy2 * q["sil"]
                dz = dy2 * q["y"] * q["dsil"]
                dp_ref[rows, sl] = dz.astype(dp_ref.dtype)
                xp, xdt, mm, lm = q["xp"], q["xdt"], q["mm"], q["lm"]
                dyx = dyv * xp
                dx = dyv * q["dl"]
                dxdt_diag = jnp.where(lo, _dot_tn(mm[0], dyv), _dot_tn(mm[1], dyv))
                dy_lo, dy_hi = jnp.where(lo, dyv, 0.0), jnp.where(lo, 0.0, dyv)
                dm0 = jnp.where(tril, _dot_nt(dy_lo, xdt), 0.0)
                dm1 = jnp.where(tril, _dot_nt(dy_hi, xdt), 0.0)
                dcb = dcb + dm0 * lm[0] + dm1 * lm[1]
                seg = []
                for hh, (dm, dyh) in enumerate(((dm0, dy_lo), (dm1, dy_hi))):
                    h = h0 + hh
                    l_t = jnp.where(triu, jnp.exp(jnp.minimum(d["cum_t"][h:h + 1, :] - d["cum"][:, h:h + 1], 0.0)), 0.0)
                    dm_t = _dot_nt(xdt, dyh)
                    seg.append(jnp.sum(dm * mm[hh], axis=1, keepdims=True)
                               - jnp.sum(dm_t * (cb_t * l_t), axis=1, keepdims=True))
                dsn = ds_ref[p]
                dzz = dyv * q["e"]
                dcm = dcm + _dot(dzz, q["sp"])
                ds_ref[p] = dsn * q["etot_col"] + _dot_tn(dzz, cm)
                du = _dot_nt(bm, dsn)
                dbm = dbm + _dot(q["u"], dsn)
                dxdt = dxdt_diag + du * q["fd"]
                wv = du * q["u"]
                cterm = dyv * q["y_off"] - wv
                ddir = dxdt * xp
                sdec = q["etot_col"] * q["sp"] * dsn

                def halves(x):
                    lo_s = jnp.sum(jnp.where(lo, x, 0.0), axis=1, keepdims=True)
                    return lo_s, jnp.sum(x, axis=1, keepdims=True) - lo_s

                c_lo, c_hi = halves(cterm)
                d_lo, d_hi = halves(ddir)
                w_lo, w_hi = halves(wv)
                x_lo, x_hi = halves(dyx)
                s_all = jnp.sum(sdec, axis=1, keepdims=True)
                s_lo = jnp.sum(jnp.where(sub_lo[:, 0:1], s_all, 0.0), axis=0, keepdims=True)
                s_hi = jnp.sum(s_all, axis=0, keepdims=True) - s_lo
                for h, cv, dv_, wv_, xv_, sv_ in ((h0, c_lo + seg[0], d_lo, w_lo, x_lo, s_lo),
                                                  (h0 + 1, c_hi + seg[1], d_hi, w_hi, x_hi, s_hi)):
                    sel = lane8 == h
                    dcum = dcum + jnp.where(sel, cv, 0.0)
                    ddt_dir = ddt_dir + jnp.where(sel, dv_, 0.0)
                    dtot = dtot + jnp.where(sel[0:1], jnp.sum(wv_, axis=0, keepdims=True) + sv_, 0.0)
                    ddv = ddv + jnp.where(sel[0:1], jnp.sum(xv_, axis=0, keepdims=True), 0.0)
                dxs.append(dx + dxdt * q["dtl"])
            da_c = _dot_hi(_tri(n, lower=False).astype(f32), dcum) + dtot
            ddt = da_c * a_row + ddt_dir
            da_ref[0] += jnp.sum(da_c * d["dt"], axis=0, keepdims=True)
            dd_ref[0] += ddv
            ddt_raw = ddt * _sig(d["pre"])
            ddtb_ref[0] += jnp.sum(ddt_raw, axis=0, keepdims=True)
            ddt_ref[0, rows, :] = ddt_raw
            dcm = dcm + _dot(dcb, bm)
            dbm = dbm + _dot_tn(dcb, cm)
            dxbc_ref[rows, :] = jnp.concatenate(dxs + [dbm, dcm], axis=1)
            return carry

        lax.fori_loop(0, ncb, chunk, 0)
        _, dsil = _silu_and_grad(pre)
        dpre = dxbc_ref[...] * dsil
        draw, dws, db = _conv_bwd(dpre, carry_ref[...], raw, prev, cwr, SSD_CONV)
        carry_ref[...] = dpre[0:_HALO]
        for j, r_ in enumerate(dws):
            dcw_ref[0, j:j + 1, :] += r_
        dcb_ref[0] += db
        dp_ref[:, gw:] = draw.astype(dp_ref.dtype)

    tmap = lambda i: nt - 1 - i
    row = pl.BlockSpec((1, 1, hpg), lambda g, i: (g, 0, 0))
    rshape = jax.ShapeDtypeStruct((G, 1, hpg), f32)
    return _pcall(body, name=name, grid=(G, nt),
                  in_specs=_ssd_specs(tb, gwx, hpg, hp, cw, gw, tmap) + [
                      pl.BlockSpec((ncb, 1, npair, pp, ns), lambda g, i: (tmap(i), g, 0, 0, 0)),
                      pl.BlockSpec((tb, gw), lambda g, i: (tmap(i), g))],
                  out_specs=[pl.BlockSpec((tb, gwx), lambda g, i: (tmap(i), g)),
                             pl.BlockSpec((1, tb, hpg), lambda g, i: (g, tmap(i), 0)),
                             pl.BlockSpec((1, SSD_CONV, cw), lambda g, i: (g, 0, 0)),
                             pl.BlockSpec((1, 1, cw), lambda g, i: (g, 0, 0)),
                             row, row, row,
                             pl.BlockSpec((1, gw), lambda g, i: (0, g))],
                  out_shape=[jax.ShapeDtypeStruct((T, G * gwx), _MXU), jax.ShapeDtypeStruct((G, T, hpg), f32),
                             jax.ShapeDtypeStruct((G, SSD_CONV, cw), f32), jax.ShapeDtypeStruct((G, 1, cw), f32),
                             rshape, rshape, rshape, jax.ShapeDtypeStruct((1, G * gw), f32)],
                  scratch=[pltpu.VMEM((npair, pp, ns), f32), pltpu.VMEM((npair, pp, ns), f32),
                           pltpu.VMEM((tb, cw), f32), pltpu.VMEM((tb, cw), f32), pltpu.VMEM((_HALO, cw), f32)],
                  sem=("parallel", "arbitrary"))(
        proj, proj, dt_raw, conv_w, conv_b, dtb_row, dtb_col, a_row, a_col, d_row, norm_w, s_chunks, dy)


def _interleave(w, parts, blk):
    lead = w.shape[:-1]
    n = w.shape[-1] // (parts * blk)
    nd = len(lead)
    w = w.reshape(*lead, parts, n, blk)
    w = jnp.swapaxes(w, nd, nd + 1)
    return w.reshape(*lead, parts * n * blk)


def _deinterleave(w, parts, blk):
    lead = w.shape[:-1]
    n = w.shape[-1] // (parts * blk)
    nd = len(lead)
    w = w.reshape(*lead, n, parts, blk)
    w = jnp.swapaxes(w, nd, nd + 1)
    return w.reshape(*lead, parts * n * blk)


def _even_perm(w, lw, hd, tc):
    return jnp.concatenate([_interleave(w[..., :2 * lw], 2, tc), _interleave(w[..., 2 * lw:], 4, hd)], axis=-1)


def _even_unperm(w, lw, hd, tc):
    return jnp.concatenate([_deinterleave(w[..., :2 * lw], 2, tc), _deinterleave(w[..., 2 * lw:], 4, hd)], axis=-1)


def _ssd_split(w, inner, offset):
    G, N = SSD_GROUPS, SSD_STATE
    lead = w.shape[:-1]
    xs = w[..., offset:offset + inner].reshape(*lead, G, inner // G)
    bm = w[..., offset + inner:offset + inner + G * N].reshape(*lead, G, N)
    cm = w[..., offset + inner + G * N:offset + inner + 2 * G * N].reshape(*lead, G, N)
    return xs, bm, cm


def _ssd_in_perm(w, inner, heads, hp):
    G = SSD_GROUPS
    lead = w.shape[:-1]
    z = w[..., :inner].reshape(*lead, G, inner // G)
    xs, bm, cm = _ssd_split(w, inner, inner)
    main = jnp.concatenate([z, xs, bm, cm], axis=-1).reshape(*lead, -1)
    dt = w[..., w.shape[-1] - heads:]
    dt = jnp.pad(dt, [(0, 0)] * len(lead) + [(0, hp - heads)])
    return main, dt


def _ssd_in_unperm(main, dt, inner, heads):
    G, N = SSD_GROUPS, SSD_STATE
    gw = inner // G
    lead = main.shape[:-1]
    m = main.reshape(*lead, G, 2 * gw + 2 * N)
    parts = [m[..., :gw], m[..., gw:2 * gw], m[..., 2 * gw:2 * gw + N], m[..., 2 * gw + N:]]
    return jnp.concatenate([p.reshape(*lead, -1) for p in parts] + [dt[..., :heads]], axis=-1)


def _ssd_conv_perm(w, inner):
    return jnp.concatenate(_ssd_split(w, inner, 0), axis=-1)


def _ssd_conv_unperm(w, inner):
    G, N = SSD_GROUPS, SSD_STATE
    gw = inner // G
    lead = w.shape[:-2]
    parts = [w[..., :gw], w[..., gw:gw + N], w[..., gw + N:]]
    return jnp.concatenate([p.reshape(*lead, -1) for p in parts], axis=-1)


def _blockdiag(w, tc):
    nb, b, _ = w.shape
    per = tc // b
    eye = jnp.eye(per, dtype=w.dtype)
    w = w.reshape(nb // per, per, b, b)
    return jnp.einsum("jbik,bc->jbick", w, eye).reshape(nb // per, tc, tc)


def _blockdiag_extract(g, b):
    nj, tc, _ = g.shape
    per = tc // b
    g = g.reshape(nj, per, b, per, b)
    return jnp.einsum("jbick,bc->jbik", g, jnp.eye(per, dtype=g.dtype)).reshape(nj * per, b, b)


def _lower_bounds(hg_lower_bounds):
    lbp = jax.nn.softmax(hg_lower_bounds.astype(f32), axis=0)
    return jnp.cumsum(lbp, axis=0) - lbp[0]


def _local_step(x, target, w):
    T, D = x.shape
    depth = w["norm_mix_w"].shape[0]
    lw = w["lru_conv_b"].shape[1]
    hw = w["hg_lower_bounds"].shape[1]
    hd = w["hg_norm_w"].shape[1]
    nh = hw // hd
    heads = w["ssd_dt_bias"].shape[1]
    inner = w["ssd_norm_w"].shape[1]
    G, N = SSD_GROUPS, SSD_STATE
    hpg = heads // G
    gw = inner // G
    hp = -(-heads // _LANES) * _LANES
    F = w["ffn_w_down"].shape[1]
    tcl = _pick(lw, (_TC_LRU, 128))
    tcf = _pick(F, (_TC_FFN, 256, 128))
    even_in = 2 * lw + 4 * hw
    n_even, n_odd = w["ev_w_in"].shape[0], w["ssd_w_in"].shape[0]

    ev_w_in = _even_perm(w["ev_w_in"], lw, hd, tcl)
    ssd_main, ssd_dtw = _ssd_in_perm(w["ssd_w_in"], inner, heads, hp)
    ssd_cw = jnp.swapaxes(_ssd_conv_perm(w["ssd_conv_w"], inner), 1, 2)
    ssd_cb = _ssd_conv_perm(w["ssd_conv_b"], inner)[:, :, None, :]
    ffn_up = _interleave(w["ffn_w_up"], 2, tcf)
    ffn_cw = _interleave(w["ffn_conv_w"], 2, tcf)
    ffn_cb = _interleave(w["ffn_conv_b"], 2, tcf)
    lbs_fn = jax.vjp(_lower_bounds, w["hg_lower_bounds"])
    lbs = lbs_fn[0]
    a_neg = -jnp.exp(w["ssd_a_log"].astype(f32))

    def rowcol(v):
        return v.reshape(G, 1, hpg), v.reshape(G, hpg, 1)

    saved = []
    for l in range(depth):
        s = {"x": x}
        h = _rmsnorm_fwd(x, w["norm_mix_w"][l], f"norm_mix_fwd_{l}")
        s["h"] = h
        if l % 2 == 0:
            e = l // 2
            proj = _mm(h, ev_w_in, b_lead=e, M=T, N=even_in, K=D, name=f"even_in_{l}")
            s["wr"] = _blockdiag(w["lru_w_r"][e], tcl).astype(_MXU)
            s["wi"] = _blockdiag(w["lru_w_i"][e], tcl).astype(_MXU)
            vecs = [w[k][e].reshape(1, lw) for k in ("lru_conv_b", "lru_b_r", "lru_b_i", "lru_lambda")]
            s["lru_args"] = (w["lru_conv_w"][e], vecs[0], s["wr"], vecs[1], s["wi"], vecs[2], vecs[3])
            ya, hst = _lru_fwd(proj, *s["lru_args"], lw, f"lru_fwd_{l}")
            s["hg_args"] = (lbs[e].reshape(1, hw), w["hg_norm_w"][e].reshape(1, hd))
            yb, stc = _hgrn_fwd(proj, *s["hg_args"], 2 * lw, nh, hd, f"hgrn_fwd_{l}")
            xm = _mm(ya, w["ev_w_out"], b_lead=e, M=T, N=D, K=lw, resid=x, name=f"even_out_a_{l}")
            xm = _mm(yb, w["ev_w_out"], b_lead=e, M=T, N=D, K=hw, b_off=(lw, 0), resid=xm, name=f"even_out_b_{l}")
            s.update(proj=proj, ya=ya, yb=yb, hst=hst, stc=stc)
        else:
            o = l // 2
            proj = _mm(h, ssd_main, b_lead=o, M=T, N=ssd_main.shape[-1], K=D, name=f"ssd_in_{l}")
            dtr = _mm(h, ssd_dtw, b_lead=o, M=T, N=hp, K=D, name=f"ssd_dt_{l}")
            dtb_r, dtb_c = rowcol(w["ssd_dt_bias"][o].astype(f32))
            a_r, a_c = rowcol(a_neg[o])
            d_r, _ = rowcol(w["ssd_d"][o].astype(f32))
            s["ssd_args"] = (ssd_cw[o], ssd_cb[o], dtb_r, dtb_c, a_r, a_c, d_r, w["ssd_norm_w"][o].reshape(1, inner))
            y, sc = _ssd_fwd(proj, dtr, *s["ssd_args"], hpg, f"ssd_fwd_{l}")
            xm = _mm(y, w["ssd_w_out"], b_lead=o, M=T, N=D, K=inner, resid=x, name=f"ssd_out_{l}")
            s.update(proj=proj, dtr=dtr, y=y, sc=sc)
        hn = _rmsnorm_fwd(xm, w["norm_ffn_w"][l], f"norm_ffn_fwd_{l}")
        up = _mm(hn, ffn_up, b_lead=l, M=T, N=2 * F, K=D, name=f"ffn_up_{l}")
        act = _ffn_mid_fwd(up, ffn_cw[l], ffn_cb[l].reshape(1, 2 * F), f"ffn_mid_fwd_{l}")
        x = _mm(act, w["ffn_w_down"], b_lead=l, M=T, N=D, K=F, resid=xm, name=f"ffn_down_{l}")
        s.update(xm=xm, hn=hn, up=up, act=act)
        saved.append(s)

    loss, dx, dxb, dwf = _loss_head(x, w["norm_final_w"], target)
    g = {k: [None] * v.shape[0] for k, v in w.items() if k != "norm_final_w"}
    g["norm_final_w"] = dwf.reshape(D)

    for l in reversed(range(depth)):
        s = saved[l]
        g["ffn_w_down"][l] = _mm(s["act"], dxb, ta=True, M=F, N=D, K=T, name=f"ffn_down_dw_{l}")
        dact = _mm(dxb, w["ffn_w_down"], b_lead=l, tb=True, M=T, N=F, K=D, name=f"ffn_down_dx_{l}")
        dup, dcw, dcb = _ffn_mid_bwd(s["up"], dact, ffn_cw[l], ffn_cb[l].reshape(1, 2 * F), f"ffn_mid_bwd_{l}")
        g["ffn_conv_w"][l] = _deinterleave(dcw, 2, tcf)
        g["ffn_conv_b"][l] = _deinterleave(dcb, 2, tcf).reshape(2 * F)
        g["ffn_w_up"][l] = _mm(s["hn"], dup, ta=True, M=D, N=2 * F, K=T, name=f"ffn_up_dw_{l}", tn=tcf,
                               tm=_pick(D, (2 * _MM_TM, _MM_TM, 512, 256, 128)),
                               out_perm=lambda j: lax.rem(j, 2) * (F // tcf) + lax.div(j, 2))
        dhn = _mm(dup, ffn_up, b_lead=l, tb=True, M=T, N=D, K=2 * F, name=f"ffn_up_dx_{l}")
        dxm, dxmb, dnf = _rmsnorm_bwd(dhn, s["xm"], w["norm_ffn_w"][l], dx, f"norm_ffn_bwd_{l}")
        g["norm_ffn_w"][l] = dnf.reshape(D)
        if l % 2 == 0:
            e = l // 2
            dy = _mm(dxmb, w["ev_w_out"], b_lead=e, tb=True, M=T, N=lw + hw, K=D, name=f"even_out_dx_{l}")
            dwo_a = _mm(s["ya"], dxmb, ta=True, M=lw, N=D, K=T, name=f"even_out_a_dw_{l}")
            dwo_b = _mm(s["yb"], dxmb, ta=True, M=hw, N=D, K=T, name=f"even_out_b_dw_{l}")
            g["ev_w_out"][e] = jnp.concatenate([dwo_a, dwo_b], axis=0)
            dproj, dcw, dcb, dwr, dbr, dwi, dbi, dlam = _lru_bwd(s["proj"], s["hst"], dy, *s["lru_args"], lw,
                                                                 f"lru_bwd_{l}")
            dproj, dlb, dnw = _hgrn_bwd(s["proj"], s["stc"], dy, dproj, *s["hg_args"], 2 * lw, lw, nh, hd,
                                        f"hgrn_bwd_{l}")
            g["lru_conv_w"][e] = dcw
            g["lru_conv_b"][e] = dcb.reshape(lw)
            g["lru_w_r"][e] = _blockdiag_extract(dwr, LRU_BLOCK)
            g["lru_w_i"][e] = _blockdiag_extract(dwi, LRU_BLOCK)
            g["lru_b_r"][e] = dbr.reshape(lw)
            g["lru_b_i"][e] = dbi.reshape(lw)
            g["lru_lambda"][e] = dlam.reshape(lw)
            g["hg_lower_bounds"][e] = dlb.reshape(hw)
            g["hg_norm_w"][e] = jnp.sum(dnw, axis=(0, 1))
            g["ev_w_in"][e] = _even_unperm(_mm(s["h"], dproj, ta=True, M=D, N=even_in, K=T, name=f"even_in_dw_{l}"),
                                           lw, hd, tcl)
            dh = _mm(dproj, ev_w_in, b_lead=e, tb=True, M=T, N=D, K=even_in, name=f"even_in_dx_{l}")
        else:
            o = l // 2
            dy = _mm(dxmb, w["ssd_w_out"], b_lead=o, tb=True, M=T, N=inner, K=D, name=f"ssd_out_dx_{l}")
            g["ssd_w_out"][o] = _mm(s["y"], dxmb, ta=True, M=inner, N=D, K=T, name=f"ssd_out_dw_{l}")
            dproj, ddt, dcw, dcb, ddtb, da, dd, dnw = _ssd_bwd(s["proj"], s["dtr"], s["sc"], dy, *s["ssd_args"], hpg,
                                                               f"ssd_bwd_{l}")
            ddt = jnp.pad(jnp.swapaxes(ddt, 0, 1).reshape(T, heads), ((0, 0), (0, hp - heads)))
            g["ssd_conv_w"][o] = _ssd_conv_unperm(jnp.swapaxes(dcw, 0, 1), inner)
            g["ssd_conv_b"][o] = _ssd_conv_unperm(dcb[:, 0, :], inner)
            g["ssd_dt_bias"][o] = ddtb.reshape(heads)
            g["ssd_a_log"][o] = da.reshape(heads) * a_neg[o]
            g["ssd_d"][o] = dd.reshape(heads)
            g["ssd_norm_w"][o] = dnw.reshape(inner)
            dmain = _mm(s["h"], dproj, ta=True, M=D, N=dproj.shape[1], K=T, name=f"ssd_in_dw_{l}")
            ddtw = _mm(s["h"], ddt, ta=True, M=D, N=hp, K=T, name=f"ssd_dt_dw_{l}")
            g["ssd_w_in"][o] = _ssd_in_unperm(dmain, ddtw, inner, heads)
            dh = _mm(dproj, ssd_main, b_lead=o, tb=True, M=T, N=D, K=dproj.shape[1], name=f"ssd_in_dx_{l}")
            dh = _mm(ddt, ssd_dtw, b_lead=o, tb=True, M=T, N=D, K=hp, resid=dh, name=f"ssd_dt_dx_{l}")
        dx, dxb, dnm = _rmsnorm_bwd(dh, s["x"], w["norm_mix_w"][l], dxm, f"norm_mix_bwd_{l}")
        g["norm_mix_w"][l] = dnm.reshape(D)

    grads = {k: (jnp.stack(v) if isinstance(v, list) else v) for k, v in g.items()}
    grads["hg_lower_bounds"] = lbs_fn[1](grads["hg_lower_bounds"])[0]
    return loss[0, 0], dx, grads


def _best_div(n, cap, mult=8):
    best = mult
    for d in range(mult, min(n, cap) + 1, mult):
        if n % d == 0:
            best = d
    assert n % best == 0, (n, best)
    return best


def _adamw(wv, gv, mv, vv, name):
    R, C = wv.shape
    tr = _best_div(R, max(8, _SLAB_BLOCK_ELEMS // C))

    def body(w_ref, g_ref, m_ref, v_ref, d_ref, nm_ref, nv_ref):
        g = g_ref[...]
        m = ADAM_B1 * m_ref[...] + (1.0 - ADAM_B1) * g
        v = ADAM_B2 * v_ref[...] + (1.0 - ADAM_B2) * (g * g)
        m_hat = m / (1.0 - ADAM_B1 ** ADAM_STEP)
        v_hat = v / (1.0 - ADAM_B2 ** ADAM_STEP)
        d_ref[...] = -ADAM_LR * (m_hat / (jnp.sqrt(v_hat) + ADAM_EPS) + ADAM_WD * w_ref[...])
        nm_ref[...] = m
        nv_ref[...] = v

    blk = pl.BlockSpec((tr, C), lambda i: (i, 0))
    shp = jax.ShapeDtypeStruct((R, C), f32)
    return _pcall(body, name=name, grid=(R // tr,), in_specs=[blk] * 4, out_specs=[blk] * 3, out_shape=[shp] * 3,
                  sem=("parallel",))(wv, gv, mv, vv)


_ANY = pl.BlockSpec(memory_space=pl.ANY)
_MESH = pl.DeviceIdType.MESH


def _where_am_i():
    x, y, c = lax.axis_index("x"), lax.axis_index("y"), lax.axis_index("c")
    chips = [(1 - x, y), (x, 1 - y), (1 - x, 1 - y)]
    return x, y, c, chips


def _rcopy(src, dst, send_sems, recv_sems, j, dev):
    return pltpu.make_async_remote_copy(src_ref=src, dst_ref=dst, send_sem=send_sems.at[j], recv_sem=recv_sems.at[j],
                                        device_id=dev, device_id_type=_MESH)


def _comm_call(body, name, out_shape, n_sems, n_in):
    scratch = [pltpu.SemaphoreType.DMA((n_sems,)), pltpu.SemaphoreType.DMA((n_sems,))]
    return pl.pallas_call(body, name=name, out_shape=out_shape, in_specs=[_ANY] * n_in, out_specs=_ANY,
                          scratch_shapes=scratch)


def _allgather_chips(shard, name):
    R, C = shard.shape
    rh = R // 2
    assert R % 2 == 0

    def body(x_ref, out_ref, send_sems, recv_sems):
        x, y, c, chips = _where_am_i()
        k = 2 * x + y
        sib = (x, y, 1 - c)

        def blk(kk, half):
            return out_ref.at[kk, pl.ds(half * rh, rh)]

        first =[_rcopy(x_ref.at[pl.ds(c * rh, rh)], blk(k, c), send_sems, recv_sems, j, (cx, cy, c))
                 for j, (cx, cy) in enumerate(chips)]
        for cp in first:
            cp.start()
        passed = []
        for j, (cx, cy) in enumerate(chips):
            kk = 2 * cx + cy
            _rcopy(blk(kk, c), blk(kk, c), send_sems, recv_sems, j, (cx, cy, c)).wait_recv()
            fwd = _rcopy(blk(kk, c), blk(kk, c), send_sems, recv_sems, 3 + j, sib)
            fwd.start()
            passed.append(fwd)
        for j, (cx, cy) in enumerate(chips):
            kk = 2 * cx + cy
            _rcopy(blk(kk, 1 - c), blk(kk, 1 - c), send_sems, recv_sems, 3 + j, sib).wait_recv()
        for cp in first + passed:
            cp.wait_send()

    return _comm_call(body, name, jax.ShapeDtypeStruct((4, R, C), shard.dtype), 6, 1)(shard)


def _d2d_exchange(g, name):
    _, R, C = g.shape
    rh = R // 2

    def body(g_ref, a_ref, send_sems, recv_sems):
        x, y, c, _ = _where_am_i()
        cp = _rcopy(g_ref.at[:, pl.ds((1 - c) * rh, rh)], a_ref, send_sems, recv_sems, 0, (x, y, 1 - c))
        cp.start()
        cp.wait()

    return _comm_call(body, name, jax.ShapeDtypeStruct((4, rh, C), g.dtype), 1, 1)(g)


def _ici_scatter(p, name):
    _, rh, C = p.shape

    def body(p_ref, b_ref, send_sems, recv_sems):
        x, y, c, chips = _where_am_i()
        cps = [_rcopy(p_ref.at[2 * cx + cy], b_ref.at[j], send_sems, recv_sems, j, (cx, cy, c))
               for j, (cx, cy) in enumerate(chips)]
        for cp in cps:
            cp.start()
        for cp in cps:
            cp.wait()

    return _comm_call(body, name, jax.ShapeDtypeStruct((3, rh, C), p.dtype), 3, 1)(p)


def _d2d_swap(q, name):
    R, C = q.shape
    rh = R // 2

    def body(q_ref, o_ref, send_sems, recv_sems):
        del q_ref
        x, y, c, _ = _where_am_i()
        mine = o_ref.at[pl.ds(c * rh, rh)]
        cp = _rcopy(mine, mine, send_sems, recv_sems, 0, (x, y, 1 - c))
        cp.start()
        cp.wait()

    return pl.pallas_call(body, name=name, out_shape=jax.ShapeDtypeStruct((R, C), q.dtype), in_specs=[_ANY],
                          out_specs=_ANY, input_output_aliases={0: 0},
                          scratch_shapes=[pltpu.SemaphoreType.DMA((1,)), pltpu.SemaphoreType.DMA((1,))])(q)


def _add_halves(g, a, c_idx, name):
    _, R, C = g.shape
    rh = R // 2
    tr = _best_div(rh, max(16, _SLAB_BLOCK_ELEMS // C), 16)
    nb = rh // tr

    def body(c_ref, g_ref, a_ref, o_ref):
        del c_ref
        o_ref[...] = (g_ref[...] + a_ref[...]).astype(o_ref.dtype)

    return _pcall(body, name=name, grid=(4, nb), prefetch=1,
                  in_specs=[pl.BlockSpec((1, tr, C), lambda j, i, c: (j, c[0] * nb + i, 0)),
                            pl.BlockSpec((1, tr, C), lambda j, i, c: (j, i, 0))],
                  out_specs=pl.BlockSpec((1, tr, C), lambda j, i, c: (j, i, 0)),
                  out_shape=jax.ShapeDtypeStruct((4, rh, C), _WIRE), sem=("parallel", "parallel"))(c_idx, g, a)


def _add_chips(p, b, ck_idx, name):
    _, rh, C = p.shape
    tr = _best_div(rh, max(16, _SLAB_BLOCK_ELEMS // C), 16)
    nb = rh // tr

    def body(ck_ref, p_ref, b_ref, o_ref):
        del ck_ref
        o_ref[...] = ((p_ref[0].astype(f32) + b_ref[0].astype(f32)) + b_ref[1].astype(f32)) + b_ref[2].astype(f32)

    return _pcall(body, name=name, grid=(nb,), prefetch=1,
                  in_specs=[pl.BlockSpec((1, tr, C), lambda i, ck: (ck[1], i, 0)),
                            pl.BlockSpec((3, tr, C), lambda i, ck: (0, i, 0))],
                  out_specs=pl.BlockSpec((tr, C), lambda i, ck: (ck[0] * nb + i, 0)),
                  out_shape=jax.ShapeDtypeStruct((2 * rh, C), f32), sem=("parallel",))(ck_idx, p, b)


def _reduce_scatter(g, c_idx, ck_idx, name):
    a = _d2d_exchange(g, name + "_d2d")
    p = _add_halves(g, a, c_idx, name + "_add2")
    b = _ici_scatter(p, name + "_ici")
    q = _add_chips(p, b, ck_idx, name + "_add4")
    return _d2d_swap(q, name + "_swap")


def _allreduce_small(buf, name):
    R, C = buf.shape

    def body(x_ref, o_ref, gat_ref, send_sems, recv_sems):
        x, y, c, _ = _where_am_i()
        me = 4 * x + 2 * y + c
        gat_ref[me] = x_ref[...]
        cps = []
        for f in range(1, 8):
            fx, fy, fc = (f >> 2) & 1, (f >> 1) & 1, f & 1
            peer = ((1 - x) if fx else x, (1 - y) if fy else y, (1 - c) if fc else c)
            cps.append(_rcopy(x_ref, gat_ref.at[me], send_sems, recv_sems, f - 1, peer))
        for cp in cps:
            cp.start()
        for cp in cps:
            cp.wait()
        acc = gat_ref[0]
        for d in range(1, 8):
            acc = acc + gat_ref[d]
        o_ref[...] = acc

    vm = pl.BlockSpec(memory_space=pltpu.VMEM)
    return pl.pallas_call(body, name=name, out_shape=jax.ShapeDtypeStruct((R, C), f32), in_specs=[vm], out_specs=vm,
                          scratch_shapes=[pltpu.VMEM((8, R, C), f32), pltpu.SemaphoreType.DMA((7,)),
                                          pltpu.SemaphoreType.DMA((7,))],
                          compiler_params=pltpu.CompilerParams(vmem_limit_bytes=_VMEM_LIMIT))(buf)


def _pack(arrs, cols=_COMM_COLS):
    flat = jnp.concatenate([a.reshape(-1).astype(f32) for a in arrs])
    per = 8 * cols
    n = -(-flat.shape[0] // per) * per
    return jnp.pad(flat, (0, n - flat.shape[0])).reshape(n // cols, cols)


def _unpack(buf, shapes):
    flat = buf.reshape(-1)
    out, off = [], 0
    for s in shapes:
        n = math.prod(s)
        out.append(flat[off:off + n].reshape(s))
        off += n
    return out


_WEIGHTS = ['norm_mix_w', 'norm_ffn_w', 'norm_final_w', 'ev_w_in', 'lru_conv_w', 'lru_conv_b', 'lru_w_r', 'lru_b_r',
            'lru_w_i', 'lru_b_i', 'lru_lambda', 'hg_lower_bounds', 'hg_norm_w', 'ev_w_out', 'ssd_w_in', 'ssd_conv_w',
            'ssd_conv_b', 'ssd_dt_bias', 'ssd_a_log', 'ssd_d', 'ssd_norm_w', 'ssd_w_out', 'ffn_w_up', 'ffn_conv_w',
            'ffn_conv_b', 'ffn_w_down']
_BIG = {'ev_w_in': 2, 'ev_w_out': 1, 'ssd_w_in': 2, 'ssd_w_out': 1, 'ffn_w_up': 2, 'ffn_w_down': 1}
_SMALL_SHARDED = {'lru_conv_w': 2, 'ssd_conv_w': 2, 'ssd_conv_b': 1, 'ssd_norm_w': 1, 'ffn_conv_w': 2}


def _full_shape(shape, ax):
    return tuple(4 * d if i == ax else d for i, d in enumerate(shape))


def kernel(x, norm_mix_w, norm_ffn_w, norm_final_w, ev_w_in, lru_conv_w, lru_conv_b, lru_w_r, lru_b_r, lru_w_i, lru_b_i, lru_lambda, hg_lower_bounds, hg_norm_w, ev_w_out, ssd_w_in, ssd_conv_w, ssd_conv_b, ssd_dt_bias, ssd_a_log, ssd_d, ssd_norm_w, ssd_w_out, ffn_w_up, ffn_conv_w, ffn_conv_b, ffn_w_down, loss_target, m_norm_mix_w, m_norm_ffn_w, m_norm_final_w, m_ev_w_in, m_lru_conv_w, m_lru_conv_b, m_lru_w_r, m_lru_b_r, m_lru_w_i, m_lru_b_i, m_lru_lambda, m_hg_lower_bounds, m_hg_norm_w, m_ev_w_out, m_ssd_w_in, m_ssd_conv_w, m_ssd_conv_b, m_ssd_dt_bias, m_ssd_a_log, m_ssd_d, m_ssd_norm_w, m_ssd_w_out, m_ffn_w_up, m_ffn_conv_w, m_ffn_conv_b, m_ffn_w_down, v_norm_mix_w, v_norm_ffn_w, v_norm_final_w, v_ev_w_in, v_lru_conv_w, v_lru_conv_b, v_lru_w_r, v_lru_b_r, v_lru_w_i, v_lru_b_i, v_lru_lambda, v_hg_lower_bounds, v_hg_norm_w, v_ev_w_out, v_ssd_w_in, v_ssd_conv_w, v_ssd_conv_b, v_ssd_dt_bias, v_ssd_a_log, v_ssd_d, v_ssd_norm_w, v_ssd_w_out, v_ffn_w_up, v_ffn_conv_w, v_ffn_conv_b, v_ffn_w_down):
    given = dict(locals())
    w = {n: given[n] for n in _WEIGHTS}
    m = {n: given["m_" + n] for n in _WEIGHTS}
    v = {n: given["v_" + n] for n in _WEIGHTS}
    cx, cy, cc = lax.axis_index("x"), lax.axis_index("y"), lax.axis_index("c")
    chip = 2 * cx + cy
    c_idx = jnp.reshape(cc, (1,)).astype(jnp.int32)
    ck_idx = jnp.stack([cc, chip]).astype(jnp.int32)

    def rows2d(a):
        return a.reshape(-1, a.shape[-1])

    full = dict(w)
    for n, ax in _BIG.items():
        own = rows2d(w[n].astype(_MXU))
        gat = _allgather_chips(own, "gather_" + n)
        gat = lax.dynamic_update_index_in_dim(gat, own, chip, 0)
        blocks = gat.reshape((4,) + w[n].shape)
        full[n] = jnp.moveaxis(blocks, 0, ax).reshape(_full_shape(w[n].shape, ax))
    placed = []
    for n, ax in _SMALL_SHARDED.items():
        z = jnp.zeros(_full_shape(w[n].shape, ax), f32)
        z = lax.dynamic_update_slice_in_dim(z, w[n].astype(f32), chip * w[n].shape[ax], ax)
        placed.append(jnp.where(cc == 0, z, 0.0))
    small_full = _unpack(_allreduce_small(_pack(placed), "gather_small"), [p.shape for p in placed])
    for n, a in zip(_SMALL_SHARDED, small_full):
        full[n] = a

    loss_part, dx, grads = _local_step(x[0], loss_target[0], full)

    small = [n for n in _WEIGHTS if n not in _BIG]
    red = _allreduce_small(_pack([loss_part.reshape(1)] + [grads[n] for n in small]), "allreduce_small")
    red = _unpack(red, [(1,)] + [grads[n].shape for n in small])
    loss = red[0][0]
    gsum = {}
    for n, a in zip(small, red[1:]):
        if n in _SMALL_SHARDED:
            ax = _SMALL_SHARDED[n]
            a = lax.dynamic_slice_in_dim(a, chip * w[n].shape[ax], w[n].shape[ax], ax)
        gsum[n] = a
    outs = _adamw(_pack([w[n] for n in small]), _pack([gsum[n] for n in small]), _pack([m[n] for n in small]),
                  _pack([v[n] for n in small]), "adamw_small")
    shapes = [w[n].shape for n in small]
    delta, new_m, new_v = ({n: a for n, a in zip(small, _unpack(o, shapes))} for o in outs)

    for n, ax in _BIG.items():
        shp = w[n].shape
        g4 = jnp.moveaxis(grads[n].reshape(shp[:ax] + (4, shp[ax]) + shp[ax + 1:]), ax, 0)
        gs = _reduce_scatter(g4.reshape(4, -1, shp[-1]), c_idx, ck_idx, "rs_" + n)
        d_, m_, v_ = _adamw(rows2d(w[n]), gs, rows2d(m[n]), rows2d(v[n]), "adamw_" + n)
        gsum[n] = gs.reshape(shp)
        delta[n], new_m[n], new_v[n] = d_.reshape(shp), m_.reshape(shp), v_.reshape(shp)

    return (loss, dx[None], *[gsum[n] for n in _WEIGHTS], *[delta[n] for n in _WEIGHTS],
            *[new_m[n] for n in _WEIGHTS], *[new_v[n] for n in _WEIGHTS])
```

```python
import functools
import math

import jax
import jax.numpy as jnp
from jax import lax
from jax.experimental import pallas as pl
from jax.experimental.pallas import tpu as pltpu

f32 = jnp.float32
bf16 = jnp.bfloat16
_MXU = jnp.bfloat16
_WIRE = jnp.bfloat16
_HI = lax.Precision.HIGHEST

EPS = 1e-6
LRU_C = 8.0
LRU_BLOCK = 64
LRU_CONV = 4
HG_CHUNK = 64
SSD_GROUPS = 8
SSD_STATE = 128
SSD_CONV = 4
SSD_CHUNK = 64
FFN_CONV = 3
ADAM_LR, ADAM_B1, ADAM_B2, ADAM_EPS, ADAM_WD, ADAM_STEP = 0.001, 0.9, 0.999, 1e-08, 0.01, 10

_VMEM_LIMIT = 56 * 1024 * 1024
_LANES = 128
_HALO = 8

_TB_NORM = 512
_TB_FFN = 512
_TB_LRU = 256
_TB_HG = 512
_TB_SSD = 256
_TC_FFN = 512
_TC_LRU = 256
_FFN_STRIP, _FFN_CW = 16, 256
_HG_UNROLL = 4
_MM_TM, _MM_TN = 1024, 1024
_MM_TK_ONE = 2048
_MM_TK_MAX = 2816
_COMM_COLS = 1024
_SLAB_BLOCK_ELEMS = 384 * 1024


def _pick(n, prefs):
    for p in prefs:
        if p <= n and n % p == 0:
            return p
    return n


def _pcall(body, *, name, grid, in_specs, out_specs, out_shape, scratch=(), sem=None, aliases=None, prefetch=0):
    params = pltpu.CompilerParams(dimension_semantics=sem, vmem_limit_bytes=_VMEM_LIMIT)
    if prefetch:
        gs = pltpu.PrefetchScalarGridSpec(num_scalar_prefetch=prefetch, grid=grid, in_specs=in_specs,
                                          out_specs=out_specs, scratch_shapes=list(scratch))
        return pl.pallas_call(body, name=name, grid_spec=gs, out_shape=out_shape,
                              input_output_aliases=aliases or {}, compiler_params=params)
    return pl.pallas_call(body, name=name, grid=grid, in_specs=in_specs, out_specs=out_specs, out_shape=out_shape,
                          scratch_shapes=list(scratch), input_output_aliases=aliases or {}, compiler_params=params)


def _sig(x):
    return jax.nn.sigmoid(x)


def _softplus(x):
    return jnp.maximum(x, 0.0) + jnp.log(1.0 + jnp.exp(-jnp.abs(x)))


def _neg_expm1(x):
    series = -x * (1.0 + x * (0.5 + x * (1.0 / 6.0 + x * (1.0 / 24.0 + x * (1.0 / 120.0)))))
    return jnp.where(x > -0.05, series, 1.0 - jnp.exp(x))


_GELU_C = math.sqrt(2.0 / math.pi)


def _gelu_and_grad(x):
    inner = _GELU_C * (x + 0.044715 * x * x * x)
    t = jnp.tanh(inner)
    g = 0.5 * x * (1.0 + t)
    dg = 0.5 * (1.0 + t) + 0.5 * x * (1.0 - t * t) * _GELU_C * (1.0 + 3.0 * 0.044715 * x * x)
    return g, dg


def _silu_and_grad(x):
    s = _sig(x)
    return x * s, s * (1.0 + x * (1.0 - s))


def _dot(a, b, dims=(((1,), (0,)), ((), ()))):
    return lax.dot_general(a.astype(_MXU), b.astype(_MXU), dims, preferred_element_type=f32)


def _dot_nt(a, b):
    return _dot(a, b, (((1,), (1,)), ((), ())))


def _dot_tn(a, b):
    return _dot(a, b, (((0,), (0,)), ((), ())))


def _dot_hi(a, b):
    return jnp.dot(a, b, preferred_element_type=f32, precision=_HI)


def _tri(n, lower=True):
    r = lax.broadcasted_iota(jnp.int32, (n, n), 0)
    c = lax.broadcasted_iota(jnp.int32, (n, n), 1)
    return (r >= c) if lower else (r <= c)


def _shift_down(cur, prev, s):
    if s == 0:
        return cur
    n = cur.shape[0]
    ext = jnp.concatenate([prev, cur], axis=0)
    return pltpu.roll(ext, s, axis=0)[_HALO:_HALO + n]


def _shift_up(cur, nxt, s):
    if s == 0:
        return cur
    n = cur.shape[0]
    ext = jnp.concatenate([cur, nxt], axis=0)
    return pltpu.roll(ext, n + _HALO - s, axis=0)[0:n]


def _conv_fwd(cur, prev, w, b, k):
    y = b + w[k - 1] * cur
    for j in range(k - 1):
        y = y + w[j] * _shift_down(cur, prev, k - 1 - j)
    return y


def _conv_bwd(dy, nxt_dy, cur, prev, w, k):
    del prev
    dx = w[k - 1] * dy
    dws = []
    for j in range(k - 1):
        up = _shift_up(dy, nxt_dy, k - 1 - j)
        dx = dx + w[j] * up
        dws.append(jnp.sum(up * cur, axis=0, keepdims=True))
    dws.append(jnp.sum(dy * cur, axis=0, keepdims=True))
    return dx, dws, jnp.sum(dy, axis=0, keepdims=True)


def _rows(ref, k, cols=None):
    if cols is None:
        return [ref[j:j + 1, :] for j in range(k)]
    return [ref[j:j + 1, cols[0]:cols[1]] for j in range(k)]


def _add_rows(ref, rows, cols=None):
    for j, r in enumerate(rows):
        if cols is None:
            ref[j:j + 1, :] += r
        else:
            ref[j:j + 1, cols[0]:cols[1]] += r


def _mm(a, b, *, M, N, K, ta=False, tb=False, a_off=(0, 0), b_off=(0, 0), resid=None, out_dtype=f32, name,
        b_lead=None, tm=None, tn=None, out_perm=None):
    tm = tm or _pick(M, (_MM_TM, 512, 256, 128))
    tn = tn or _pick(N, (_MM_TN, 512, 256, 128))
    narrow = a.dtype.itemsize <= 2 and b.dtype.itemsize <= 2
    if not narrow:
        tk = _pick(K, (512, 256, 128))
    elif K <= _MM_TK_ONE:
        tk = K
    else:
        tk = max(d for d in range(_LANES, _MM_TK_MAX + 1, _LANES) if K % d == 0)
    nk = K // tk
    if ta:
        assert a_off[0] % tk == 0 and a_off[1] % tm == 0
        a_spec = pl.BlockSpec((tk, tm), lambda i, j, k: (k + a_off[0] // tk, i + a_off[1] // tm))
        a_dim = 0
    else:
        assert a_off[0] % tm == 0 and a_off[1] % tk == 0
        a_spec = pl.BlockSpec((tm, tk), lambda i, j, k: (i + a_off[0] // tm, k + a_off[1] // tk))
        a_dim = 1
    lead = () if b_lead is None else (None,)
    pre = (lambda t: t) if b_lead is None else (lambda t: (b_lead,) + t)
    if tb:
        assert b_off[0] % tn == 0 and b_off[1] % tk == 0
        b_spec = pl.BlockSpec(lead + (tn, tk), lambda i, j, k: pre((j + b_off[0] // tn, k + b_off[1] // tk)))
        b_dim = 1
    else:
        assert b_off[0] % tk == 0 and b_off[1] % tn == 0
        b_spec = pl.BlockSpec(lead + (tk, tn), lambda i, j, k: pre((k + b_off[0] // tk, j + b_off[1] // tn)))
        b_dim = 0
    dims = (((a_dim,), (b_dim,)), ((), ()))
    out_col = (lambda j: j) if out_perm is None else out_perm
    has_resid = resid is not None

    def body_one(*refs):
        a_ref, b_ref = refs[0], refs[1]
        o_ref = refs[-1]
        r = _dot(a_ref[...], b_ref[...], dims)
        if has_resid:
            r = r + refs[2][...].astype(f32)
        o_ref[...] = r.astype(out_dtype)

    def body_acc(*refs):
        a_ref, b_ref = refs[0], refs[1]
        o_ref, acc_ref = refs[-2], refs[-1]
        kk = pl.program_id(2)

        @pl.when(kk == 0)
        def _():
            acc_ref[...] = _dot(a_ref[...], b_ref[...], dims)

        @pl.when(kk > 0)
        def _():
            acc_ref[...] += _dot(a_ref[...], b_ref[...], dims)

        @pl.when(kk == nk - 1)
        def _():
            r = acc_ref[...]
            if has_resid:
                r = r + refs[2][...].astype(f32)
            o_ref[...] = r.astype(out_dtype)

    in_specs = [a_spec, b_spec]
    args = [a, b]
    if has_resid:
        in_specs.append(pl.BlockSpec((tm, tn), lambda i, j, k: (i, j)))
        args.append(resid)
    return _pcall(body_one if nk == 1 else body_acc, name=name, grid=(M // tm, N // tn, nk), in_specs=in_specs,
                  out_specs=pl.BlockSpec((tm, tn), lambda i, j, k: (i, out_col(j))),
                  out_shape=jax.ShapeDtypeStruct((M, N), out_dtype),
                  scratch=[] if nk == 1 else [pltpu.VMEM((tm, tn), f32)],
                  sem=("parallel", "parallel", "arbitrary"))(*args)


def _rmsnorm_fwd(x, w, name):
    T, D = x.shape
    tb = _pick(T, (_TB_NORM, 256, 128, 64))

    def body(x_ref, w_ref, o_ref):
        xv = x_ref[...]
        rstd = lax.rsqrt(jnp.mean(xv * xv, axis=-1, keepdims=True) + EPS)
        o_ref[...] = (xv * rstd * w_ref[...]).astype(o_ref.dtype)

    return _pcall(body, name=name, grid=(T // tb,),
                  in_specs=[pl.BlockSpec((tb, D), lambda i: (i, 0)), pl.BlockSpec((1, D), lambda i: (0, 0))],
                  out_specs=pl.BlockSpec((tb, D), lambda i: (i, 0)),
                  out_shape=jax.ShapeDtypeStruct((T, D), _MXU), sem=("parallel",))(x, w.reshape(1, D))


def _rmsnorm_bwd(dh, x, w, dres, name):
    T, D = x.shape
    tb = _pick(T, (_TB_NORM, 256, 128, 64))

    def body(dh_ref, x_ref, w_ref, dres_ref, dx_ref, dxm_ref, dw_ref):
        xv = x_ref[...]
        rstd = lax.rsqrt(jnp.mean(xv * xv, axis=-1, keepdims=True) + EPS)
        xhat = xv * rstd
        dhv = dh_ref[...]
        g = dhv * w_ref[...]
        dx = dres_ref[...] + rstd * (g - xhat * jnp.mean(g * xhat, axis=-1, keepdims=True))
        dx_ref[...] = dx
        dxm_ref[...] = dx.astype(dxm_ref.dtype)

        @pl.when(pl.program_id(0) == 0)
        def _():
            dw_ref[...] = jnp.zeros_like(dw_ref)

        dw_ref[...] += jnp.sum(dhv * xhat, axis=0, keepdims=True)

    row = pl.BlockSpec((tb, D), lambda i: (i, 0))
    vec = pl.BlockSpec((1, D), lambda i: (0, 0))
    return _pcall(body, name=name, grid=(T // tb,), in_specs=[row, row, vec, row], out_specs=[row, row, vec],
                  out_shape=[jax.ShapeDtypeStruct((T, D), f32), jax.ShapeDtypeStruct((T, D), _MXU),
                             jax.ShapeDtypeStruct((1, D), f32)],
                  sem=("arbitrary",))(dh, x, w.reshape(1, D), dres)


def _loss_head(x, w, target):
    T, D = x.shape
    tb = _pick(T, (_TB_NORM, 256, 128, 64))

    def body(x_ref, w_ref, t_ref, loss_ref, dx_ref, dxm_ref, dw_ref):
        xv = x_ref[...]
        rstd = lax.rsqrt(jnp.mean(xv * xv, axis=-1, keepdims=True) + EPS)
        xhat = xv * rstd
        err = xhat * w_ref[...] - t_ref[...]
        dy = err * (1.0 / D)
        g = dy * w_ref[...]
        dx = rstd * (g - xhat * jnp.mean(g * xhat, axis=-1, keepdims=True))
        dx_ref[...] = dx
        dxm_ref[...] = dx.astype(dxm_ref.dtype)

        @pl.when(pl.program_id(0) == 0)
        def _():
            dw_ref[...] = jnp.zeros_like(dw_ref)
            loss_ref[...] = jnp.zeros_like(loss_ref)

        dw_ref[...] += jnp.sum(dy * xhat, axis=0, keepdims=True)
        part = 0.5 * jnp.sum(jnp.sum(err * err, axis=-1, keepdims=True) * (1.0 / D), axis=0, keepdims=True)
        loss_ref[...] += jnp.broadcast_to(part, loss_ref.shape)

    row = pl.BlockSpec((tb, D), lambda i: (i, 0))
    vec = pl.BlockSpec((1, D), lambda i: (0, 0))
    return _pcall(body, name="loss_head", grid=(T // tb,), in_specs=[row, vec, row],
                  out_specs=[pl.BlockSpec((1, _LANES), lambda i: (0, 0)), row, row, vec],
                  out_shape=[jax.ShapeDtypeStruct((1, _LANES), f32), jax.ShapeDtypeStruct((T, D), f32),
                             jax.ShapeDtypeStruct((T, D), _MXU), jax.ShapeDtypeStruct((1, D), f32)],
                  sem=("arbitrary",))(x, w.reshape(1, D), target)


def _halo_spec(tb, width, col_fn):
    r = tb // _HALO
    return pl.BlockSpec((_HALO, width), lambda j, i: (jnp.maximum(i * r - 1, 0), col_fn(j)))


def _ffn_mid_fwd(up, conv_w, conv_b, name):
    T, F2 = up.shape
    tc = _pick(F2 // 2, (_TC_FFN, 256, 128))
    tb = _pick(T, (_TB_FFN, 256, 128, 64))
    nj = F2 // (2 * tc)

    def body(cur_ref, prev_ref, w_ref, b_ref, o_ref):
        cur = cur_ref[...]
        prev = jnp.where(pl.program_id(1) > 0, prev_ref[...], 0.0)
        u = _conv_fwd(cur, prev, _rows(w_ref, FFN_CONV), b_ref[...], FFN_CONV)
        gate, val = u[:, :tc], u[:, tc:]
        o_ref[...] = (gate * _sig(gate) * val).astype(o_ref.dtype)

    return _pcall(body, name=name, grid=(nj, T // tb),
                  in_specs=[pl.BlockSpec((tb, 2 * tc), lambda j, i: (i, j)), _halo_spec(tb, 2 * tc, lambda j: j),
                            pl.BlockSpec((FFN_CONV, 2 * tc), lambda j, i: (0, j)),
                            pl.BlockSpec((1, 2 * tc), lambda j, i: (0, j))],
                  out_specs=pl.BlockSpec((tb, tc), lambda j, i: (i, j)),
                  out_shape=jax.ShapeDtypeStruct((T, F2 // 2), _MXU), sem=("parallel", "parallel"))(
        up, up, conv_w, conv_b)


def _ffn_mid_bwd(up, dact, conv_w, conv_b, name):
    T, F2 = up.shape
    tc = _pick(F2 // 2, (_TC_FFN, 256, 128))
    tb = _pick(T, (_TB_FFN, 256, 128, 64))
    nj, nt = F2 // (2 * tc), T // tb
    r = tb // _HALO

    cw = _pick(tc, (_FFN_CW, 128))
    rs = _pick(tb, (_FFN_STRIP, 8))
    k = FFN_CONV

    def fold(x):
        acc = x[0:_HALO]
        for r0 in range(_HALO, x.shape[0], _HALO):
            acc = acc + x[r0:r0 + _HALO]
        return acc

    def conv_t(dy, nxt, x, w):
        dx = w[k - 1] * dy
        sums = []
        for j in range(k - 1):
            up = _shift_up(dy, nxt, k - 1 - j)
            dx = dx + w[j] * up
            sums.append(fold(up * x))
        return dx, sums + [fold(dy * x), fold(dy)]

    def body(cur_ref, prev_ref, da_ref, w_ref, b_ref, dup_ref, dw_ref, db_ref, carry_ref):
        i = pl.program_id(1)
        not_first = nt - 1 - i > 0

        @pl.when(i == 0)
        def _():
            carry_ref[...] = jnp.zeros_like(carry_ref)
            dw_ref[...] = jnp.zeros_like(dw_ref)
            db_ref[...] = jnp.zeros_like(db_ref)

        ns = tb // rs
        for g in range(tc // cw):
            gc, vc = slice(g * cw, (g + 1) * cw), slice(tc + g * cw, tc + (g + 1) * cw)
            wg, wv = _rows(w_ref, k, (gc.start, gc.stop)), _rows(w_ref, k, (vc.start, vc.stop))
            bg, bv = b_ref[:, gc], b_ref[:, vc]
            pg = jnp.where(not_first, prev_ref[:, gc], 0.0)
            pv = jnp.where(not_first, prev_ref[:, vc], 0.0)

            def strip(ss, acc):
                s = ns - 1 - ss
                r0 = pl.multiple_of(s * rs, rs)
                rows = pl.ds(r0, rs)
                halo = pl.ds(pl.multiple_of(jnp.maximum(r0 - _HALO, 0), _HALO), _HALO)
                xg, xv = cur_ref[rows, gc], cur_ref[rows, vc]
                gate = _conv_fwd(xg, jnp.where(s > 0, cur_ref[halo, gc], pg), wg, bg, k)
                val = _conv_fwd(xv, jnp.where(s > 0, cur_ref[halo, vc], pv), wv, bv, k)
                da = da_ref[rows, gc]
                sil, dsil = _silu_and_grad(gate)
                dug, duv = da * val * dsil, da * sil
                dxg, sg = conv_t(dug, carry_ref[:, gc], xg, wg)
                dxv, sv = conv_t(duv, carry_ref[:, vc], xv, wv)
                dup_ref[rows, gc] = dxg.astype(dup_ref.dtype)
                dup_ref[rows, vc] = dxv.astype(dup_ref.dtype)
                carry_ref[:, gc] = dug[0:_HALO]
                carry_ref[:, vc] = duv[0:_HALO]
                return tuple(a + b for a, b in zip(acc, sg + sv))

            zero = jnp.zeros((_HALO, cw), f32)
            acc = lax.fori_loop(0, ns, strip, (zero,) * (2 * (k + 1)))
            tot = [jnp.sum(a, axis=0, keepdims=True) for a in acc]
            _add_rows(dw_ref, tot[0:k], (gc.start, gc.stop))
            _add_rows(dw_ref, tot[k + 1:2 * k + 1], (vc.start, vc.stop))
            db_ref[:, gc] += tot[k]
            db_ref[:, vc] += tot[2 * k + 1]

    return _pcall(body, name=name, grid=(nj, nt),
                  in_specs=[pl.BlockSpec((tb, 2 * tc), lambda j, i: (nt - 1 - i, j)),
                            pl.BlockSpec((_HALO, 2 * tc), lambda j, i: (jnp.maximum((nt - 1 - i) * r - 1, 0), j)),
                            pl.BlockSpec((tb, tc), lambda j, i: (nt - 1 - i, j)),
                            pl.BlockSpec((FFN_CONV, 2 * tc), lambda j, i: (0, j)),
                            pl.BlockSpec((1, 2 * tc), lambda j, i: (0, j))],
                  out_specs=[pl.BlockSpec((tb, 2 * tc), lambda j, i: (nt - 1 - i, j)),
                             pl.BlockSpec((FFN_CONV, 2 * tc), lambda j, i: (0, j)),
                             pl.BlockSpec((1, 2 * tc), lambda j, i: (0, j))],
                  out_shape=[jax.ShapeDtypeStruct((T, F2), _MXU), jax.ShapeDtypeStruct((FFN_CONV, F2), f32),
                             jax.ShapeDtypeStruct((1, F2), f32)],
                  scratch=[pltpu.VMEM((_HALO, 2 * tc), f32)], sem=("parallel", "arbitrary"))(
        up, up, dact, conv_w, conv_b)


def _scan_fwd(a, u):
    n = a.shape[0]
    row = lax.broadcasted_iota(jnp.int32, a.shape, 0)
    d = 1
    while d < n:
        a_s = jnp.where(row >= d, pltpu.roll(a, d, axis=0), 1.0)
        u_s = jnp.where(row >= d, pltpu.roll(u, d, axis=0), 0.0)
        u = u + a * u_s
        a = a * a_s
        d *= 2
    return a, u


def _scan_bwd(m, v):
    n = m.shape[0]
    row = lax.broadcasted_iota(jnp.int32, m.shape, 0)
    d = 1
    while d < n:
        ok = row + d < n
        m_s = jnp.where(ok, pltpu.roll(m, n - d, axis=0), 1.0)
        v_s = jnp.where(ok, pltpu.roll(v, n - d, axis=0), 0.0)
        v = v + m * v_s
        m = m * m_s
        d *= 2
    return v


def _lru_gates(xc, wr, br, wi, bi, lam):
    r = _sig(_dot(xc, wr) + br)
    gi = _sig(_dot(xc, wi) + bi)
    sp = _softplus(-lam)
    log_a = -LRU_C * r * sp
    a = jnp.exp(log_a)
    mult = jnp.sqrt(_neg_expm1(2.0 * log_a))
    return r, gi, sp, a, mult


def _lru_fwd(proj, conv_w, conv_b, wr, br, wi, bi, lam, lw, name):
    T = proj.shape[0]
    tc = _pick(lw, (_TC_LRU, 128))
    tb = _pick(T, (_TB_LRU, 128, 64))
    nj = lw // tc

    def body(cur_ref, prev_ref, cw_ref, cb_ref, wr_ref, br_ref, wi_ref, bi_ref, lam_ref, ya_ref, h_ref, carry_ref):
        i = pl.program_id(1)

        @pl.when(i == 0)
        def _():
            carry_ref[...] = jnp.zeros_like(carry_ref)

        xa = cur_ref[:, :tc]
        ga = cur_ref[:, tc:]
        prev = jnp.where(i > 0, prev_ref[:, :tc], 0.0)
        xc = _conv_fwd(xa, prev, _rows(cw_ref, LRU_CONV), cb_ref[...], LRU_CONV)
        r, gi, sp, a, mult = _lru_gates(xc, wr_ref[0], br_ref[...], wi_ref[0], bi_ref[...], lam_ref[...])
        acum, h = _scan_fwd(a, mult * gi * xc)
        h = h + acum * carry_ref[0:1, :]
        h_ref[...] = h
        gel, _ = _gelu_and_grad(ga)
        ya_ref[...] = (h * gel).astype(ya_ref.dtype)
        carry_ref[...] = pltpu.roll(h, 1, axis=0)[0:_HALO]

    vec = pl.BlockSpec((1, tc), lambda j, i: (0, j))
    mat = pl.BlockSpec((1, tc, tc), lambda j, i: (j, 0, 0))
    out = pl.BlockSpec((tb, tc), lambda j, i: (i, j))
    return _pcall(body, name=name, grid=(nj, T // tb),
                  in_specs=[pl.BlockSpec((tb, 2 * tc), lambda j, i: (i, j)), _halo_spec(tb, 2 * tc, lambda j: j),
                            pl.BlockSpec((LRU_CONV, tc), lambda j, i: (0, j)), vec, mat, vec, mat, vec, vec],
                  out_specs=[out, out],
                  out_shape=[jax.ShapeDtypeStruct((T, lw), _MXU), jax.ShapeDtypeStruct((T, lw), f32)],
                  scratch=[pltpu.VMEM((_HALO, tc), f32)], sem=("parallel", "arbitrary"))(
        proj, proj, conv_w, conv_b, wr, br, wi, bi, lam)


def _lru_bwd(proj, h, dy, conv_w, conv_b, wr, br, wi, bi, lam, lw, name):
    T, width = proj.shape
    tc = _pick(lw, (_TC_LRU, 128))
    tb = _pick(T, (_TB_LRU, 128, 64))
    nj, nt = lw // tc, T // tb
    rr = tb // _HALO

    def body(cur_ref, prev_ref, h_ref, hp_ref, dy_ref, cw_ref, cb_ref, wr_ref, br_ref, wi_ref, bi_ref, lam_ref,
             dp_ref, dcw_ref, dcb_ref, dwr_ref, dbr_ref, dwi_ref, dbi_ref, dlam_ref, cl_ref, cx_ref):
        i = pl.program_id(1)
        ti = nt - 1 - i

        @pl.when(i == 0)
        def _():
            for ref in (cl_ref, cx_ref, dcw_ref, dcb_ref, dwr_ref, dbr_ref, dwi_ref, dbi_ref, dlam_ref):
                ref[...] = jnp.zeros_like(ref)

        xa = cur_ref[:, :tc]
        ga = cur_ref[:, tc:]
        prev = jnp.where(ti > 0, prev_ref[:, :tc], 0.0)
        cw = _rows(cw_ref, LRU_CONV)
        xc = _conv_fwd(xa, prev, cw, cb_ref[...], LRU_CONV)
        wrv, wiv, lam_v = wr_ref[0], wi_ref[0], lam_ref[...]
        r, gi, sp, a, mult = _lru_gates(xc, wrv, br_ref[...], wiv, bi_ref[...], lam_v)
        hcur = h_ref[...]
        hprev = _shift_down(hcur, jnp.where(ti > 0, hp_ref[...], 0.0), 1)
        gel, dgel = _gelu_and_grad(ga)
        dya = dy_ref[...]
        dga = dya * hcur * dgel
        row = lax.broadcasted_iota(jnp.int32, a.shape, 0)
        v = dya * gel + jnp.where(row == tb - 1, cl_ref[0:1, :], 0.0)
        m = jnp.where(row < tb - 1, pltpu.roll(a, tb - 1, axis=0), 0.0)
        lamb = _scan_bwd(m, v)
        cl_ref[...] = (a * lamb)[0:_HALO]
        da = lamb * hprev
        dmult = lamb * gi * xc
        dgi = lamb * mult * xc
        dxc = lamb * mult * gi
        dlog_a = da * a - dmult * (a * a) / mult
        dr = dlog_a * (-LRU_C * sp)
        dlam_ref[...] += jnp.sum(dlog_a * (-LRU_C) * r, axis=0, keepdims=True) * (-_sig(-lam_v))
        dpr = dr * r * (1.0 - r)
        dpi = dgi * gi * (1.0 - gi)
        dbr_ref[...] += jnp.sum(dpr, axis=0, keepdims=True)
        dbi_ref[...] += jnp.sum(dpi, axis=0, keepdims=True)
        dwr_ref[0] += _dot_tn(xc, dpr)
        dwi_ref[0] += _dot_tn(xc, dpi)
        dxc = dxc + _dot_nt(dpr, wrv) + _dot_nt(dpi, wiv)
        dxa, dws, db = _conv_bwd(dxc, cx_ref[...], xa, prev, cw, LRU_CONV)
        cx_ref[...] = dxc[0:_HALO]
        _add_rows(dcw_ref, dws)
        dcb_ref[...] += db
        dp_ref[...] = jnp.concatenate([dxa, dga], axis=1).astype(dp_ref.dtype)

    vec = pl.BlockSpec((1, tc), lambda j, i: (0, j))
    mat = pl.BlockSpec((1, tc, tc), lambda j, i: (j, 0, 0))
    cwspec = pl.BlockSpec((LRU_CONV, tc), lambda j, i: (0, j))
    blk = pl.BlockSpec((tb, tc), lambda j, i: (nt - 1 - i, j))
    vshape = jax.ShapeDtypeStruct((1, lw), f32)
    mshape = jax.ShapeDtypeStruct((nj, tc, tc), f32)
    return _pcall(body, name=name, grid=(nj, nt),
                  in_specs=[pl.BlockSpec((tb, 2 * tc), lambda j, i: (nt - 1 - i, j)),
                            pl.BlockSpec((_HALO, 2 * tc), lambda j, i: (jnp.maximum((nt - 1 - i) * rr - 1, 0), j)),
                            blk,
                            pl.BlockSpec((_HALO, tc), lambda j, i: (jnp.maximum((nt - 1 - i) * rr - 1, 0), j)),
                            blk, cwspec, vec, mat, vec, mat, vec, vec],
                  out_specs=[pl.BlockSpec((tb, 2 * tc), lambda j, i: (nt - 1 - i, j)), cwspec, vec, mat, vec, mat, vec,
                             vec],
                  out_shape=[jax.ShapeDtypeStruct((T, width), _MXU), jax.ShapeDtypeStruct((LRU_CONV, lw), f32),
                             vshape, mshape, vshape, mshape, vshape, vshape],
                  scratch=[pltpu.VMEM((_HALO, tc), f32), pltpu.VMEM((_HALO, tc), f32)],
                  sem=("parallel", "arbitrary"))(
        proj, proj, h, h, dy, conv_w, conv_b, wr, br, wi, bi, lam)


def _hg_chunk_fwd(blk, lb, hd):
    q, fr, v, gt = blk[:, 0:hd], blk[:, hd:2 * hd], blk[:, 2 * hd:3 * hd], blk[:, 3 * hd:4 * hd]
    sg = _sig(fr)
    f = lb + (1.0 - lb) * sg
    k = (1.0 - lb) * (1.0 - sg)
    g = jnp.log(f)
    n = HG_CHUNK
    cum = _dot_hi(_tri(n).astype(f32), g)
    row = lax.broadcasted_iota(jnp.int32, g.shape, 0)
    tot = jnp.sum(g, axis=0, keepdims=True)
    mid = jnp.sum(jnp.where(row < n // 2, g, 0.0), axis=0, keepdims=True)
    sq = _sig(q)
    qs = q * sq * (hd ** -0.5)
    e_qd, e_kd, e_qc, e_kt = jnp.exp(cum - mid), jnp.exp(mid - cum), jnp.exp(cum), jnp.exp(tot - cum)
    qd, kd, qc, kt = qs * e_qd, k * e_kd, qs * e_qc, k * e_kt
    p = jnp.where(_tri(n), _dot_nt(qd, kd), 0.0)
    return dict(q=q, v=v, gt=gt, sg=sg, f=f, k=k, sq=sq, qs=qs, tot=tot, e_qd=e_qd, e_kd=e_kd, e_qc=e_qc,
                e_kt=e_kt, qd=qd, kd=kd, qc=qc, kt=kt, p=p)


def _hgrn_fwd(proj, lb, nw, col0, nh, hd, name):
    T = proj.shape[0]
    tb = _pick(T, (_TB_HG, 256, 128, 64))
    ncb = tb // HG_CHUNK
    cb0 = col0 // (4 * hd)

    def body(p_ref, lb_ref, nw_ref, y_ref, sc_ref, st_ref):
        @pl.when(pl.program_id(1) == 0)
        def _():
            st_ref[...] = jnp.zeros_like(st_ref)

        lbv, nwv = lb_ref[...], nw_ref[...]

        def chunk(c, carry):
            r0 = pl.multiple_of(c * HG_CHUNK, HG_CHUNK)
            d = _hg_chunk_fwd(p_ref[pl.ds(r0, HG_CHUNK), :], lbv, hd)
            st = st_ref[...]
            sc_ref[c, 0] = st
            o = _dot(d["p"], d["v"]) + _dot_nt(d["qc"], st)
            st_ref[...] = st * jnp.exp(d["tot"]) + _dot_tn(d["v"], d["kt"])
            rstd = lax.rsqrt(jnp.mean(o * o, axis=-1, keepdims=True) + EPS)
            sil, _ = _silu_and_grad(d["gt"])
            y_ref[pl.ds(r0, HG_CHUNK), :] = (o * rstd * nwv * sil).astype(y_ref.dtype)
            return carry

        lax.fori_loop(0, ncb, chunk, 0, unroll=_HG_UNROLL if ncb % _HG_UNROLL == 0 else 1)

    return _pcall(body, name=name, grid=(nh, T // tb),
                  in_specs=[pl.BlockSpec((tb, 4 * hd), lambda h, i: (i, cb0 + h)),
                            pl.BlockSpec((1, hd), lambda h, i: (0, h)), pl.BlockSpec((1, hd), lambda h, i: (0, 0))],
                  out_specs=[pl.BlockSpec((tb, hd), lambda h, i: (i, h)),
                             pl.BlockSpec((ncb, 1, hd, hd), lambda h, i: (i, h, 0, 0))],
                  out_shape=[jax.ShapeDtypeStruct((T, nh * hd), _MXU),
                             jax.ShapeDtypeStruct((T // HG_CHUNK, nh, hd, hd), f32)],
                  scratch=[pltpu.VMEM((hd, hd), f32)], sem=("parallel", "arbitrary"))(proj, lb, nw)


def _hgrn_bwd(proj, st_chunks, dy, dproj, lb, nw, col0, dycol0, nh, hd, name):
    T = proj.shape[0]
    tb = _pick(T, (_TB_HG, 256, 128, 64))
    ncb = tb // HG_CHUNK
    nt = T // tb
    cb0 = col0 // (4 * hd)
    dyb0 = dycol0 // hd
    n = HG_CHUNK

    def body(p_ref, sc_ref, dy_ref, dpin_ref, lb_ref, nw_ref, dp_ref, dlb_ref, dnw_ref, dst_ref):
        del dpin_ref

        @pl.when(pl.program_id(1) == 0)
        def _():
            dst_ref[...] = jnp.zeros_like(dst_ref)
            dlb_ref[...] = jnp.zeros_like(dlb_ref)
            dnw_ref[...] = jnp.zeros_like(dnw_ref)

        lbv, nwv = lb_ref[...], nw_ref[...]
        tril = _tri(n)

        def chunk(cc, carry):
            c = ncb - 1 - cc
            r0 = pl.multiple_of(c * HG_CHUNK, HG_CHUNK)
            d = _hg_chunk_fwd(p_ref[pl.ds(r0, HG_CHUNK), :], lbv, hd)
            st = sc_ref[c, 0]
            v, gt = d["v"], d["gt"]
            o = _dot(d["p"], v) + _dot_nt(d["qc"], st)
            rstd = lax.rsqrt(jnp.mean(o * o, axis=-1, keepdims=True) + EPS)
            sil, dsil = _silu_and_grad(gt)
            dyv = dy_ref[pl.ds(r0, HG_CHUNK), :]
            xhat = o * rstd
            don = dyv * sil
            dgt = dyv * xhat * nwv * dsil
            dnw_ref[0] += jnp.sum(don * xhat, axis=0, keepdims=True)
            g2 = don * nwv
            do = rstd * (g2 - xhat * jnp.mean(g2 * xhat, axis=-1, keepdims=True))
            dstn = dst_ref[...]
            dp = jnp.where(tril, _dot_nt(do, v), 0.0)
            dv = _dot_tn(d["p"], do) + _dot_nt(d["kt"], dstn)
            dqd = _dot(dp, d["kd"])
            dkd = _dot_tn(dp, d["qd"])
            dqc = _dot(do, st)
            dkt = _dot(v, dstn)
            e_tot = jnp.exp(d["tot"])
            dst_ref[...] = dstn * e_tot + _dot_tn(do, d["qc"])
            dtot = jnp.sum(e_tot * st * dstn, axis=0, keepdims=True) + jnp.sum(dkt * d["kt"], axis=0, keepdims=True)
            dqs = dqd * d["e_qd"] + dqc * d["e_qc"]
            dk = dkd * d["e_kd"] + dkt * d["e_kt"]
            dcum = dqd * d["qd"] - dkd * d["kd"] + dqc * d["qc"] - dkt * d["kt"]
            dg = _dot_hi(_tri(n, lower=False).astype(f32), dcum) + dtot
            df = dg / d["f"] - dk
            sg = d["sg"]
            dfr = df * (1.0 - lbv) * sg * (1.0 - sg)
            dlb_ref[...] += jnp.sum(df * (1.0 - sg), axis=0, keepdims=True)
            sq, q = d["sq"], d["q"]
            dq = dqs * (hd ** -0.5) * sq * (1.0 + q * (1.0 - sq))
            dp_ref[pl.ds(r0, HG_CHUNK), :] = jnp.concatenate([dq, dfr, dv, dgt], axis=1).astype(dp_ref.dtype)
            return carry

        lax.fori_loop(0, ncb, chunk, 0, unroll=_HG_UNROLL if ncb % _HG_UNROLL == 0 else 1)

    width = dproj.shape[1]
    return _pcall(body, name=name, grid=(nh, nt),
                  in_specs=[pl.BlockSpec((tb, 4 * hd), lambda h, i: (nt - 1 - i, cb0 + h)),
                            pl.BlockSpec((ncb, 1, hd, hd), lambda h, i: (nt - 1 - i, h, 0, 0)),
                            pl.BlockSpec((tb, hd), lambda h, i: (nt - 1 - i, dyb0 + h)),
                            pl.BlockSpec(memory_space=pl.ANY),
                            pl.BlockSpec((1, hd), lambda h, i: (0, h)), pl.BlockSpec((1, hd), lambda h, i: (0, 0))],
                  out_specs=[pl.BlockSpec((tb, 4 * hd), lambda h, i: (nt - 1 - i, cb0 + h)),
                             pl.BlockSpec((1, hd), lambda h, i: (0, h)),
                             pl.BlockSpec((1, 1, hd), lambda h, i: (h, 0, 0))],
                  out_shape=[jax.ShapeDtypeStruct((T, width), _MXU), jax.ShapeDtypeStruct((1, nh * hd), f32),
                             jax.ShapeDtypeStruct((nh, 1, hd), f32)],
                  scratch=[pltpu.VMEM((hd, hd), f32)], sem=("parallel", "arbitrary"), aliases={3: 0})(
        proj, st_chunks, dy, dproj, lb, nw)


def _ssd_dt(dtraw_c, g, hpg, dtb_row, dtb_col, a_row, a_col):
    hp = dtraw_c.shape[1]
    sel = (lax.broadcasted_iota(jnp.int32, (hp, hpg), 0) == g * hpg + lax.broadcasted_iota(jnp.int32, (hp, hpg), 1))
    selt = (lax.broadcasted_iota(jnp.int32, (hpg, hp), 1) == g * hpg + lax.broadcasted_iota(jnp.int32, (hpg, hp), 0))
    pre = _dot_hi(dtraw_c, sel.astype(f32)) + dtb_row
    pre_t = lax.dot_general(selt.astype(f32), dtraw_c, (((1,), (1,)), ((), ())), preferred_element_type=f32,
                            precision=_HI) + dtb_col
    dt = _softplus(pre)
    dt_t = _softplus(pre_t)
    return pre, dt, dt * a_row, dt_t * a_col


def _ssd_chunk_fwd(xbc, z, dtraw_c, g, prm, s_ref, hpg, gw):
    n, ns, pp = SSD_CHUNK, SSD_STATE, 2 * (gw // hpg)
    dtb_row, dtb_col, a_row, a_col, d_row = prm
    bm, cm = xbc[:, gw:gw + ns], xbc[:, gw + ns:gw + 2 * ns]
    pre, dt, a, a_t = _ssd_dt(dtraw_c, g, hpg, dtb_row, dtb_col, a_row, a_col)
    cum = _dot_hi(_tri(n).astype(f32), a)
    cum_t = _dot_hi(a_t, _tri(n, lower=False).astype(f32))
    tot = jnp.sum(a, axis=0, keepdims=True)
    cb = _dot_nt(cm, bm)
    tril = _tri(n)
    lo = lax.broadcasted_iota(jnp.int32, (n, pp), 1) < pp // 2
    sub_lo = lax.broadcasted_iota(jnp.int32, (pp, ns), 0) < pp // 2

    def per_lane(m, h0):
        return jnp.where(lo[0:m.shape[0]], m[:, h0:h0 + 1], m[:, h0 + 1:h0 + 2])

    pairs = []
    for p in range(hpg // 2):
        h0 = 2 * p
        xp = xbc[:, pp * p:pp * (p + 1)]
        dtl = per_lane(dt, h0)
        xdt = xp * dtl
        lm, mm = [], []
        for h in (h0, h0 + 1):
            l_h = jnp.where(tril, jnp.exp(jnp.minimum(cum[:, h:h + 1] - cum_t[h:h + 1, :], 0.0)), 0.0)
            lm.append(l_h)
            mm.append(cb * l_h)
        y_diag = _dot(mm[0], jnp.where(lo, xdt, 0.0)) + _dot(mm[1], jnp.where(lo, 0.0, xdt))
        sp = s_ref[p]
        e = jnp.exp(per_lane(cum, h0))
        y_off = _dot_nt(cm, sp) * e
        fd = jnp.exp(per_lane(tot - cum, h0))
        u = xdt * fd
        etot = jnp.exp(tot)
        etot_col = jnp.where(sub_lo, etot[:, h0:h0 + 1], etot[:, h0 + 1:h0 + 2])
        dl = per_lane(d_row, h0)
        y = y_diag + y_off + xp * dl
        zp = z[:, pp * p:pp * (p + 1)]
        sil, dsil = _silu_and_grad(zp)
        pairs.append(dict(xp=xp, dtl=dtl, xdt=xdt, lm=lm, mm=mm, y_diag=y_diag, sp=sp, e=e, y_off=y_off, fd=fd, u=u,
                          etot_col=etot_col, dl=dl, y=y, sil=sil, dsil=dsil, y2=y * sil))
    ms = sum(jnp.sum(q["y2"] * q["y2"], axis=-1, keepdims=True) for q in pairs) * (1.0 / gw)
    rstd = lax.rsqrt(ms + EPS)
    return dict(bm=bm, cm=cm, pre=pre, dt=dt, a=a, cum=cum, cum_t=cum_t, tot=tot, cb=cb, lo=lo, sub_lo=sub_lo,
                pairs=pairs, rstd=rstd)


def _ssd_specs(tb, gwx, hpg, hp, cw, gw, tmap):
    r = tb // _HALO
    row = pl.BlockSpec((1, 1, hpg), lambda g, i: (g, 0, 0))
    col = pl.BlockSpec((1, hpg, 1), lambda g, i: (g, 0, 0))
    return [pl.BlockSpec((tb, gwx), lambda g, i: (tmap(i), g)),
            pl.BlockSpec((_HALO, gwx), lambda g, i: (jnp.maximum(tmap(i) * r - 1, 0), g)),
            pl.BlockSpec((tb, hp), lambda g, i: (tmap(i), 0)),
            pl.BlockSpec((1, SSD_CONV, cw), lambda g, i: (g, 0, 0)),
            pl.BlockSpec((1, 1, cw), lambda g, i: (g, 0, 0)),
            row, col, row, col, row,
            pl.BlockSpec((1, gw), lambda g, i: (0, g))]


def _ssd_fwd(proj, dt_raw, conv_w, conv_b, dtb_row, dtb_col, a_row, a_col, d_row, norm_w, hpg, name):
    T = proj.shape[0]
    G = SSD_GROUPS
    gwx = proj.shape[1] // G
    gw = (gwx - 2 * SSD_STATE) // 2
    cw = gw + 2 * SSD_STATE
    hp = dt_raw.shape[1]
    tb = _pick(T, (_TB_SSD, 128, 64))
    ncb = tb // SSD_CHUNK
    npair = hpg // 2
    pp = 2 * (gw // hpg)
    assert pp == _LANES and hpg % 2 == 0

    def body(p_ref, prev_ref, dt_ref, cw_ref, cb_ref, dtbr_ref, dtbc_ref, ar_ref, ac_ref, dr_ref, nw_ref,
             y_ref, sc_ref, s_ref, xbc_ref):
        g = pl.program_id(0)
        i = pl.program_id(1)

        @pl.when(i == 0)
        def _():
            s_ref[...] = jnp.zeros_like(s_ref)

        raw = p_ref[:, gw:]
        prev = jnp.where(i > 0, prev_ref[:, gw:], 0.0)
        pre = _conv_fwd(raw, prev, [cw_ref[0, j:j + 1, :] for j in range(SSD_CONV)], cb_ref[0], SSD_CONV)
        xbc_ref[...] = pre * _sig(pre)
        prm = (dtbr_ref[0], dtbc_ref[0], ar_ref[0], ac_ref[0], dr_ref[0])
        nwv = nw_ref[...]

        def chunk(c, carry):
            r0 = pl.multiple_of(c * SSD_CHUNK, SSD_CHUNK)
            rows = pl.ds(r0, SSD_CHUNK)
            sc_ref[c, 0] = s_ref[...]
            d = _ssd_chunk_fwd(xbc_ref[rows, :], p_ref[rows, 0:gw], dt_ref[rows, :], g, prm, s_ref, hpg, gw)
            for p, q in enumerate(d["pairs"]):
                s_ref[p] = q["sp"] * q["etot_col"] + _dot_tn(q["u"], d["bm"])
                y_ref[rows, pp * p:pp * (p + 1)] = (q["y2"] * d["rstd"] * nwv[:, pp * p:pp * (p + 1)]).astype(
                    y_ref.dtype)
            return carry

        lax.fori_loop(0, ncb, chunk, 0, unroll=2 if ncb % 2 == 0 else 1)

    return _pcall(body, name=name, grid=(G, T // tb),
                  in_specs=_ssd_specs(tb, gwx, hpg, hp, cw, gw, lambda i: i),
                  out_specs=[pl.BlockSpec((tb, gw), lambda g, i: (i, g)),
                             pl.BlockSpec((ncb, 1, npair, pp, SSD_STATE), lambda g, i: (i, g, 0, 0, 0))],
                  out_shape=[jax.ShapeDtypeStruct((T, G * gw), _MXU),
                             jax.ShapeDtypeStruct((T // SSD_CHUNK, G, npair, pp, SSD_STATE), f32)],
                  scratch=[pltpu.VMEM((npair, pp, SSD_STATE), f32), pltpu.VMEM((tb, cw), f32)],
                  sem=("parallel", "arbitrary"))(
        proj, proj, dt_raw, conv_w, conv_b, dtb_row, dtb_col, a_row, a_col, d_row, norm_w)


def _ssd_bwd(proj, dt_raw, s_chunks, dy, conv_w, conv_b, dtb_row, dtb_col, a_row, a_col, d_row, norm_w, hpg, name):
    T = proj.shape[0]
    G = SSD_GROUPS
    gwx = proj.shape[1] // G
    gw = (gwx - 2 * SSD_STATE) // 2
    ns = SSD_STATE
    cw = gw + 2 * ns
    hp = dt_raw.shape[1]
    tb = _pick(T, (_TB_SSD, 128, 64))
    ncb, nt = tb // SSD_CHUNK, T // tb
    npair = hpg // 2
    pp = 2 * (gw // hpg)
    n = SSD_CHUNK

    def body(p_ref, prev_ref, dt_ref, cw_ref, cb_ref, dtbr_ref, dtbc_ref, ar_ref, ac_ref, dr_ref, nw_ref,
             sc_ref, dy_ref,
             dp_ref, ddt_ref, dcw_ref, dcb_ref, ddtb_ref, da_ref, dd_ref, dnw_ref,
             ds_ref, s_ref, xbc_ref, dxbc_ref, carry_ref):
        g = pl.program_id(0)
        i = pl.program_id(1)
        ti = nt - 1 - i

        @pl.when(i == 0)
        def _():
            for ref in (ds_ref, carry_ref, dcw_ref, dcb_ref, ddtb_ref, da_ref, dd_ref, dnw_ref):
                ref[...] = jnp.zeros_like(ref)

        raw = p_ref[:, gw:]
        prev = jnp.where(ti > 0, prev_ref[:, gw:], 0.0)
        cwr = [cw_ref[0, j:j + 1, :] for j in range(SSD_CONV)]
        pre = _conv_fwd(raw, prev, cwr, cb_ref[0], SSD_CONV)
        xbc_ref[...] = pre * _sig(pre)
        prm = (dtbr_ref[0], dtbc_ref[0], ar_ref[0], ac_ref[0], dr_ref[0])
        a_row = ar_ref[0]
        nwv = nw_ref[...]
        tril = _tri(n)
        lane8 = lax.broadcasted_iota(jnp.int32, (n, hpg), 1)

        def chunk(cc, carry):
            c = ncb - 1 - cc
            r0 = pl.multiple_of(c * SSD_CHUNK, SSD_CHUNK)
            rows = pl.ds(r0, SSD_CHUNK)
            s_ref[...] = sc_ref[c, 0]
            xbc = xbc_ref[rows, :]
            d = _ssd_chunk_fwd(xbc, p_ref[rows, 0:gw], dt_ref[rows, :], g, prm, s_ref, hpg, gw)
            bm, cm, lo, sub_lo, rstd = d["bm"], d["cm"], d["lo"], d["sub_lo"], d["rstd"]
            dyo = dy_ref[rows, :]
            s1 = sum(jnp.sum(dyo[:, pp * p:pp * (p + 1)] * nwv[:, pp * p:pp * (p + 1)] * q["y2"], axis=-1,
                             keepdims=True) for p, q in enumerate(d["pairs"]))
            dcb = jnp.zeros((n, n), f32)
            cb_t = _dot_nt(bm, cm)
            triu = _tri(n, lower=False)
            dbm = jnp.zeros((n, ns), f32)
            dcm = jnp.zeros((n, ns), f32)
            dcum = jnp.zeros((n, hpg), f32)
            ddt_dir = jnp.zeros((n, hpg), f32)
            dtot = jnp.zeros((1, hpg), f32)
            ddv = jnp.zeros((1, hpg), f32)
            dxs = []
            for p, q in enumerate(d["pairs"]):
                h0 = 2 * p
                sl = slice(pp * p, pp * (p + 1))
                dyp = dyo[:, sl]
                nwp = nwv[:, sl]
                dnw_ref[:, sl] += jnp.sum(dyp * q["y2"] * rstd, axis=0, keepdims=True)
                dy2 = rstd * (dyp * nwp - q["y2"] * (rstd * rstd) * s1 * (1.0 / gw))
                dyv = dy2 * q["sil"]
                dz = dy2 * q["y"] * q["dsil"]
                dp_ref[rows, sl] = dz.astype(dp_ref.dtype)
                xp, xdt, mm, lm = q["xp"], q["xdt"], q["mm"], q["lm"]
                dyx = dyv * xp
                dx = dyv * q["dl"]
                dxdt_diag = jnp.where(lo, _dot_tn(mm[0], dyv), _dot_tn(mm[1], dyv))
                dy_lo, dy_hi = jnp.where(lo, dyv, 0.0), jnp.where(lo, 0.0, dyv)
                dm0 = jnp.where(tril, _dot_nt(dy_lo, xdt), 0.0)
                dm1 = jnp.where(tril, _dot_nt(dy_hi, xdt), 0.0)
                dcb = dcb + dm0 * lm[0] + dm1 * lm[1]
                seg = []
                for hh, (dm, dyh) in enumerate(((dm0, dy_lo), (dm1, dy_hi))):
                    h = h0 + hh
                    l_t = jnp.where(triu, jnp.exp(jnp.minimum(d["cum_t"][h:h + 1, :] - d["cum"][:, h:h + 1], 0.0)), 0.0)
                    dm_t = _dot_nt(xdt, dyh)
                    seg.append(jnp.sum(dm * mm[hh], axis=1, keepdims=True)
                               - jnp.sum(dm_t * (cb_t * l_t), axis=1, keepdims=True))
                dsn = ds_ref[p]
                dzz = dyv * q["e"]
                dcm = dcm + _dot(dzz, q["sp"])
                ds_ref[p] = dsn * q["etot_col"] + _dot_tn(dzz, cm)
                du = _dot_nt(bm, dsn)
                dbm = dbm + _dot(q["u"], dsn)
                dxdt = dxdt_diag + du * q["fd"]
                wv = du * q["u"]
                cterm = dyv * q["y_off"] - wv
                ddir = dxdt * xp
                sdec = q["etot_col"] * q["sp"] * dsn

                def halves(x):
                    lo_s = jnp.sum(jnp.where(lo, x, 0.0), axis=1, keepdims=True)
                    return lo_s, jnp.sum(x, axis=1, keepdims=True) - lo_s

                c_lo, c_hi = halves(cterm)
                d_lo, d_hi = halves(ddir)
                w_lo, w_hi = halves(wv)
                x_lo, x_hi = halves(dyx)
                s_all = jnp.sum(sdec, axis=1, keepdims=True)
                s_lo = jnp.sum(jnp.where(sub_lo[:, 0:1], s_all, 0.0), axis=0, keepdims=True)
                s_hi = jnp.sum(s_all, axis=0, keepdims=True) - s_lo
                for h, cv, dv_, wv_, xv_, sv_ in ((h0, c_lo + seg[0], d_lo, w_lo, x_lo, s_lo),
                                                  (h0 + 1, c_hi + seg[1], d_hi, w_hi, x_hi, s_hi)):
                    sel = lane8 == h
                    dcum = dcum + jnp.where(sel, cv, 0.0)
                    ddt_dir = ddt_dir + jnp.where(sel, dv_, 0.0)
                    dtot = dtot + jnp.where(sel[0:1], jnp.sum(wv_, axis=0, keepdims=True) + sv_, 0.0)
                    ddv = ddv + jnp.where(sel[0:1], jnp.sum(xv_, axis=0, keepdims=True), 0.0)
                dxs.append(dx + dxdt * q["dtl"])
            da_c = _dot_hi(_tri(n, lower=False).astype(f32), dcum) + dtot
            ddt = da_c * a_row + ddt_dir
            da_ref[0] += jnp.sum(da_c * d["dt"], axis=0, keepdims=True)
            dd_ref[0] += ddv
            ddt_raw = ddt * _sig(d["pre"])
            ddtb_ref[0] += jnp.sum(ddt_raw, axis=0, keepdims=True)
            ddt_ref[0, rows, :] = ddt_raw
            dcm = dcm + _dot(dcb, bm)
            dbm = dbm + _dot_tn(dcb, cm)
            dxbc_ref[rows, :] = jnp.concatenate(dxs + [dbm, dcm], axis=1)
            return carry

        lax.fori_loop(0, ncb, chunk, 0, unroll=2 if ncb % 2 == 0 else 1)
        _, dsil = _silu_and_grad(pre)
        dpre = dxbc_ref[...] * dsil
        draw, dws, db = _conv_bwd(dpre, carry_ref[...], raw, prev, cwr, SSD_CONV)
        carry_ref[...] = dpre[0:_HALO]
        for j, r_ in enumerate(dws):
            dcw_ref[0, j:j + 1, :] += r_
        dcb_ref[0] += db
        dp_ref[:, gw:] = draw.astype(dp_ref.dtype)

    tmap = lambda i: nt - 1 - i
    row = pl.BlockSpec((1, 1, hpg), lambda g, i: (g, 0, 0))
    rshape = jax.ShapeDtypeStruct((G, 1, hpg), f32)
    return _pcall(body, name=name, grid=(G, nt),
                  in_specs=_ssd_specs(tb, gwx, hpg, hp, cw, gw, tmap) + [
                      pl.BlockSpec((ncb, 1, npair, pp, ns), lambda g, i: (tmap(i), g, 0, 0, 0)),
                      pl.BlockSpec((tb, gw), lambda g, i: (tmap(i), g))],
                  out_specs=[pl.BlockSpec((tb, gwx), lambda g, i: (tmap(i), g)),
                             pl.BlockSpec((1, tb, hpg), lambda g, i: (g, tmap(i), 0)),
                             pl.BlockSpec((1, SSD_CONV, cw), lambda g, i: (g, 0, 0)),
                             pl.BlockSpec((1, 1, cw), lambda g, i: (g, 0, 0)),
                             row, row, row,
                             pl.BlockSpec((1, gw), lambda g, i: (0, g))],
                  out_shape=[jax.ShapeDtypeStruct((T, G * gwx), _MXU), jax.ShapeDtypeStruct((G, T, hpg), f32),
                             jax.ShapeDtypeStruct((G, SSD_CONV, cw), f32), jax.ShapeDtypeStruct((G, 1, cw), f32),
                             rshape, rshape, rshape, jax.ShapeDtypeStruct((1, G * gw), f32)],
                  scratch=[pltpu.VMEM((npair, pp, ns), f32), pltpu.VMEM((npair, pp, ns), f32),
                           pltpu.VMEM((tb, cw), f32), pltpu.VMEM((tb, cw), f32), pltpu.VMEM((_HALO, cw), f32)],
                  sem=("parallel", "arbitrary"))(
        proj, proj, dt_raw, conv_w, conv_b, dtb_row, dtb_col, a_row, a_col, d_row, norm_w, s_chunks, dy)


def _interleave(w, parts, blk):
    lead = w.shape[:-1]
    n = w.shape[-1] // (parts * blk)
    nd = len(lead)
    w = w.reshape(*lead, parts, n, blk)
    w = jnp.swapaxes(w, nd, nd + 1)
    return w.reshape(*lead, parts * n * blk)


def _deinterleave(w, parts, blk):
    lead = w.shape[:-1]
    n = w.shape[-1] // (parts * blk)
    nd = len(lead)
    w = w.reshape(*lead, n, parts, blk)
    w = jnp.swapaxes(w, nd, nd + 1)
    return w.reshape(*lead, parts * n * blk)


def _even_perm(w, lw, hd, tc):
    return jnp.concatenate([_interleave(w[..., :2 * lw], 2, tc), _interleave(w[..., 2 * lw:], 4, hd)], axis=-1)


def _even_unperm(w, lw, hd, tc):
    return jnp.concatenate([_deinterleave(w[..., :2 * lw], 2, tc), _deinterleave(w[..., 2 * lw:], 4, hd)], axis=-1)


def _ssd_split(w, inner, offset):
    G, N = SSD_GROUPS, SSD_STATE
    lead = w.shape[:-1]
    xs = w[..., offset:offset + inner].reshape(*lead, G, inner // G)
    bm = w[..., offset + inner:offset + inner + G * N].reshape(*lead, G, N)
    cm = w[..., offset + inner + G * N:offset + inner + 2 * G * N].reshape(*lead, G, N)
    return xs, bm, cm


def _ssd_in_perm(w, inner, heads, hp):
    G = SSD_GROUPS
    lead = w.shape[:-1]
    z = w[..., :inner].reshape(*lead, G, inner // G)
    xs, bm, cm = _ssd_split(w, inner, inner)
    main = jnp.concatenate([z, xs, bm, cm], axis=-1).reshape(*lead, -1)
    dt = w[..., w.shape[-1] - heads:]
    dt = jnp.pad(dt, [(0, 0)] * len(lead) + [(0, hp - heads)])
    return main, dt


def _ssd_in_unperm(main, dt, inner, heads):
    G, N = SSD_GROUPS, SSD_STATE
    gw = inner // G
    lead = main.shape[:-1]
    m = main.reshape(*lead, G, 2 * gw + 2 * N)
    parts = [m[..., :gw], m[..., gw:2 * gw], m[..., 2 * gw:2 * gw + N], m[..., 2 * gw + N:]]
    return jnp.concatenate([p.reshape(*lead, -1) for p in parts] + [dt[..., :heads]], axis=-1)


def _ssd_conv_perm(w, inner):
    return jnp.concatenate(_ssd_split(w, inner, 0), axis=-1)


def _ssd_conv_unperm(w, inner):
    G, N = SSD_GROUPS, SSD_STATE
    gw = inner // G
    lead = w.shape[:-2]
    parts = [w[..., :gw], w[..., gw:gw + N], w[..., gw + N:]]
    return jnp.concatenate([p.reshape(*lead, -1) for p in parts], axis=-1)


def _blockdiag(w, tc):
    nb, b, _ = w.shape
    per = tc // b
    eye = jnp.eye(per, dtype=w.dtype)
    w = w.reshape(nb // per, per, b, b)
    return jnp.einsum("jbik,bc->jbick", w, eye).reshape(nb // per, tc, tc)


def _blockdiag_extract(g, b):
    nj, tc, _ = g.shape
    per = tc // b
    g = g.reshape(nj, per, b, per, b)
    return jnp.einsum("jbick,bc->jbik", g, jnp.eye(per, dtype=g.dtype)).reshape(nj * per, b, b)


def _lower_bounds(hg_lower_bounds):
    lbp = jax.nn.softmax(hg_lower_bounds.astype(f32), axis=0)
    return jnp.cumsum(lbp, axis=0) - lbp[0]


def _local_step(x, target, w):
    T, D = x.shape
    depth = w["norm_mix_w"].shape[0]
    lw = w["lru_conv_b"].shape[1]
    hw = w["hg_lower_bounds"].shape[1]
    hd = w["hg_norm_w"].shape[1]
    nh = hw // hd
    heads = w["ssd_dt_bias"].shape[1]
    inner = w["ssd_norm_w"].shape[1]
    G, N = SSD_GROUPS, SSD_STATE
    hpg = heads // G
    gw = inner // G
    hp = -(-heads // _LANES) * _LANES
    F = w["ffn_w_down"].shape[1]
    tcl = _pick(lw, (_TC_LRU, 128))
    tcf = _pick(F, (_TC_FFN, 256, 128))
    even_in = 2 * lw + 4 * hw
    n_even, n_odd = w["ev_w_in"].shape[0], w["ssd_w_in"].shape[0]

    ev_w_in = _even_perm(w["ev_w_in"], lw, hd, tcl)
    ssd_main, ssd_dtw = _ssd_in_perm(w["ssd_w_in"], inner, heads, hp)
    ssd_cw = jnp.swapaxes(_ssd_conv_perm(w["ssd_conv_w"], inner), 1, 2)
    ssd_cb = _ssd_conv_perm(w["ssd_conv_b"], inner)[:, :, None, :]
    ffn_up = _interleave(w["ffn_w_up"], 2, tcf)
    ffn_cw = _interleave(w["ffn_conv_w"], 2, tcf)
    ffn_cb = _interleave(w["ffn_conv_b"], 2, tcf)
    lbs_fn = jax.vjp(_lower_bounds, w["hg_lower_bounds"])
    lbs = lbs_fn[0]
    a_neg = -jnp.exp(w["ssd_a_log"].astype(f32))

    def rowcol(v):
        return v.reshape(G, 1, hpg), v.reshape(G, hpg, 1)

    saved = []
    for l in range(depth):
        s = {"x": x}
        h = _rmsnorm_fwd(x, w["norm_mix_w"][l], f"norm_mix_fwd_{l}")
        s["h"] = h
        if l % 2 == 0:
            e = l // 2
            proj = _mm(h, ev_w_in, b_lead=e, M=T, N=even_in, K=D, name=f"even_in_{l}")
            s["wr"] = _blockdiag(w["lru_w_r"][e], tcl).astype(_MXU)
            s["wi"] = _blockdiag(w["lru_w_i"][e], tcl).astype(_MXU)
            vecs = [w[k][e].reshape(1, lw) for k in ("lru_conv_b", "lru_b_r", "lru_b_i", "lru_lambda")]
            s["lru_args"] = (w["lru_conv_w"][e], vecs[0], s["wr"], vecs[1], s["wi"], vecs[2], vecs[3])
            ya, hst = _lru_fwd(proj, *s["lru_args"], lw, f"lru_fwd_{l}")
            s["hg_args"] = (lbs[e].reshape(1, hw), w["hg_norm_w"][e].reshape(1, hd))
            yb, stc = _hgrn_fwd(proj, *s["hg_args"], 2 * lw, nh, hd, f"hgrn_fwd_{l}")
            xm = _mm(ya, w["ev_w_out"], b_lead=e, M=T, N=D, K=lw, resid=x, name=f"even_out_a_{l}")
            xm = _mm(yb, w["ev_w_out"], b_lead=e, M=T, N=D, K=hw, b_off=(lw, 0), resid=xm, name=f"even_out_b_{l}")
            s.update(proj=proj, ya=ya, yb=yb, hst=hst, stc=stc)
        else:
            o = l // 2
            proj = _mm(h, ssd_main, b_lead=o, M=T, N=ssd_main.shape[-1], K=D, name=f"ssd_in_{l}")
            dtr = _mm(h, ssd_dtw, b_lead=o, M=T, N=hp, K=D, name=f"ssd_dt_{l}")
            dtb_r, dtb_c = rowcol(w["ssd_dt_bias"][o].astype(f32))
            a_r, a_c = rowcol(a_neg[o])
            d_r, _ = rowcol(w["ssd_d"][o].astype(f32))
            s["ssd_args"] = (ssd_cw[o], ssd_cb[o], dtb_r, dtb_c, a_r, a_c, d_r, w["ssd_norm_w"][o].reshape(1, inner))
            y, sc = _ssd_fwd(proj, dtr, *s["ssd_args"], hpg, f"ssd_fwd_{l}")
            xm = _mm(y, w["ssd_w_out"], b_lead=o, M=T, N=D, K=inner, resid=x, name=f"ssd_out_{l}")
            s.update(proj=proj, dtr=dtr, y=y, sc=sc)
        hn = _rmsnorm_fwd(xm, w["norm_ffn_w"][l], f"norm_ffn_fwd_{l}")
        up = _mm(hn, ffn_up, b_lead=l, M=T, N=2 * F, K=D, name=f"ffn_up_{l}")
        act = _ffn_mid_fwd(up, ffn_cw[l], ffn_cb[l].reshape(1, 2 * F), f"ffn_mid_fwd_{l}")
        x = _mm(act, w["ffn_w_down"], b_lead=l, M=T, N=D, K=F, resid=xm, name=f"ffn_down_{l}")
        s.update(xm=xm, hn=hn, up=up, act=act)
        saved.append(s)

    loss, dx, dxb, dwf = _loss_head(x, w["norm_final_w"], target)
    g = {k: [None] * v.shape[0] for k, v in w.items() if k != "norm_final_w"}
    g["norm_final_w"] = dwf.reshape(D)

    for l in reversed(range(depth)):
        s = saved[l]
        g["ffn_w_down"][l] = _mm(s["act"], dxb, ta=True, M=F, N=D, K=T, name=f"ffn_down_dw_{l}")
        dact = _mm(dxb, w["ffn_w_down"], b_lead=l, tb=True, M=T, N=F, K=D, name=f"ffn_down_dx_{l}")
        dup, dcw, dcb = _ffn_mid_bwd(s["up"], dact, ffn_cw[l], ffn_cb[l].reshape(1, 2 * F), f"ffn_mid_bwd_{l}")
        g["ffn_conv_w"][l] = _deinterleave(dcw, 2, tcf)
        g["ffn_conv_b"][l] = _deinterleave(dcb, 2, tcf).reshape(2 * F)
        g["ffn_w_up"][l] = _mm(s["hn"], dup, ta=True, M=D, N=2 * F, K=T, name=f"ffn_up_dw_{l}", tn=tcf,
                               tm=_pick(D, (2 * _MM_TM, _MM_TM, 512, 256, 128)),
                               out_perm=lambda j: lax.rem(j, 2) * (F // tcf) + lax.div(j, 2))
        dhn = _mm(dup, ffn_up, b_lead=l, tb=True, M=T, N=D, K=2 * F, name=f"ffn_up_dx_{l}")
        dxm, dxmb, dnf = _rmsnorm_bwd(dhn, s["xm"], w["norm_ffn_w"][l], dx, f"norm_ffn_bwd_{l}")
        g["norm_ffn_w"][l] = dnf.reshape(D)
        if l % 2 == 0:
            e = l // 2
            dy = _mm(dxmb, w["ev_w_out"], b_lead=e, tb=True, M=T, N=lw + hw, K=D, name=f"even_out_dx_{l}")
            dwo_a = _mm(s["ya"], dxmb, ta=True, M=lw, N=D, K=T, name=f"even_out_a_dw_{l}")
            dwo_b = _mm(s["yb"], dxmb, ta=True, M=hw, N=D, K=T, name=f"even_out_b_dw_{l}")
            g["ev_w_out"][e] = jnp.concatenate([dwo_a, dwo_b], axis=0)
            dproj, dcw, dcb, dwr, dbr, dwi, dbi, dlam = _lru_bwd(s["proj"], s["hst"], dy, *s["lru_args"], lw,
                                                                 f"lru_bwd_{l}")
            dproj, dlb, dnw = _hgrn_bwd(s["proj"], s["stc"], dy, dproj, *s["hg_args"], 2 * lw, lw, nh, hd,
                                        f"hgrn_bwd_{l}")
            g["lru_conv_w"][e] = dcw
            g["lru_conv_b"][e] = dcb.reshape(lw)
            g["lru_w_r"][e] = _blockdiag_extract(dwr, LRU_BLOCK)
            g["lru_w_i"][e] = _blockdiag_extract(dwi, LRU_BLOCK)
            g["lru_b_r"][e] = dbr.reshape(lw)
            g["lru_b_i"][e] = dbi.reshape(lw)
            g["lru_lambda"][e] = dlam.reshape(lw)
            g["hg_lower_bounds"][e] = dlb.reshape(hw)
            g["hg_norm_w"][e] = jnp.sum(dnw, axis=(0, 1))
            g["ev_w_in"][e] = _even_unperm(_mm(s["h"], dproj, ta=True, M=D, N=even_in, K=T, name=f"even_in_dw_{l}"),
                                           lw, hd, tcl)
            dh = _mm(dproj, ev_w_in, b_lead=e, tb=True, M=T, N=D, K=even_in, name=f"even_in_dx_{l}")
        else:
            o = l // 2
            dy = _mm(dxmb, w["ssd_w_out"], b_lead=o, tb=True, M=T, N=inner, K=D, name=f"ssd_out_dx_{l}")
            g["ssd_w_out"][o] = _mm(s["y"], dxmb, ta=True, M=inner, N=D, K=T, name=f"ssd_out_dw_{l}")
            dproj, ddt, dcw, dcb, ddtb, da, dd, dnw = _ssd_bwd(s["proj"], s["dtr"], s["sc"], dy, *s["ssd_args"], hpg,
                                                               f"ssd_bwd_{l}")
            ddt = jnp.pad(jnp.swapaxes(ddt, 0, 1).reshape(T, heads), ((0, 0), (0, hp - heads)))
            g["ssd_conv_w"][o] = _ssd_conv_unperm(jnp.swapaxes(dcw, 0, 1), inner)
            g["ssd_conv_b"][o] = _ssd_conv_unperm(dcb[:, 0, :], inner)
            g["ssd_dt_bias"][o] = ddtb.reshape(heads)
            g["ssd_a_log"][o] = da.reshape(heads) * a_neg[o]
            g["ssd_d"][o] = dd.reshape(heads)
            g["ssd_norm_w"][o] = dnw.reshape(inner)
            dmain = _mm(s["h"], dproj, ta=True, M=D, N=dproj.shape[1], K=T, name=f"ssd_in_dw_{l}")
            ddtw = _mm(s["h"], ddt, ta=True, M=D, N=hp, K=T, name=f"ssd_dt_dw_{l}")
            g["ssd_w_in"][o] = _ssd_in_unperm(dmain, ddtw, inner, heads)
            dh = _mm(dproj, ssd_main, b_lead=o, tb=True, M=T, N=D, K=dproj.shape[1], name=f"ssd_in_dx_{l}")
            dh = _mm(ddt, ssd_dtw, b_lead=o, tb=True, M=T, N=D, K=hp, resid=dh, name=f"ssd_dt_dx_{l}")
        dx, dxb, dnm = _rmsnorm_bwd(dh, s["x"], w["norm_mix_w"][l], dxm, f"norm_mix_bwd_{l}")
        g["norm_mix_w"][l] = dnm.reshape(D)

    grads = {k: (jnp.stack(v) if isinstance(v, list) else v) for k, v in g.items()}
    grads["hg_lower_bounds"] = lbs_fn[1](grads["hg_lower_bounds"])[0]
    return loss[0, 0], dx, grads


def _best_div(n, cap, mult=8):
    best = mult
    for d in range(mult, min(n, cap) + 1, mult):
        if n % d == 0:
            best = d
    assert n % best == 0, (n, best)
    return best


def _adamw(wv, gv, mv, vv, name):
    R, C = wv.shape
    tr = _best_div(R, max(8, _SLAB_BLOCK_ELEMS // C))

    def body(w_ref, g_ref, m_ref, v_ref, d_ref, nm_ref, nv_ref):
        g = g_ref[...]
        m = ADAM_B1 * m_ref[...] + (1.0 - ADAM_B1) * g
        v = ADAM_B2 * v_ref[...] + (1.0 - ADAM_B2) * (g * g)
        m_hat = m / (1.0 - ADAM_B1 ** ADAM_STEP)
        v_hat = v / (1.0 - ADAM_B2 ** ADAM_STEP)
        d_ref[...] = -ADAM_LR * (m_hat / (jnp.sqrt(v_hat) + ADAM_EPS) + ADAM_WD * w_ref[...])
        nm_ref[...] = m
        nv_ref[...] = v

    blk = pl.BlockSpec((tr, C), lambda i: (i, 0))
    shp = jax.ShapeDtypeStruct((R, C), f32)
    return _pcall(body, name=name, grid=(R // tr,), in_specs=[blk] * 4, out_specs=[blk] * 3, out_shape=[shp] * 3,
                  sem=("parallel",))(wv, gv, mv, vv)


_ANY = pl.BlockSpec(memory_space=pl.ANY)
_MESH = pl.DeviceIdType.MESH


def _where_am_i():
    x, y, c = lax.axis_index("x"), lax.axis_index("y"), lax.axis_index("c")
    chips = [(1 - x, y), (x, 1 - y), (1 - x, 1 - y)]
    return x, y, c, chips


def _rcopy(src, dst, send_sems, recv_sems, j, dev):
    return pltpu.make_async_remote_copy(src_ref=src, dst_ref=dst, send_sem=send_sems.at[j], recv_sem=recv_sems.at[j],
                                        device_id=dev, device_id_type=_MESH)


def _comm_call(body, name, out_shape, n_sems, n_in):
    scratch = [pltpu.SemaphoreType.DMA((n_sems,)), pltpu.SemaphoreType.DMA((n_sems,))]
    return pl.pallas_call(body, name=name, out_shape=out_shape, in_specs=[_ANY] * n_in, out_specs=_ANY,
                          scratch_shapes=scratch)


def _allgather_chips(shard, name):
    R, C = shard.shape
    rh = R // 2
    assert R % 2 == 0

    def body(x_ref, out_ref, send_sems, recv_sems):
        x, y, c, chips = _where_am_i()
        k = 2 * x + y
        sib = (x, y, 1 - c)

        def blk(kk, half):
            return out_ref.at[kk, pl.ds(half * rh, rh)]

        first =[_rcopy(x_ref.at[pl.ds(c * rh, rh)], blk(k, c), send_sems, recv_sems, j, (cx, cy, c))
                 for j, (cx, cy) in enumerate(chips)]
        for cp in first:
            cp.start()
        passed = []
        for j, (cx, cy) in enumerate(chips):
            kk = 2 * cx + cy
            _rcopy(blk(kk, c), blk(kk, c), send_sems, recv_sems, j, (cx, cy, c)).wait_recv()
            fwd = _rcopy(blk(kk, c), blk(kk, c), send_sems, recv_sems, 3 + j, sib)
            fwd.start()
            passed.append(fwd)
        for j, (cx, cy) in enumerate(chips):
            kk = 2 * cx + cy
            _rcopy(blk(kk, 1 - c), blk(kk, 1 - c), send_sems, recv_sems, 3 + j, sib).wait_recv()
        for cp in first + passed:
            cp.wait_send()

    return _comm_call(body, name, jax.ShapeDtypeStruct((4, R, C), shard.dtype), 6, 1)(shard)


def _d2d_exchange(g, name):
    _, R, C = g.shape
    rh = R // 2

    def body(g_ref, a_ref, send_sems, recv_sems):
        x, y, c, _ = _where_am_i()
        cp = _rcopy(g_ref.at[:, pl.ds((1 - c) * rh, rh)], a_ref, send_sems, recv_sems, 0, (x, y, 1 - c))
        cp.start()
        cp.wait()

    return _comm_call(body, name, jax.ShapeDtypeStruct((4, rh, C), g.dtype), 1, 1)(g)


def _ici_scatter(p, name):
    _, rh, C = p.shape

    def body(p_ref, b_ref, send_sems, recv_sems):
        x, y, c, chips = _where_am_i()
        cps = [_rcopy(p_ref.at[2 * cx + cy], b_ref.at[j], send_sems, recv_sems, j, (cx, cy, c))
               for j, (cx, cy) in enumerate(chips)]
        for cp in cps:
            cp.start()
        for cp in cps:
            cp.wait()

    return _comm_call(body, name, jax.ShapeDtypeStruct((3, rh, C), p.dtype), 3, 1)(p)


def _d2d_swap(q, name):
    R, C = q.shape
    rh = R // 2

    def body(q_ref, o_ref, send_sems, recv_sems):
        del q_ref
        x, y, c, _ = _where_am_i()
        mine = o_ref.at[pl.ds(c * rh, rh)]
        cp = _rcopy(mine, mine, send_sems, recv_sems, 0, (x, y, 1 - c))
        cp.start()
        cp.wait()

    return pl.pallas_call(body, name=name, out_shape=jax.ShapeDtypeStruct((R, C), q.dtype), in_specs=[_ANY],
                          out_specs=_ANY, input_output_aliases={0: 0},
                          scratch_shapes=[pltpu.SemaphoreType.DMA((1,)), pltpu.SemaphoreType.DMA((1,))])(q)


def _add_halves(g, a, c_idx, name):
    _, R, C = g.shape
    rh = R // 2
    tr = _best_div(rh, max(16, _SLAB_BLOCK_ELEMS // C), 16)
    nb = rh // tr

    def body(c_ref, g_ref, a_ref, o_ref):
        del c_ref
        o_ref[...] = (g_ref[...] + a_ref[...]).astype(o_ref.dtype)

    return _pcall(body, name=name, grid=(4, nb), prefetch=1,
                  in_specs=[pl.BlockSpec((1, tr, C), lambda j, i, c: (j, c[0] * nb + i, 0)),
                            pl.BlockSpec((1, tr, C), lambda j, i, c: (j, i, 0))],
                  out_specs=pl.BlockSpec((1, tr, C), lambda j, i, c: (j, i, 0)),
                  out_shape=jax.ShapeDtypeStruct((4, rh, C), _WIRE), sem=("parallel", "parallel"))(c_idx, g, a)


def _add_chips(p, b, ck_idx, name):
    _, rh, C = p.shape
    tr = _best_div(rh, max(16, _SLAB_BLOCK_ELEMS // C), 16)
    nb = rh // tr

    def body(ck_ref, p_ref, b_ref, o_ref):
        del ck_ref
        o_ref[...] = ((p_ref[0].astype(f32) + b_ref[0].astype(f32)) + b_ref[1].astype(f32)) + b_ref[2].astype(f32)

    return _pcall(body, name=name, grid=(nb,), prefetch=1,
                  in_specs=[pl.BlockSpec((1, tr, C), lambda i, ck: (ck[1], i, 0)),
                            pl.BlockSpec((3, tr, C), lambda i, ck: (0, i, 0))],
                  out_specs=pl.BlockSpec((tr, C), lambda i, ck: (ck[0] * nb + i, 0)),
                  out_shape=jax.ShapeDtypeStruct((2 * rh, C), f32), sem=("parallel",))(ck_idx, p, b)


def _reduce_scatter(g, c_idx, ck_idx, name):
    a = _d2d_exchange(g, name + "_d2d")
    p = _add_halves(g, a, c_idx, name + "_add2")
    b = _ici_scatter(p, name + "_ici")
    q = _add_chips(p, b, ck_idx, name + "_add4")
    return _d2d_swap(q, name + "_swap")


def _allreduce_small(buf, name):
    R, C = buf.shape

    def body(x_ref, o_ref, gat_ref, send_sems, recv_sems):
        x, y, c, _ = _where_am_i()
        me = 4 * x + 2 * y + c
        gat_ref[me] = x_ref[...]
        cps = []
        for f in range(1, 8):
            fx, fy, fc = (f >> 2) & 1, (f >> 1) & 1, f & 1
            peer = ((1 - x) if fx else x, (1 - y) if fy else y, (1 - c) if fc else c)
            cps.append(_rcopy(x_ref, gat_ref.at[me], send_sems, recv_sems, f - 1, peer))
        for cp in cps:
            cp.start()
        for cp in cps:
            cp.wait()
        acc = gat_ref[0]
        for d in range(1, 8):
            acc = acc + gat_ref[d]
        o_ref[...] = acc

    vm = pl.BlockSpec(memory_space=pltpu.VMEM)
    return pl.pallas_call(body, name=name, out_shape=jax.ShapeDtypeStruct((R, C), f32), in_specs=[vm], out_specs=vm,
                          scratch_shapes=[pltpu.VMEM((8, R, C), f32), pltpu.SemaphoreType.DMA((7,)),
                                          pltpu.SemaphoreType.DMA((7,))],
                          compiler_params=pltpu.CompilerParams(vmem_limit_bytes=_VMEM_LIMIT))(buf)


def _pack(arrs, cols=_COMM_COLS):
    flat = jnp.concatenate([a.reshape(-1).astype(f32) for a in arrs])
    per = 8 * cols
    n = -(-flat.shape[0] // per) * per
    return jnp.pad(flat, (0, n - flat.shape[0])).reshape(n // cols, cols)


def _unpack(buf, shapes):
    flat = buf.reshape(-1)
    out, off = [], 0
    for s in shapes:
        n = math.prod(s)
        out.append(flat[off:off + n].reshape(s))
        off += n
    return out


_WEIGHTS = ['norm_mix_w', 'norm_ffn_w', 'norm_final_w', 'ev_w_in', 'lru_conv_w', 'lru_conv_b', 'lru_w_r', 'lru_b_r',
            'lru_w_i', 'lru_b_i', 'lru_lambda', 'hg_lower_bounds', 'hg_norm_w', 'ev_w_out', 'ssd_w_in', 'ssd_conv_w',
            'ssd_conv_b', 'ssd_dt_bias', 'ssd_a_log', 'ssd_d', 'ssd_norm_w', 'ssd_w_out', 'ffn_w_up', 'ffn_conv_w',
            'ffn_conv_b', 'ffn_w_down']
_BIG = {'ev_w_in': 2, 'ev_w_out': 1, 'ssd_w_in': 2, 'ssd_w_out': 1, 'ffn_w_up': 2, 'ffn_w_down': 1}
_SMALL_SHARDED = {'lru_conv_w': 2, 'ssd_conv_w': 2, 'ssd_conv_b': 1, 'ssd_norm_w': 1, 'ffn_conv_w': 2}


def _full_shape(shape, ax):
    return tuple(4 * d if i == ax else d for i, d in enumerate(shape))


def kernel(x, norm_mix_w, norm_ffn_w, norm_final_w, ev_w_in, lru_conv_w, lru_conv_b, lru_w_r, lru_b_r, lru_w_i, lru_b_i, lru_lambda, hg_lower_bounds, hg_norm_w, ev_w_out, ssd_w_in, ssd_conv_w, ssd_conv_b, ssd_dt_bias, ssd_a_log, ssd_d, ssd_norm_w, ssd_w_out, ffn_w_up, ffn_conv_w, ffn_conv_b, ffn_w_down, loss_target, m_norm_mix_w, m_norm_ffn_w, m_norm_final_w, m_ev_w_in, m_lru_conv_w, m_lru_conv_b, m_lru_w_r, m_lru_b_r, m_lru_w_i, m_lru_b_i, m_lru_lambda, m_hg_lower_bounds, m_hg_norm_w, m_ev_w_out, m_ssd_w_in, m_ssd_conv_w, m_ssd_conv_b, m_ssd_dt_bias, m_ssd_a_log, m_ssd_d, m_ssd_norm_w, m_ssd_w_out, m_ffn_w_up, m_ffn_conv_w, m_ffn_conv_b, m_ffn_w_down, v_norm_mix_w, v_norm_ffn_w, v_norm_final_w, v_ev_w_in, v_lru_conv_w, v_lru_conv_b, v_lru_w_r, v_lru_b_r, v_lru_w_i, v_lru_b_i, v_lru_lambda, v_hg_lower_bounds, v_hg_norm_w, v_ev_w_out, v_ssd_w_in, v_ssd_conv_w, v_ssd_conv_b, v_ssd_dt_bias, v_ssd_a_log, v_ssd_d, v_ssd_norm_w, v_ssd_w_out, v_ffn_w_up, v_ffn_conv_w, v_ffn_conv_b, v_ffn_w_down):
    given = dict(locals())
    w = {n: given[n] for n in _WEIGHTS}
    m = {n: given["m_" + n] for n in _WEIGHTS}
    v = {n: given["v_" + n] for n in _WEIGHTS}
    cx, cy, cc = lax.axis_index("x"), lax.axis_index("y"), lax.axis_index("c")
    chip = 2 * cx + cy
    c_idx = jnp.reshape(cc, (1,)).astype(jnp.int32)
    ck_idx = jnp.stack([cc, chip]).astype(jnp.int32)

    def rows2d(a):
        return a.reshape(-1, a.shape[-1])

    full = dict(w)
    for n, ax in _BIG.items():
        own = rows2d(w[n].astype(_MXU))
        gat = _allgather_chips(own, "gather_" + n)
        gat = lax.dynamic_update_index_in_dim(gat, own, chip, 0)
        blocks = gat.reshape((4,) + w[n].shape)
        full[n] = jnp.moveaxis(blocks, 0, ax).reshape(_full_shape(w[n].shape, ax))
    placed = []
    for n, ax in _SMALL_SHARDED.items():
        z = jnp.zeros(_full_shape(w[n].shape, ax), f32)
        z = lax.dynamic_update_slice_in_dim(z, w[n].astype(f32), chip * w[n].shape[ax], ax)
        placed.append(jnp.where(cc == 0, z, 0.0))
    small_full = _unpack(_allreduce_small(_pack(placed), "gather_small"), [p.shape for p in placed])
    for n, a in zip(_SMALL_SHARDED, small_full):
        full[n] = a

    loss_part, dx, grads = _local_step(x[0], loss_target[0], full)

    small = [n for n in _WEIGHTS if n not in _BIG]
    red = _allreduce_small(_pack([loss_part.reshape(1)] + [grads[n] for n in small]), "allreduce_small")
    red = _unpack(red, [(1,)] + [grads[n].shape for n in small])
    loss = red[0][0]
    gsum = {}
    for n, a in zip(small, red[1:]):
        if n in _SMALL_SHARDED:
            ax = _SMALL_SHARDED[n]
            a = lax.dynamic_slice_in_dim(a, chip * w[n].shape[ax], w[n].shape[ax], ax)
        gsum[n] = a
    outs = _adamw(_pack([w[n] for n in small]), _pack([gsum[n] for n in small]), _pack([m[n] for n in small]),
                  _pack([v[n] for n in small]), "adamw_small")
    shapes = [w[n].shape for n in small]
    delta, new_m, new_v = ({n: a for n, a in zip(small, _unpack(o, shapes))} for o in outs)

    for n, ax in _BIG.items():
        shp = w[n].shape
        g4 = jnp.moveaxis(grads[n].reshape(shp[:ax] + (4, shp[ax]) + shp[ax + 1:]), ax, 0)
        gs = _reduce_scatter(g4.reshape(4, -1, shp[-1]), c_idx, ck_idx, "rs_" + n)
        d_, m_, v_ = _adamw(rows2d(w[n]), gs, rows2d(m[n]), rows2d(v[n]), "adamw_" + n)
        gsum[n] = gs.reshape(shp)
        delta[n], new_m[n], new_v[n] = d_.reshape(shp), m_.reshape(shp), v_.reshape(shp)

    return (loss, dx[None], *[gsum[n] for n in _WEIGHTS], *[delta[n] for n in _WEIGHTS],
            *[new_m[n] for n in _WEIGHTS], *[new_v[n] for n in _WEIGHTS])
```
